```python
import jax
import jax.numpy as jnp
from jax import lax
import numpy as np

D_MODEL = 1024
BATCH = 16
SEQ = 2048
DEPTH = 2

GRID_W = 64
CTX_LEN = 256
MLA_HEADS = 8
QK_NOPE = 64
QK_ROPE = 32
V_HEAD = 64
Q_LORA = 256
KV_LORA = 128
Q_BLOCK = 128
CONV_CH = 256
CONV_WIDTH = 31
ML_HEADS = 4
ML_HEAD_DIM = 64
ML_WIDTH = ML_HEADS * ML_HEAD_DIM
MLSTM_CHUNK = 128
D_FF = 4 * D_MODEL
MIX_WIDTH = MLA_HEADS * V_HEAD + CONV_CH + ML_WIDTH
ROPE_THETA = 10000.0
LN_EPS = 1e-5
RMS_EPS = 1e-6
DEEPNORM_ALPHA = (2 * DEPTH) ** 0.25
DEEPNORM_BETA = (8 * DEPTH) ** -0.25
IN_SIZES = (Q_LORA, KV_LORA, QK_ROPE, 2 * CONV_CH, ML_WIDTH, ML_WIDTH, ML_WIDTH, ML_WIDTH, 4 * ML_HEADS)
IN_COLS = sum(IN_SIZES)

kernel_name = "hybrid_mla_conformer_mlstm_dit_block"


def layer_norm(x, g, b):
    xf = x.astype(jnp.float32)
    mu = jnp.mean(xf, -1, keepdims=True)
    var = jnp.mean(jnp.square(xf - mu), -1, keepdims=True)
    return ((xf - mu) * lax.rsqrt(var + LN_EPS) * g + b).astype(x.dtype)


def rms_norm(x, g):
    xf = x.astype(jnp.float32)
    return (xf * lax.rsqrt(jnp.mean(jnp.square(xf), -1, keepdims=True) + RMS_EPS) * g).astype(x.dtype)


def split_cols(p):
    parts, off = [], 0
    for size in IN_SIZES:
        parts.append(p[..., off:off + size])
        off += size
    return parts


def axial_rope(t):
    n = t.shape[-2]
    rows = n // GRID_W
    pos = jnp.arange(rows * GRID_W)
    row, col = pos // GRID_W, pos % GRID_W
    half = QK_ROPE // 2
    nf = half // 2
    inv_freq = ROPE_THETA ** (-jnp.arange(nf, dtype=jnp.float32) / nf)

    def rot(xa, p):
        ang = p.astype(jnp.float32)[:, None] * inv_freq
        cos, sin = jnp.cos(ang), jnp.sin(ang)
        x1 = xa[..., :nf].astype(jnp.float32)
        x2 = xa[..., nf:].astype(jnp.float32)
        return jnp.concatenate([x1 * cos - x2 * sin, x1 * sin + x2 * cos], -1)

    out = jnp.concatenate([rot(t[..., :half], row), rot(t[..., half:], col)], -1)
    return out.astype(t.dtype)


def mla_project(cq, ckv, kr, g_qn, w_uq, g_kvn, w_ukv, with_rope):
    B, L, _ = cq.shape
    q = (rms_norm(cq, g_qn) @ w_uq).reshape(B, L, MLA_HEADS, QK_NOPE + QK_ROPE).transpose(0, 2, 1, 3)
    kv = (rms_norm(ckv, g_kvn) @ w_ukv).reshape(B, L, MLA_HEADS, QK_NOPE + V_HEAD).transpose(0, 2, 1, 3)
    q_nope, q_rope = q[..., :QK_NOPE], q[..., QK_NOPE:]
    k_nope, v = kv[..., :QK_NOPE], kv[..., QK_NOPE:]
    k_rope = kr[:, None]
    if with_rope:
        q_rope = axial_rope(q_rope)
        k_rope = axial_rope(k_rope)
    k_rope = jnp.broadcast_to(k_rope, (B, MLA_HEADS, L, QK_ROPE))
    return (jnp.concatenate([q_nope, q_rope], -1), jnp.concatenate([k_nope, k_rope], -1), v)


def attend(q, k, v):
    scale = (QK_NOPE + QK_ROPE) ** -0.5
    s = jnp.einsum('bhqd,bhkd->bhqk', q, k).astype(jnp.float32) * scale
    p = jax.nn.softmax(s, axis=-1).astype(v.dtype)
    return jnp.einsum('bhqk,bhkd->bhqd', p, v)


def blocked_attend(q, k, v):
    B, H, S, dq = q.shape
    nb = S // Q_BLOCK
    qb = q.reshape(B, H, nb, Q_BLOCK, dq).transpose(2, 0, 1, 3, 4)
    ob = lax.map(lambda qq: attend(qq, k, v), qb)
    return ob.transpose(1, 2, 0, 3, 4).reshape(B, H, S, -1)


def merge_heads(o):
    B, H, L, d = o.shape
    return o.transpose(0, 2, 1, 3).reshape(B, L, H * d)


def conformer_conv(u, w, b, ln_g, ln_b):
    a, gt = u[..., :CONV_CH], u[..., CONV_CH:]
    z = a * jax.nn.sigmoid(gt)
    z = lax.conv_general_dilated(
        z, w[:, None, :].astype(z.dtype), window_strides=(1,),
        padding=[(CONV_WIDTH // 2, CONV_WIDTH // 2)],
        dimension_numbers=('NWC', 'WIO', 'NWC'), feature_group_count=CONV_CH) + b
    return jax.nn.silu(layer_norm(z, ln_g, ln_b))


def mlstm_chunkwise(q, k, v, log_i, log_f, state):
    B, H, L, d = q.shape
    T = MLSTM_CHUNK
    nc = L // T
    tril = jnp.tril(jnp.ones((T, T), dtype=bool))

    def chunks(t):
        return jnp.moveaxis(t.astype(jnp.float32).reshape(t.shape[:2] + (nc, T) + t.shape[3:]), 2, 0)

    def step(carry, inp):
        C, n, m = carry
        qc, kc, vc, lic, lfc = inp
        b = jnp.cumsum(lfc, axis=-1)
        dmat = b[..., :, None] - b[..., None, :] + lic[..., None, :]
        dmat = jnp.where(tril, dmat, -jnp.inf)
        m_inter = b + m[..., None]
        m_t = jnp.maximum(m_inter, jnp.max(dmat, axis=-1))
        wts = jnp.exp(dmat - m_t[..., None])
        inter = jnp.exp(m_inter - m_t)
        sqk = jnp.einsum('bhtd,bhsd->bhts', qc, kc) * wts
        num = jnp.einsum('bhts,bhsd->bhtd', sqk, vc) + inter[..., None] * jnp.einsum('bhvk,bhtk->bhtv', C, qc)
        den = jnp.sum(sqk, -1) + inter * jnp.einsum('bhk,bhtk->bht', n, qc)
        h = num / jnp.maximum(jnp.abs(den), jnp.exp(-m_t))[..., None]
        b_last = b[..., -1]
        g = b_last[..., None] - b + lic
        m_new = jnp.maximum(b_last + m, jnp.max(g, axis=-1))
        w_upd = jnp.exp(g - m_new[..., None])
        decay = jnp.exp(b_last + m - m_new)
        C_new = decay[..., None, None] * C + jnp.einsum('bhs,bhsv,bhsk->bhvk', w_upd, vc, kc)
        n_new = decay[..., None] * n + jnp.einsum('bhs,bhsk->bhk', w_upd, kc)
        return (C_new, n_new, m_new), h

    final, hs = lax.scan(step, state, (chunks(q), chunks(k), chunks(v), chunks(log_i), chunks(log_f)))
    return jnp.moveaxis(hs, 0, 2).reshape(B, H, L, d), final


def to_heads(t):
    B, L, _ = t.shape
    return t.reshape(B, L, ML_HEADS, ML_HEAD_DIM).transpose(0, 2, 1, 3)


def gate_logs(pre, b_gates):
    g = jnp.moveaxis((pre + b_gates).astype(jnp.float32), -1, 1)
    i_f, f_f, i_b, f_b = jnp.split(g, 4, axis=1)
    return i_f, jax.nn.log_sigmoid(f_f), i_b, jax.nn.log_sigmoid(f_b)


def mlstm_bidirectional(q_l, k_l, v_l, gate_l, q_c, k_c, v_c, gate_c, b_gates):
    B = q_l.shape[0]
    ks = ML_HEAD_DIM ** -0.5
    ql, kl, vl = to_heads(q_l), to_heads(k_l) * ks, to_heads(v_l)
    qc, kc, vc = to_heads(q_c), to_heads(k_c) * ks, to_heads(v_c)
    il_f, fl_f, il_b, fl_b = gate_logs(gate_l, b_gates)
    ic_f, fc_f, ic_b, fc_b = gate_logs(gate_c, b_gates)
    zero = (jnp.zeros((B, ML_HEADS, ML_HEAD_DIM, ML_HEAD_DIM), jnp.float32),
            jnp.zeros((B, ML_HEADS, ML_HEAD_DIM), jnp.float32),
            jnp.zeros((B, ML_HEADS), jnp.float32))
    hc_f, st_f = mlstm_chunkwise(qc, kc, vc, ic_f, fc_f, zero)
    hl_f, _ = mlstm_chunkwise(ql, kl, vl, il_f, fl_f, st_f)
    flip = lambda t: jnp.flip(t, axis=2)
    hc_b, st_b = mlstm_chunkwise(flip(qc), flip(kc), flip(vc), flip(ic_b), flip(fc_b), zero)
    hl_b, _ = mlstm_chunkwise(flip(ql), flip(kl), flip(vl), flip(il_b), flip(fl_b), st_b)
    return hl_f + flip(hl_b), hc_f + flip(hc_b)


def mlstm_out(h, o_pre, g):
    B, H, L, d = h.shape
    hh = h.transpose(0, 2, 1, 3) * jax.nn.sigmoid(o_pre.astype(jnp.float32)).reshape(B, L, H, d)
    mu = jnp.mean(hh, -1, keepdims=True)
    var = jnp.mean(jnp.square(hh - mu), -1, keepdims=True)
    hn = (hh - mu) * lax.rsqrt(var + LN_EPS)
    return (hn.reshape(B, L, H * d) * g).astype(o_pre.dtype)


def sqrelu_mlp(u, w1, b1, w2, b2):
    h = jax.nn.relu(u @ w1 + b1)
    return (h * h) @ w2 + b2


def trunk_layer(x, ctx, c, c_ctx, w_ada, b_ada, w_in, g_qn, w_uq, g_kvn, w_ukv,
                conv_w, conv_b, conv_ln_g, conv_ln_b, b_gates, ml_norm_g, w_out,
                ln1_g, ln1_b, w_mlp1, b_mlp1, w_mlp2, b_mlp2, ln2_g, ln2_b, is_last):
    mod = jax.nn.silu(c) @ w_ada + b_ada
    sh1, sc1, g1, sh2, sc2, g2 = jnp.split(mod[:, None, :], 6, axis=-1)
    cmod = jax.nn.silu(c_ctx) @ w_ada + b_ada
    csh1, csc1, cg1, csh2, csc2, cg2 = jnp.split(cmod, 6, axis=-1)

    cq, ckv, kr, conv_in, mq, mk, mv, mo, mg = split_cols((x * (1 + sc1) + sh1) @ w_in)
    ccq, cckv, ckr, cconv_in, cmq, cmk, cmv, cmo, cmg = split_cols((ctx * (1 + csc1) + csh1) @ w_in)

    q_l, k_l, v_l = mla_project(cq, ckv, kr, g_qn, w_uq, g_kvn, w_ukv, True)
    q_c, k_c, v_c = mla_project(ccq, cckv, ckr, g_qn, w_uq, g_kvn, w_ukv, False)
    att = merge_heads(blocked_attend(q_l, jnp.concatenate([k_l, k_c], 2), jnp.concatenate([v_l, v_c], 2)))
    conv = conformer_conv(conv_in, conv_w, conv_b, conv_ln_g, conv_ln_b)
    h_l, h_c = mlstm_bidirectional(mq, mk, mv, mg, cmq, cmk, cmv, cmg, b_gates)
    ml = mlstm_out(h_l, mo, ml_norm_g)

    y = jnp.concatenate([att, conv, ml], -1) @ w_out
    x = layer_norm(DEEPNORM_ALPHA * x + g1 * y, ln1_g, ln1_b)
    x = layer_norm(DEEPNORM_ALPHA * x + g2 * sqrelu_mlp(x * (1 + sc2) + sh2, w_mlp1, b_mlp1, w_mlp2, b_mlp2),
                   ln2_g, ln2_b)

    if not is_last:
        att_c = merge_heads(attend(q_c, k_c, v_c))
        conv_c = conformer_conv(cconv_in, conv_w, conv_b, conv_ln_g, conv_ln_b)
        ml_c = mlstm_out(h_c, cmo, ml_norm_g)
        yc = jnp.concatenate([att_c, conv_c, ml_c], -1) @ w_out
        ctx = layer_norm(DEEPNORM_ALPHA * ctx + cg1 * yc, ln1_g, ln1_b)
        ctx = layer_norm(DEEPNORM_ALPHA * ctx + cg2 * sqrelu_mlp(ctx * (1 + csc2) + csh2, w_mlp1, b_mlp1,
                                                                 w_mlp2, b_mlp2), ln2_g, ln2_b)
    return x, ctx


def setup_inputs(seed: int = 0) -> dict:
    key = jax.random.key(seed)
    ks = iter(jax.random.split(key, 32))

    def nrm(shape, scale):
        return jax.random.normal(next(ks), shape, jnp.float32) * scale

    L = DEPTH
    D = D_MODEL
    x = nrm((BATCH, SEQ, D), 1.0)
    c = nrm((BATCH, D), 1.0)
    ctx = nrm((BATCH, CTX_LEN, D), 1.0)
    c_ctx = nrm((D,), 1.0)
    w_ada = nrm((L, D, 6 * D), D ** -0.5)
    b_ada = nrm((L, 6 * D), 0.02)
    w_in = nrm((L, D, IN_COLS), D ** -0.5)
    g_qn = 1.0 + nrm((L, Q_LORA), 0.05)
    w_uq = nrm((L, Q_LORA, MLA_HEADS * (QK_NOPE + QK_ROPE)), Q_LORA ** -0.5)
    g_kvn = 1.0 + nrm((L, KV_LORA), 0.05)
    w_ukv = nrm((L, KV_LORA, MLA_HEADS * (QK_NOPE + V_HEAD)), KV_LORA ** -0.5)
    conv_w = nrm((L, CONV_WIDTH, CONV_CH), CONV_WIDTH ** -0.5)
    conv_b = nrm((L, CONV_CH), 0.02)
    conv_ln_g = 1.0 + nrm((L, CONV_CH), 0.05)
    conv_ln_b = nrm((L, CONV_CH), 0.02)
    f_bias = jnp.linspace(3.0, 6.0, ML_HEADS)
    z = jnp.zeros((ML_HEADS,), jnp.float32)
    b_gates = jnp.concatenate([z, f_bias, z, f_bias])[None, :] + nrm((L, 4 * ML_HEADS), 0.1)
    ml_norm_g = 1.0 + nrm((L, ML_WIDTH), 0.05)
    w_out = nrm((L, MIX_WIDTH, D), MIX_WIDTH ** -0.5 * DEEPNORM_BETA)
    ln1_g = 1.0 + nrm((L, D), 0.05)
    ln1_b = nrm((L, D), 0.02)
    w_mlp1 = nrm((L, D, D_FF), D ** -0.5)
    b_mlp1 = nrm((L, D_FF), 0.02)
    w_mlp2 = nrm((L, D_FF, D), D_FF ** -0.5 * DEEPNORM_BETA)
    b_mlp2 = nrm((L, D), 0.02)
    ln2_g = 1.0 + nrm((L, D), 0.05)
    ln2_b = nrm((L, D), 0.02)
    return {"x": x, "c": c, "ctx": ctx, "c_ctx": c_ctx, "w_ada": w_ada, "b_ada": b_ada, "w_in": w_in,
            "g_qn": g_qn, "w_uq": w_uq, "g_kvn": g_kvn, "w_ukv": w_ukv, "conv_w": conv_w,
            "conv_b": conv_b, "conv_ln_g": conv_ln_g, "conv_ln_b": conv_ln_b, "b_gates": b_gates,
            "ml_norm_g": ml_norm_g, "w_out": w_out, "ln1_g": ln1_g, "ln1_b": ln1_b, "w_mlp1": w_mlp1,
            "b_mlp1": b_mlp1, "w_mlp2": w_mlp2, "b_mlp2": b_mlp2, "ln2_g": ln2_g, "ln2_b": ln2_b}


def reference(x, c, ctx, c_ctx, w_ada, b_ada, w_in, g_qn, w_uq, g_kvn, w_ukv, conv_w, conv_b,
              conv_ln_g, conv_ln_b, b_gates, ml_norm_g, w_out, ln1_g, ln1_b, w_mlp1, b_mlp1,
              w_mlp2, b_mlp2, ln2_g, ln2_b):
    for l in range(DEPTH):
        x, ctx = trunk_layer(x, ctx, c, c_ctx, w_ada[l], b_ada[l], w_in[l], g_qn[l], w_uq[l], g_kvn[l],
                             w_ukv[l], conv_w[l], conv_b[l], conv_ln_g[l], conv_ln_b[l], b_gates[l],
                             ml_norm_g[l], w_out[l], ln1_g[l], ln1_b[l], w_mlp1[l], b_mlp1[l], w_mlp2[l],
                             b_mlp2[l], ln2_g[l], ln2_b[l], l == DEPTH - 1)
    return x
```

```python
import functools

import numpy as np
import jax
import jax.numpy as jnp
from jax import lax
from jax.experimental import pallas as pl
from jax.experimental.pallas import tpu as pltpu

F32 = jnp.float32
BF16 = jnp.bfloat16

LANES = 128
SUBLANES = 8
VMEM_LIMIT_BYTES = 56 * 1024 * 1024

D_MODEL = 1024
GRID_W = 64
CTX_LEN = 256
MLA_HEADS = 8
QK_NOPE = 64
QK_ROPE = 32
V_HEAD = 64
Q_LORA = 256
KV_LORA = 128
CONV_CH = 256
CONV_WIDTH = 31
ML_HEADS = 4
ML_HEAD_DIM = 64
ML_WIDTH = ML_HEADS * ML_HEAD_DIM
CHUNK = 128
D_FF = 4 * D_MODEL
ROPE_THETA = 10000.0
LN_EPS = 1e-5
RMS_EPS = 1e-6
DEPTH = 2
DEEPNORM_ALPHA = (2 * DEPTH) ** 0.25

TM = 256
HALO = 16
HEAD_PAD = LANES

C_CQ = 0
C_CKV = C_CQ + Q_LORA
C_KR = C_CKV + KV_LORA
C_CONV = C_KR + LANES
C_MQ = C_CONV + 2 * CONV_CH
C_MK = C_MQ + ML_WIDTH
C_MV = C_MK + ML_WIDTH
C_MO = C_MV + ML_WIDTH
C_GI = C_MO + ML_WIDTH
C_GF = C_GI + LANES
IN_PAD = C_GF + LANES
N_GATE = 2 * ML_HEADS


def _layer_norm(v, g, b):
    mu = jnp.mean(v, -1, keepdims=True)
    d = v - mu
    var = jnp.mean(d * d, -1, keepdims=True)
    return d * lax.rsqrt(var + LN_EPS) * g + b


def _rms_norm(v, g):
    return v * lax.rsqrt(jnp.mean(v * v, -1, keepdims=True) + RMS_EPS) * g


def _sigmoid(v):
    return 1.0 / (1.0 + jnp.exp(-v))


def _log_sigmoid(v):
    return jnp.minimum(v, 0.0) - jnp.log1p(jnp.exp(-jnp.abs(v)))


def _rope(t, cos, sin_lo, sin_hi):
    return t * cos + pltpu.roll(t, LANES - 8, 1) * sin_lo + pltpu.roll(t, 8, 1) * sin_hi


def _ada_kernel(c_ref, w_ref, b_ref, o_ref):
    c = c_ref[...]
    s = (c * _sigmoid(c)).astype(BF16)
    o_ref[0] = jnp.dot(s, w_ref[0].astype(BF16), preferred_element_type=F32) + b_ref[0]


def _ada_call(cc, w_ada, b_ada):
    depth, d, n6 = w_ada.shape
    rows = cc.shape[0]
    nblk = n6 // d
    return pl.pallas_call(
        _ada_kernel,
        out_shape=jax.ShapeDtypeStruct((depth, rows, n6), F32),
        grid=(depth, nblk),
        in_specs=[
            pl.BlockSpec((rows, d), lambda l, n: (0, 0)),
            pl.BlockSpec((1, d, d), lambda l, n: (l, 0, n)),
            pl.BlockSpec((1, 1, d), lambda l, n: (l, 0, n)),
        ],
        out_specs=pl.BlockSpec((1, rows, d), lambda l, n: (l, 0, n)),
        compiler_params=pltpu.CompilerParams(dimension_semantics=("parallel", "parallel")),
        name="ada_mod",
    )(cc, w_ada, b_ada.reshape(depth, 1, n6))


def _in_kernel(x_ref, mod_ref, w_ref, gq_ref, gkv_ref, wuq_ref, wuk_ref, wuv_ref, bgi_ref, bgf_ref,
               cos_ref, slo_ref, shi_ref,
               q_ref, k_ref, v_ref, z_ref, mq_ref, mk_ref, mv_ref, mo_ref,
               gic_ref, gfc_ref, gir_ref, gfr_ref):
    x = x_ref[0]
    sh1 = mod_ref[0, 0:1, :]
    sc1 = mod_ref[0, 1:2, :]
    xm = (x * (1.0 + sc1) + sh1).astype(BF16)

    def proj(lo, hi):
        return jnp.dot(xm, w_ref[:, lo:hi], preferred_element_type=F32)

    cos = cos_ref[...]
    slo = slo_ref[...]
    shi = shi_ref[...]

    cqn = _rms_norm(proj(C_CQ, C_CQ + Q_LORA), gq_ref[...]).astype(BF16)
    qf = jnp.dot(cqn, wuq_ref[...], preferred_element_type=F32) * ((QK_NOPE + QK_ROPE) ** -0.5)
    for h in range(MLA_HEADS):
        sl = slice(h * HEAD_PAD, (h + 1) * HEAD_PAD)
        q_ref[0, :, sl] = _rope(qf[:, sl], cos, slo, shi).astype(BF16)

    ckvn = _rms_norm(proj(C_CKV, C_CKV + KV_LORA), gkv_ref[...]).astype(BF16)
    kr = _rope(proj(C_KR, C_KR + LANES), cos, slo, shi)
    kf = jnp.dot(ckvn, wuk_ref[...], preferred_element_type=F32)
    for h in range(MLA_HEADS):
        sl = slice(h * HEAD_PAD, (h + 1) * HEAD_PAD)
        k_ref[0, :, sl] = (kf[:, sl] + kr).astype(BF16)
    v_ref[0] = jnp.dot(ckvn, wuv_ref[...], preferred_element_type=F32).astype(BF16)

    a = proj(C_CONV, C_CONV + CONV_CH)
    gt = proj(C_CONV + CONV_CH, C_CONV + 2 * CONV_CH)
    z_ref[0] = a * _sigmoid(gt)

    mq_ref[0] = proj(C_MQ, C_MQ + ML_WIDTH).astype(BF16)
    mk_ref[0] = (proj(C_MK, C_MK + ML_WIDTH) * (ML_HEAD_DIM ** -0.5)).astype(BF16)
    mv_ref[0] = proj(C_MV, C_MV + ML_WIDTH).astype(BF16)
    mo_ref[0] = proj(C_MO, C_MO + ML_WIDTH)

    gi = proj(C_GI, C_GI + LANES) + bgi_ref[...]
    gf = _log_sigmoid(proj(C_GF, C_GF + LANES) + bgf_ref[...])
    gic_ref[0] = gi[:, :N_GATE]
    gfc_ref[0] = gf[:, :N_GATE]
    gi_t = gi.T
    gf_t = gf.T
    for cc in range(TM // CHUNK):
        gir_ref[0, cc] = gi_t[:N_GATE, cc * CHUNK:(cc + 1) * CHUNK]
        gfr_ref[0, cc] = gf_t[:N_GATE, cc * CHUNK:(cc + 1) * CHUNK]


def _in_call(xx, mod, w_in_p, gq, gkv, wuq, wuk, wuv, bgi, bgf, cos, slo, shi):
    b, ltot, d = xx.shape
    nt = ltot // TM
    nc = ltot // CHUNK
    ctx_row = mod.shape[0] - SUBLANES

    def tok(width):
        return pl.BlockSpec((1, TM, width), lambda i, t: (i, t, 0))

    def const2(shape):
        return pl.BlockSpec(shape, lambda i, t: (0, 0))

    tab = pl.BlockSpec((TM, LANES), lambda i, t: (t, 0))
    grow = pl.BlockSpec((1, TM // CHUNK, N_GATE, CHUNK), lambda i, t: (i, t, 0, 0))
    outs = [
        (jax.ShapeDtypeStruct((b, ltot, MLA_HEADS * HEAD_PAD), BF16), tok(MLA_HEADS * HEAD_PAD)),
        (jax.ShapeDtypeStruct((b, ltot, MLA_HEADS * HEAD_PAD), BF16), tok(MLA_HEADS * HEAD_PAD)),
        (jax.ShapeDtypeStruct((b, ltot, MLA_HEADS * V_HEAD), BF16), tok(MLA_HEADS * V_HEAD)),
        (jax.ShapeDtypeStruct((b, ltot, CONV_CH), F32), tok(CONV_CH)),
        (jax.ShapeDtypeStruct((b, ltot, ML_WIDTH), BF16), tok(ML_WIDTH)),
        (jax.ShapeDtypeStruct((b, ltot, ML_WIDTH), BF16), tok(ML_WIDTH)),
        (jax.ShapeDtypeStruct((b, ltot, ML_WIDTH), BF16), tok(ML_WIDTH)),
        (jax.ShapeDtypeStruct((b, ltot, ML_WIDTH), F32), tok(ML_WIDTH)),
        (jax.ShapeDtypeStruct((b, ltot, N_GATE), F32), tok(N_GATE)),
        (jax.ShapeDtypeStruct((b, ltot, N_GATE), F32), tok(N_GATE)),
        (jax.ShapeDtypeStruct((b, nc, N_GATE, CHUNK), F32), grow),
        (jax.ShapeDtypeStruct((b, nc, N_GATE, CHUNK), F32), grow),
    ]
    return pl.pallas_call(
        _in_kernel,
        out_shape=[o[0] for o in outs],
        grid=(b, nt),
        in_specs=[
            tok(d),
            pl.BlockSpec((1, 6, d), lambda i, t: (jnp.where(t == 0, ctx_row, i), 0, 0)),
            const2(w_in_p.shape),
            const2(gq.shape), const2(gkv.shape),
            const2(wuq.shape), const2(wuk.shape), const2(wuv.shape),
            const2(bgi.shape), const2(bgf.shape),
            tab, tab, tab,
        ],
        out_specs=[o[1] for o in outs],
        compiler_params=pltpu.CompilerParams(dimension_semantics=("parallel", "parallel"),
                                             vmem_limit_bytes=VMEM_LIMIT_BYTES),
        name="in_proj",
    )(xx, mod, w_in_p, gq, gkv, wuq, wuk, wuv, bgi, bgf, cos, slo, shi)


def _attn_kernel(q_ref, k_ref, v_ref, o_ref, *, t0, ltot):
    def run(nk):
        outs = []
        for hh in range(2):
            sl = slice(hh * HEAD_PAD, (hh + 1) * HEAD_PAD)
            s = lax.dot_general(q_ref[0, :, sl], k_ref[0, :nk, sl], (((1,), (1,)), ((), ())),
                                preferred_element_type=F32)
            m = jnp.max(s, -1, keepdims=True)
            p = jnp.exp(s - m)
            l = jnp.sum(p, -1, keepdims=True)
            o = jnp.dot(p.astype(BF16), v_ref[0, :nk, :], preferred_element_type=F32)
            outs.append(o / l)
        lane = lax.broadcasted_iota(jnp.int32, outs[0].shape, 1)
        o_ref[0] = jnp.where(lane < V_HEAD, outs[0], outs[1]).astype(o_ref.dtype)

    if t0 == 0:
        t = pl.program_id(2)

        @pl.when(t == 0)
        def _():
            run(CTX_LEN)

        @pl.when(t > 0)
        def _():
            run(ltot)
    else:
        run(ltot)


def _attn_call(q, k, v, t0):
    b, ltot, _ = q.shape
    nt = ltot // TM - t0
    npair = MLA_HEADS // 2
    return pl.pallas_call(
        functools.partial(_attn_kernel, t0=t0, ltot=ltot),
        out_shape=jax.ShapeDtypeStruct((b, nt * TM, MLA_HEADS * V_HEAD), BF16),
        grid=(b, npair, nt),
        in_specs=[
            pl.BlockSpec((1, TM, 2 * HEAD_PAD), lambda i, j, t: (i, t + t0, j)),
            pl.BlockSpec((1, ltot, 2 * HEAD_PAD), lambda i, j, t: (i, 0, j)),
            pl.BlockSpec((1, ltot, 2 * V_HEAD), lambda i, j, t: (i, 0, j)),
        ],
        out_specs=pl.BlockSpec((1, TM, 2 * V_HEAD), lambda i, j, t: (i, t, j)),
        compiler_params=pltpu.CompilerParams(dimension_semantics=("parallel", "parallel", "parallel"),
                                             vmem_limit_bytes=VMEM_LIMIT_BYTES),
        name="mla_attn",
    )(q, k, v)


def _mlstm_kernel(q_ref, k_ref, v_ref, mo_ref, gic_ref, gfc_ref, gir_ref, gfr_ref, g_ref,
                  o_ref, hf_ref, hb_ref, *, nchunk, nctx):
    t_ = CHUNK
    row = lax.broadcasted_iota(jnp.int32, (t_, t_), 0)
    col = lax.broadcasted_iota(jnp.int32, (t_, t_), 1)
    lower = col <= row
    upper = col >= row
    lower_f = lower.astype(F32)
    upper_f = upper.astype(F32)
    blockdiag = (row < ML_HEAD_DIM) == (col < ML_HEAD_DIM)
    half0 = lax.broadcasted_iota(jnp.int32, (1, LANES), 1) < ML_HEAD_DIM
    rowhalf0 = lax.broadcasted_iota(jnp.int32, (LANES, 1), 0) < ML_HEAD_DIM
    npair = ML_HEADS // 2
    hp = lax.Precision.HIGHEST

    def step(d, pair, c, state, h_ref):
        s_mat, n_vec, m0, m1 = state
        m_prev = (m0, m1)
        sl = slice(pair * LANES, (pair + 1) * LANES)
        q = q_ref[0, c, :, sl]
        k = k_ref[0, c, :, sl]
        v = v_ref[0, c, :, sl]
        gic = gic_ref[0, c]
        gfc = gfc_ref[0, c]
        gir = gir_ref[0, c]
        gfr = gfr_ref[0, c]
        tri_c, tri_r, causal = (lower_f, upper_f, lower) if d == 0 else (upper_f, lower_f, upper)
        bcol = jnp.dot(tri_c, gfc, preferred_element_type=F32, precision=hp)
        brow = jnp.dot(gfr, tri_r, preferred_element_type=F32, precision=hp)
        tot = jnp.sum(gfc, axis=0, keepdims=True)
        qf = q.astype(F32)
        zero_b = jnp.zeros_like(q)

        num_i = jnp.zeros((t_, LANES), F32)
        per_head = []
        for hh in range(2):
            ci = d * ML_HEADS + pair * 2 + hh
            mask_h = half0 if hh == 0 else jnp.logical_not(half0)
            bt = bcol[:, ci:ci + 1]
            dm = bt - brow[ci:ci + 1, :] + gir[ci:ci + 1, :]
            dm = jnp.where(causal, dm, -jnp.inf)
            m_intra = jnp.max(dm, -1, keepdims=True)
            m_inter = bt + m_prev[hh]
            m_t = jnp.maximum(m_inter, m_intra)
            wts = jnp.exp(dm - m_t)
            inter = jnp.exp(m_inter - m_t)
            qk = lax.dot_general(jnp.where(mask_h, q, zero_b), k, (((1,), (1,)), ((), ())),
                                 preferred_element_type=F32)
            sqk = qk * wts
            den_i = jnp.sum(sqk, -1, keepdims=True)
            num_i = num_i + jnp.dot(sqk.astype(BF16), jnp.where(mask_h, v, zero_b),
                                    preferred_element_type=F32)
            qn = jnp.sum(jnp.where(mask_h, qf * n_vec, 0.0), -1, keepdims=True)
            den = den_i + inter * qn
            tot_h = tot[:, ci:ci + 1]
            g_col = tot_h - bt + gic[:, ci:ci + 1]
            m_new = jnp.maximum(tot_h + m_prev[hh], jnp.max(g_col, axis=0, keepdims=True))
            w_col = jnp.exp(g_col - m_new)
            decay = jnp.exp(tot_h + m_prev[hh] - m_new)
            per_head.append((inter, den, m_t, w_col, decay, m_new))

        (i0, d0, t0_, w0, dc0, mn0), (i1, d1, t1_, w1, dc1, mn1) = per_head
        inter_m = jnp.where(half0, i0, i1)
        den_m = jnp.where(half0, d0, d1)
        mt_m = jnp.where(half0, t0_, t1_)
        num = num_i + inter_m * jnp.dot(q, s_mat.astype(BF16), preferred_element_type=F32)
        h_ref[c, :, sl] = num / jnp.maximum(jnp.abs(den_m), jnp.exp(-mt_m))

        wk = k.astype(F32) * jnp.where(half0, w0, w1)
        upd = jnp.dot(wk.T.astype(BF16), v, preferred_element_type=F32)
        s_new = jnp.where(rowhalf0, dc0, dc1) * s_mat + jnp.where(blockdiag, upd, 0.0)
        n_new = jnp.where(half0, dc0, dc1) * n_vec + jnp.sum(wk, axis=0, keepdims=True)
        return s_new, n_new, mn0, mn1

    def init():
        return (jnp.zeros((LANES, LANES), F32), jnp.zeros((1, LANES), F32),
                jnp.zeros((1, 1), F32), jnp.zeros((1, 1), F32))

    def body(i, carry):
        cb = jnp.where(i < nctx, nctx - 1 - i, nchunk - 1 + nctx - i)
        new = []
        for pair in range(npair):
            new.append(step(0, pair, i, carry[2 * pair], hf_ref))
            new.append(step(1, pair, cb, carry[2 * pair + 1], hb_ref))
        return tuple(new)

    lax.fori_loop(0, nchunk, body, tuple(init() for _ in range(2 * npair)))

    g = g_ref[...]

    def fin(c, _):
        hh = (hf_ref[c] + hb_ref[c]) * _sigmoid(mo_ref[0, c])
        outs = []
        for pair in range(npair):
            hp_ = hh[:, pair * LANES:(pair + 1) * LANES]
            s_all = jnp.sum(hp_, -1, keepdims=True)
            s_lo = jnp.sum(jnp.where(half0, hp_, 0.0), -1, keepdims=True)
            mu = jnp.where(half0, s_lo, s_all - s_lo) * (1.0 / ML_HEAD_DIM)
            dl = hp_ - mu
            d2 = dl * dl
            v_all = jnp.sum(d2, -1, keepdims=True)
            v_lo = jnp.sum(jnp.where(half0, d2, 0.0), -1, keepdims=True)
            var = jnp.where(half0, v_lo, v_all - v_lo) * (1.0 / ML_HEAD_DIM)
            outs.append(dl * lax.rsqrt(var + LN_EPS))
        o_ref[0, c] = (jnp.concatenate(outs, -1) * g).astype(o_ref.dtype)
        return 0

    lax.fori_loop(0, nchunk, fin, 0)


def _mlstm_call(mq, mk, mv, mo, gic, gfc, gir, gfr, g):
    b, ltot, w = mq.shape
    nc = ltot // CHUNK
    r4 = lambda a: a.reshape(b, nc, CHUNK, a.shape[-1])
    blk = lambda width: pl.BlockSpec((1, nc, CHUNK, width), lambda i: (i, 0, 0, 0))
    out = pl.pallas_call(
        functools.partial(_mlstm_kernel, nchunk=nc, nctx=CTX_LEN // CHUNK),
        out_shape=jax.ShapeDtypeStruct((b, nc, CHUNK, w), BF16),
        grid=(b,),
        in_specs=[blk(w), blk(w), blk(w), blk(w), blk(N_GATE), blk(N_GATE),
                  pl.BlockSpec((1, nc, N_GATE, CHUNK), lambda i: (i, 0, 0, 0)),
                  pl.BlockSpec((1, nc, N_GATE, CHUNK), lambda i: (i, 0, 0, 0)),
                  pl.BlockSpec((1, w), lambda i: (0, 0))],
        out_specs=blk(w),
        scratch_shapes=[pltpu.VMEM((nc, CHUNK, w), F32), pltpu.VMEM((nc, CHUNK, w), F32)],
        compiler_params=pltpu.CompilerParams(dimension_semantics=("parallel",),
                                             vmem_limit_bytes=VMEM_LIMIT_BYTES),
        name="mlstm",
    )(r4(mq), r4(mk), r4(mv), r4(mo), r4(gic), r4(gfc), gir, gfr, g)
    return out.reshape(b, ltot, w)


def _post_kernel(x_ref, mod_ref, att_ref, z_ref, zp_ref, zn_ref, ml_ref, cw_ref, cb_ref, cg_ref, cbb_ref,
                 wo_ref, g1_ref, b1_ref, o_ref, zbuf_ref, *, t0, nt_all):
    t = pl.program_id(1) + t0
    left_ok = t >= 2
    right_ok = jnp.logical_and(t >= 1, t <= nt_all - 2)
    zbuf_ref[0:HALO, :] = jnp.where(left_ok, zp_ref[0], 0.0)
    zbuf_ref[HALO:HALO + TM, :] = z_ref[0]
    zbuf_ref[HALO + TM:, :] = jnp.where(right_ok, zn_ref[0], 0.0)
    acc = jnp.zeros((TM, CONV_CH), F32)
    off = HALO - CONV_WIDTH // 2
    for j in range(CONV_WIDTH):
        acc = acc + cw_ref[j:j + 1, :] * zbuf_ref[off + j:off + j + TM, :]
    cv = _layer_norm(acc + cb_ref[...], cg_ref[...], cbb_ref[...])
    cv = cv * _sigmoid(cv)

    n_att = MLA_HEADS * V_HEAD
    y = jnp.dot(att_ref[0], wo_ref[0:n_att, :], preferred_element_type=F32)
    y = y + jnp.dot(cv.astype(BF16), wo_ref[n_att:n_att + CONV_CH, :], preferred_element_type=F32)
    y = y + jnp.dot(ml_ref[0], wo_ref[n_att + CONV_CH:, :], preferred_element_type=F32)
    g1 = mod_ref[0, 2:3, :]
    o_ref[0] = _layer_norm(DEEPNORM_ALPHA * x_ref[0] + g1 * y, g1_ref[...], b1_ref[...])


def _post_call(xx, mod, att, z, ml, cw, cb, cg, cbb, wo, g1, b1, t0):
    b, ltot, d = xx.shape
    nt_all = ltot // TM
    nt = nt_all - t0
    ctx_row = mod.shape[0] - SUBLANES
    hpt = TM // HALO
    nhalo = ltot // HALO

    def tok(width, off):
        return pl.BlockSpec((1, TM, width), lambda i, t: (i, t + off, 0))

    def const2(shape):
        return pl.BlockSpec(shape, lambda i, t: (0, 0))

    return pl.pallas_call(
        functools.partial(_post_kernel, t0=t0, nt_all=nt_all),
        out_shape=jax.ShapeDtypeStruct((b, nt * TM, d), F32),
        grid=(b, nt),
        in_specs=[
            tok(d, t0),
            pl.BlockSpec((1, 6, d), lambda i, t: (jnp.where(t + t0 == 0, ctx_row, i), 0, 0)),
            tok(att.shape[-1], 0),
            tok(CONV_CH, t0),
            pl.BlockSpec((1, HALO, CONV_CH), lambda i, t: (i, jnp.maximum((t + t0) * hpt - 1, 0), 0)),
            pl.BlockSpec((1, HALO, CONV_CH), lambda i, t: (i, jnp.minimum((t + t0 + 1) * hpt, nhalo - 1), 0)),
            tok(ML_WIDTH, t0),
            const2(cw.shape), const2(cb.shape), const2(cg.shape), const2(cbb.shape),
            const2(wo.shape), const2(g1.shape), const2(b1.shape),
        ],
        out_specs=tok(d, 0),
        scratch_shapes=[pltpu.VMEM((TM + 2 * HALO, CONV_CH), F32)],
        compiler_params=pltpu.CompilerParams(dimension_semantics=("parallel", "parallel"),
                                             vmem_limit_bytes=VMEM_LIMIT_BYTES),
        name="out_proj",
    )(xx, mod, att, z, z, z, ml, cw, cb, cg, cbb, wo, g1, b1)


FF_CHUNK = 1024


def _mlp_kernel(x_ref, mod_ref, w1_ref, b1_ref, w2_ref, b2_ref, g_ref, b_ref, o_ref):
    x = x_ref[0]
    sh2 = mod_ref[0, 3:4, :]
    sc2 = mod_ref[0, 4:5, :]
    g2 = mod_ref[0, 5:6, :]
    u = (x * (1.0 + sc2) + sh2).astype(BF16)
    acc = jnp.zeros(x.shape, F32)
    for c in range(D_FF // FF_CHUNK):
        sl = slice(c * FF_CHUNK, (c + 1) * FF_CHUNK)
        h = jnp.maximum(jnp.dot(u, w1_ref[:, sl], preferred_element_type=F32) + b1_ref[:, sl], 0.0)
        acc = acc + jnp.dot((h * h).astype(BF16), w2_ref[sl, :], preferred_element_type=F32)
    o_ref[0] = _layer_norm(DEEPNORM_ALPHA * x + g2 * (acc + b2_ref[...]), g_ref[...], b_ref[...])


def _mlp_call(x1, mod, w1, b1, w2, b2, g, bb, t0_mod):
    b, ln, d = x1.shape
    nt = ln // TM
    ctx_row = mod.shape[0] - SUBLANES

    def const2(shape):
        return pl.BlockSpec(shape, lambda i, t: (0, 0), pipeline_mode=pl.Buffered(1))

    return pl.pallas_call(
        _mlp_kernel,
        out_shape=jax.ShapeDtypeStruct((b, ln, d), F32),
        grid=(b, nt),
        in_specs=[
            pl.BlockSpec((1, TM, d), lambda i, t: (i, t, 0)),
            pl.BlockSpec((1, 6, d), lambda i, t: (jnp.where(t + t0_mod == 0, ctx_row, i), 0, 0)),
            const2(w1.shape), const2(b1.shape), const2(w2.shape), const2(b2.shape),
            const2(g.shape), const2(bb.shape),
        ],
        out_specs=pl.BlockSpec((1, TM, d), lambda i, t: (i, t, 0)),
        compiler_params=pltpu.CompilerParams(dimension_semantics=("parallel", "parallel"),
                                             vmem_limit_bytes=VMEM_LIMIT_BYTES),
        name="mlp",
    )(x1, mod, w1, b1, w2, b2, g, bb)


def _rope_tables(seq):
    half = QK_ROPE // 2
    nf = half // 2
    pos = np.arange(seq)
    inv_freq = ROPE_THETA ** (-np.arange(nf, dtype=np.float32) / nf)
    cos = np.ones((CTX_LEN + seq, LANES), np.float32)
    slo = np.zeros((CTX_LEN + seq, LANES), np.float32)
    shi = np.zeros((CTX_LEN + seq, LANES), np.float32)
    for part, p in enumerate((pos // GRID_W, pos % GRID_W)):
        ang = p.astype(np.float32)[:, None] * inv_freq[None, :]
        c, s = np.cos(ang), np.sin(ang)
        base = QK_NOPE + part * half
        cos[CTX_LEN:, base:base + nf] = c
        cos[CTX_LEN:, base + nf:base + half] = c
        slo[CTX_LEN:, base:base + nf] = -s
        shi[CTX_LEN:, base + nf:base + half] = s
    return jnp.asarray(cos), jnp.asarray(slo), jnp.asarray(shi)


def _pad_in_weights(w_in):
    d = w_in.shape[0]
    offs = np.cumsum([0, Q_LORA, KV_LORA, QK_ROPE, 2 * CONV_CH, ML_WIDTH, ML_WIDTH, ML_WIDTH, ML_WIDTH])
    o_cq, o_ckv, o_kr, o_conv, o_mq, o_mk, o_mv, o_mo, o_mg = [int(o) for o in offs]
    z = lambda n: jnp.zeros((d, n), w_in.dtype)
    mg = w_in[:, o_mg:o_mg + 4 * ML_HEADS]
    h = ML_HEADS
    gi = jnp.concatenate([mg[:, 0:h], mg[:, 2 * h:3 * h]], 1)
    gf = jnp.concatenate([mg[:, h:2 * h], mg[:, 3 * h:4 * h]], 1)
    cols = [
        w_in[:, o_cq:o_ckv], w_in[:, o_ckv:o_kr],
        z(QK_NOPE), w_in[:, o_kr:o_conv], z(LANES - QK_NOPE - QK_ROPE),
        w_in[:, o_conv:o_mg],
        gi, z(LANES - N_GATE), gf, z(LANES - N_GATE),
    ]
    return jnp.concatenate(cols, 1).astype(BF16)


def _pad_gate_bias(b_gates):
    h = ML_HEADS
    pad = jnp.zeros((LANES - N_GATE,), b_gates.dtype)
    bgi = jnp.concatenate([b_gates[0:h], b_gates[2 * h:3 * h], pad])[None, :]
    bgf = jnp.concatenate([b_gates[h:2 * h], b_gates[3 * h:4 * h], pad])[None, :]
    return bgi, bgf


def _pad_mla_weights(w_uq, w_ukv):
    dq = QK_NOPE + QK_ROPE
    wq = w_uq.reshape(Q_LORA, MLA_HEADS, dq)
    wq = jnp.pad(wq, ((0, 0), (0, 0), (0, HEAD_PAD - dq))).reshape(Q_LORA, MLA_HEADS * HEAD_PAD)
    wkv = w_ukv.reshape(KV_LORA, MLA_HEADS, QK_NOPE + V_HEAD)
    wk = jnp.pad(wkv[:, :, :QK_NOPE], ((0, 0), (0, 0), (0, HEAD_PAD - QK_NOPE)))
    wk = wk.reshape(KV_LORA, MLA_HEADS * HEAD_PAD)
    wv = wkv[:, :, QK_NOPE:].reshape(KV_LORA, MLA_HEADS * V_HEAD)
    return wq.astype(BF16), wk.astype(BF16), wv.astype(BF16)


def kernel(x, c, ctx, c_ctx, w_ada, b_ada, w_in, g_qn, w_uq, g_kvn, w_ukv, conv_w, conv_b, conv_ln_g, conv_ln_b, b_gates, ml_norm_g, w_out, ln1_g, ln1_b, w_mlp1, b_mlp1, w_mlp2, b_mlp2, ln2_g, ln2_b):
    b, seq, d = x.shape
    depth = w_in.shape[0]
    assert ctx.shape[1] == CTX_LEN == TM and seq % TM == 0 and d == D_MODEL and depth == DEPTH
    row = lambda a: a[None, :]

    cc = jnp.concatenate([c, c_ctx[None, :], jnp.zeros((SUBLANES - 1, d), c.dtype)], 0)
    mod_all = _ada_call(cc, w_ada, b_ada).reshape(depth, cc.shape[0], 6, d)
    cos, slo, shi = _rope_tables(seq)

    xx = jnp.concatenate([ctx, x], 1)
    for l in range(depth):
        last = l == depth - 1
        t0 = 1 if last else 0
        mod = mod_all[l]
        wq, wk, wv = _pad_mla_weights(w_uq[l], w_ukv[l])
        bgi, bgf = _pad_gate_bias(b_gates[l])
        q, k, v, z, mq, mk, mv, mo, gic, gfc, gir, gfr = _in_call(
            xx, mod, _pad_in_weights(w_in[l]), row(g_qn[l]), row(g_kvn[l]), wq, wk, wv, bgi, bgf, cos, slo, shi)
        att = _attn_call(q, k, v, t0)
        ml = _mlstm_call(mq, mk, mv, mo, gic, gfc, gir, gfr, row(ml_norm_g[l]))
        x1 = _post_call(xx, mod, att, z, ml, conv_w[l], row(conv_b[l]), row(conv_ln_g[l]), row(conv_ln_b[l]),
                        w_out[l].astype(BF16), row(ln1_g[l]), row(ln1_b[l]), t0)
        xx = _mlp_call(x1, mod, w_mlp1[l].astype(BF16), row(b_mlp1[l]), w_mlp2[l].astype(BF16),
                       row(b_mlp2[l]), row(ln2_g[l]), row(ln2_b[l]), t0)
    return xx
```

```python
import functools

import numpy as np
import jax
import jax.numpy as jnp
from jax import lax
from jax.experimental import pallas as pl
from jax.experimental.pallas import tpu as pltpu

F32 = jnp.float32
BF16 = jnp.bfloat16

LANES = 128
SUBLANES = 8
VMEM_LIMIT_BYTES = 56 * 1024 * 1024

D_MODEL = 1024
GRID_W = 64
CTX_LEN = 256
MLA_HEADS = 8
QK_NOPE = 64
QK_ROPE = 32
V_HEAD = 64
Q_LORA = 256
KV_LORA = 128
CONV_CH = 256
CONV_WIDTH = 31
ML_HEADS = 4
ML_HEAD_DIM = 64
ML_WIDTH = ML_HEADS * ML_HEAD_DIM
CHUNK = 128
D_FF = 4 * D_MODEL
ROPE_THETA = 10000.0
LN_EPS = 1e-5
RMS_EPS = 1e-6
DEPTH = 2
DEEPNORM_ALPHA = (2 * DEPTH) ** 0.25

TM = 256
HALO = 16
HEAD_PAD = LANES

C_CQ = 0
C_CKV = C_CQ + Q_LORA
C_KR = C_CKV + KV_LORA
C_CONV = C_KR + LANES
C_MQ = C_CONV + 2 * CONV_CH
C_MK = C_MQ + ML_WIDTH
C_MV = C_MK + ML_WIDTH
C_MO = C_MV + ML_WIDTH
C_GI = C_MO + ML_WIDTH
C_GF = C_GI + LANES
IN_PAD = C_GF + LANES
N_GATE = 2 * ML_HEADS


def _layer_norm(v, g, b):
    mu = jnp.mean(v, -1, keepdims=True)
    d = v - mu
    var = jnp.mean(d * d, -1, keepdims=True)
    return d * lax.rsqrt(var + LN_EPS) * g + b


def _rms_norm(v, g):
    return v * lax.rsqrt(jnp.mean(v * v, -1, keepdims=True) + RMS_EPS) * g


def _sigmoid(v):
    return 1.0 / (1.0 + jnp.exp(-v))


def _log_sigmoid(v):
    return jnp.minimum(v, 0.0) - jnp.log1p(jnp.exp(-jnp.abs(v)))


def _rope(t, cos, sin_lo, sin_hi):
    return t * cos + pltpu.roll(t, LANES - 8, 1) * sin_lo + pltpu.roll(t, 8, 1) * sin_hi


def _ada_kernel(c_ref, w_ref, b_ref, o_ref):
    c = c_ref[...]
    s = (c * _sigmoid(c)).astype(BF16)
    o_ref[0] = jnp.dot(s, w_ref[0].astype(BF16), preferred_element_type=F32) + b_ref[0]


def _ada_call(cc, w_ada, b_ada):
    depth, d, n6 = w_ada.shape
    rows = cc.shape[0]
    nblk = n6 // d
    return pl.pallas_call(
        _ada_kernel,
        out_shape=jax.ShapeDtypeStruct((depth, rows, n6), F32),
        grid=(depth, nblk),
        in_specs=[
            pl.BlockSpec((rows, d), lambda l, n: (0, 0)),
            pl.BlockSpec((1, d, d), lambda l, n: (l, 0, n)),
            pl.BlockSpec((1, 1, d), lambda l, n: (l, 0, n)),
        ],
        out_specs=pl.BlockSpec((1, rows, d), lambda l, n: (l, 0, n)),
        compiler_params=pltpu.CompilerParams(dimension_semantics=("parallel", "parallel")),
        name="ada_mod",
    )(cc, w_ada, b_ada.reshape(depth, 1, n6))


def _in_kernel(x_ref, mod_ref, w_ref, gq_ref, gkv_ref, wuq_ref, wuk_ref, wuv_ref, bgi_ref, bgf_ref,
               cos_ref, slo_ref, shi_ref,
               q_ref, k_ref, v_ref, z_ref, mq_ref, mkt_ref, mv_ref, mo_ref, gir_ref, gfr_ref):
    x = x_ref[0]
    sh1 = mod_ref[0, 0:1, :]
    sc1 = mod_ref[0, 1:2, :]
    xm = (x * (1.0 + sc1) + sh1).astype(BF16)

    def proj(lo, hi):
        return jnp.dot(xm, w_ref[:, lo:hi], preferred_element_type=F32)

    cos = cos_ref[...]
    slo = slo_ref[...]
    shi = shi_ref[...]

    cqn = _rms_norm(proj(C_CQ, C_CQ + Q_LORA), gq_ref[...]).astype(BF16)
    qf = jnp.dot(cqn, wuq_ref[...], preferred_element_type=F32) * ((QK_NOPE + QK_ROPE) ** -0.5)
    for h in range(MLA_HEADS):
        sl = slice(h * HEAD_PAD, (h + 1) * HEAD_PAD)
        q_ref[0, :, sl] = _rope(qf[:, sl], cos, slo, shi).astype(BF16)

    ckvn = _rms_norm(proj(C_CKV, C_CKV + KV_LORA), gkv_ref[...]).astype(BF16)
    kr = _rope(proj(C_KR, C_KR + LANES), cos, slo, shi)
    kf = jnp.dot(ckvn, wuk_ref[...], preferred_element_type=F32)
    for h in range(MLA_HEADS):
        sl = slice(h * HEAD_PAD, (h + 1) * HEAD_PAD)
        k_ref[0, :, sl] = (kf[:, sl] + kr).astype(BF16)
    v_ref[0] = jnp.dot(ckvn, wuv_ref[...], preferred_element_type=F32).astype(BF16)

    a = proj(C_CONV, C_CONV + CONV_CH)
    gt = proj(C_CONV + CONV_CH, C_CONV + 2 * CONV_CH)
    z_ref[0] = a * _sigmoid(gt)

    mq_ref[0] = proj(C_MQ, C_MQ + ML_WIDTH).astype(BF16)
    mk_t = (proj(C_MK, C_MK + ML_WIDTH) * (ML_HEAD_DIM ** -0.5)).T
    mv_ref[0] = proj(C_MV, C_MV + ML_WIDTH).astype(BF16)
    mo_ref[0] = proj(C_MO, C_MO + ML_WIDTH)

    gi_t = (proj(C_GI, C_GI + LANES) + bgi_ref[...]).T
    gf_t = _log_sigmoid(proj(C_GF, C_GF + LANES) + bgf_ref[...]).T
    for cc in range(TM // CHUNK):
        cs = slice(cc * CHUNK, (cc + 1) * CHUNK)
        mkt_ref[0, cc] = mk_t[:, cs].astype(BF16)
        gir_ref[0, cc] = gi_t[:N_GATE, cs]
        gfr_ref[0, cc] = gf_t[:N_GATE, cs]


def _in_call(xx, mod, w_in_p, gq, gkv, wuq, wuk, wuv, bgi, bgf, cos, slo, shi):
    b, ltot, d = xx.shape
    nt = ltot // TM
    nc = ltot // CHUNK
    ctx_row = mod.shape[0] - SUBLANES

    def tok(width):
        return pl.BlockSpec((1, TM, width), lambda i, t: (i, t, 0))

    def const2(shape):
        return pl.BlockSpec(shape, lambda i, t: (0, 0))

    tab = pl.BlockSpec((TM, LANES), lambda i, t: (t, 0))
    grow = pl.BlockSpec((1, TM // CHUNK, N_GATE, CHUNK), lambda i, t: (i, t, 0, 0))
    outs = [
        (jax.ShapeDtypeStruct((b, ltot, MLA_HEADS * HEAD_PAD), BF16), tok(MLA_HEADS * HEAD_PAD)),
        (jax.ShapeDtypeStruct((b, ltot, MLA_HEADS * HEAD_PAD), BF16), tok(MLA_HEADS * HEAD_PAD)),
        (jax.ShapeDtypeStruct((b, ltot, MLA_HEADS * V_HEAD), BF16), tok(MLA_HEADS * V_HEAD)),
        (jax.ShapeDtypeStruct((b, ltot, CONV_CH), F32), tok(CONV_CH)),
        (jax.ShapeDtypeStruct((b, ltot, ML_WIDTH), BF16), tok(ML_WIDTH)),
        (jax.ShapeDtypeStruct((b, nc, ML_WIDTH, CHUNK), BF16),
         pl.BlockSpec((1, TM // CHUNK, ML_WIDTH, CHUNK), lambda i, t: (i, t, 0, 0))),
        (jax.ShapeDtypeStruct((b, ltot, ML_WIDTH), BF16), tok(ML_WIDTH)),
        (jax.ShapeDtypeStruct((b, ltot, ML_WIDTH), F32), tok(ML_WIDTH)),
        (jax.ShapeDtypeStruct((b, nc, N_GATE, CHUNK), F32), grow),
        (jax.ShapeDtypeStruct((b, nc, N_GATE, CHUNK), F32), grow),
    ]
    return pl.pallas_call(
        _in_kernel,
        out_shape=[o[0] for o in outs],
        grid=(b, nt),
        in_specs=[
            tok(d),
            pl.BlockSpec((1, 6, d), lambda i, t: (jnp.where(t == 0, ctx_row, i), 0, 0)),
            const2(w_in_p.shape),
            const2(gq.shape), const2(gkv.shape),
            const2(wuq.shape), const2(wuk.shape), const2(wuv.shape),
            const2(bgi.shape), const2(bgf.shape),
            tab, tab, tab,
        ],
        out_specs=[o[1] for o in outs],
        compiler_params=pltpu.CompilerParams(dimension_semantics=("parallel", "parallel"),
                                             vmem_limit_bytes=VMEM_LIMIT_BYTES),
        name="in_proj",
    )(xx, mod, w_in_p, gq, gkv, wuq, wuk, wuv, bgi, bgf, cos, slo, shi)


def _attn_kernel(q_ref, k_ref, v_ref, o_ref, *, t0, ltot):
    def run(nk):
        outs = []
        for hh in range(2):
            sl = slice(hh * HEAD_PAD, (hh + 1) * HEAD_PAD)
            s = lax.dot_general(q_ref[0, :, sl], k_ref[0, :nk, sl], (((1,), (1,)), ((), ())),
                                preferred_element_type=F32)
            m = jnp.max(s, -1, keepdims=True)
            p = jnp.exp(s - m)
            l = jnp.sum(p, -1, keepdims=True)
            o = jnp.dot(p.astype(BF16), v_ref[0, :nk, :], preferred_element_type=F32)
            outs.append(o / l)
        lane = lax.broadcasted_iota(jnp.int32, outs[0].shape, 1)
        o_ref[0] = jnp.where(lane < V_HEAD, outs[0], outs[1]).astype(o_ref.dtype)

    if t0 == 0:
        t = pl.program_id(2)

        @pl.when(t == 0)
        def _():
            run(CTX_LEN)

        @pl.when(t > 0)
        def _():
            run(ltot)
    else:
        run(ltot)


def _attn_call(q, k, v, t0):
    b, ltot, _ = q.shape
    nt = ltot // TM - t0
    npair = MLA_HEADS // 2
    return pl.pallas_call(
        functools.partial(_attn_kernel, t0=t0, ltot=ltot),
        out_shape=jax.ShapeDtypeStruct((b, nt * TM, MLA_HEADS * V_HEAD), BF16),
        grid=(b, npair, nt),
        in_specs=[
            pl.BlockSpec((1, TM, 2 * HEAD_PAD), lambda i, j, t: (i, t + t0, j)),
            pl.BlockSpec((1, ltot, 2 * HEAD_PAD), lambda i, j, t: (i, 0, j)),
            pl.BlockSpec((1, ltot, 2 * V_HEAD), lambda i, j, t: (i, 0, j)),
        ],
        out_specs=pl.BlockSpec((1, TM, 2 * V_HEAD), lambda i, j, t: (i, t, j)),
        compiler_params=pltpu.CompilerParams(dimension_semantics=("parallel", "parallel", "parallel"),
                                             vmem_limit_bytes=VMEM_LIMIT_BYTES),
        name="mla_attn",
    )(q, k, v)


def _mlstm_kernel(q_ref, kt_ref, v_ref, mo_ref, gi_ref, gf_ref, g_ref,
                  o_ref, hf_ref, hb_ref, sn_ref, a_ref, rp_ref, e_ref, tot_ref, gmax_ref, *, nchunk, nctx):
    t_ = CHUNK
    npair = ML_HEADS // 2
    row = lax.broadcasted_iota(jnp.int32, (t_, t_), 0)
    col = lax.broadcasted_iota(jnp.int32, (t_, t_), 1)
    causal_masks = (col <= row, col >= row)
    blockdiag = (row < ML_HEAD_DIM) == (col < ML_HEAD_DIM)
    half0 = lax.broadcasted_iota(jnp.int32, (1, LANES), 1) < ML_HEAD_DIM
    rowhalf0 = lax.broadcasted_iota(jnp.int32, (LANES, 1), 0) < ML_HEAD_DIM
    lane8 = lax.broadcasted_iota(jnp.int32, (1, t_), 1)
    ones8 = jnp.ones((N_GATE, t_), F32)
    n_parts = 3
    ones_row0 = 2 * n_parts * N_GATE
    zpad = jnp.zeros((t_ - ones_row0 - N_GATE, t_), F32)
    ones_b = jnp.ones((t_, LANES), BF16)
    zeros_b = jnp.zeros((t_, LANES), BF16)
    half_ones = tuple(jnp.broadcast_to(jnp.where(half0, on, 1.0 - on), (t_, LANES)).astype(BF16) for on in (1.0, 0.0))
    sub16 = lax.broadcasted_iota(jnp.int32, (2 * SUBLANES, 2 * LANES), 0)

    r2 = lax.broadcasted_iota(jnp.int32, (t_, 2 * LANES), 0)
    l2 = lax.broadcasted_iota(jnp.int32, (t_, 2 * LANES), 1)
    blk, gate = r2 >> 3, r2 & 7
    lblk, lhalf = l2 >> 7, (l2 >> 6) & 1
    consts = {}
    for d in range(2):
        for p in range(npair):
            c0 = d * ML_HEADS + 2 * p
            neg_r = jnp.where(jnp.logical_and(blk < n_parts, gate == c0 + lblk), -1.0, 0.0)
            group = jnp.where(blk < n_parts, 0, jnp.where(blk < 2 * n_parts, 1, 2))
            bcast = jnp.where(jnp.logical_and(group == lblk, gate == c0 + lhalf), 1.0, 0.0)
            consts[d, p] = (neg_r.astype(BF16), bcast.astype(BF16))

    def split3(x):
        hi = x.astype(BF16).astype(F32)
        rem = x - hi
        mid = rem.astype(BF16).astype(F32)
        return [hi, mid, rem - mid]

    def lane_scan(x, op, fill, reverse):
        k = 1
        while k < t_:
            if reverse:
                shifted, valid = pltpu.roll(x, t_ - k, 1), lane8 < t_ - k
            else:
                shifted, valid = pltpu.roll(x, k, 1), lane8 >= k
            x = op(x, jnp.where(valid, shifted, fill))
            k *= 2
        return x

    nrow = nchunk * N_GATE
    is_fwd_row = (lax.broadcasted_iota(jnp.int32, (nrow, t_), 0) & (N_GATE - 1)) < ML_HEADS
    lf = gf_ref[0].reshape(nrow, t_)
    lf_parts = jnp.concatenate([part.astype(BF16) for part in split3(lf)], axis=1)
    tri_up = jnp.where(row <= col, 1.0, 0.0).astype(BF16)
    tri_dn = jnp.where(row >= col, 1.0, 0.0).astype(BF16)
    b = jnp.where(is_fwd_row,
                  jnp.dot(lf_parts, jnp.concatenate([tri_up] * n_parts, axis=0), preferred_element_type=F32),
                  jnp.dot(lf_parts, jnp.concatenate([tri_dn] * n_parts, axis=0), preferred_element_type=F32))
    tot = jnp.sum(lf, axis=1, keepdims=True)
    r = gi_ref[0].reshape(nrow, t_) - b
    rmax = jnp.where(is_fwd_row, lane_scan(r, jnp.maximum, -jnp.inf, False),
                     lane_scan(r, jnp.maximum, -jnp.inf, True))
    rlast = jnp.max(r, axis=1, keepdims=True)
    col_parts = split3(rmax) + split3(b)
    for c in range(nchunk):
        rows_c = slice(c * N_GATE, (c + 1) * N_GATE)
        packed = jnp.concatenate([part[rows_c] for part in col_parts] + [ones8, zpad], axis=0)
        a_ref[c] = packed.T.astype(BF16)
    for j, part in enumerate(split3(r)):
        rp_ref[j] = part
    e_ref[...] = jnp.exp(r - rlast)
    tot_ref[...] = jnp.broadcast_to(tot, (nrow, t_))
    gmax_ref[...] = jnp.broadcast_to(tot + rlast, (nrow, t_))

    def prep(d, c, m_prev):
        rows_c = pl.ds(pl.multiple_of(c * N_GATE, N_GATE), N_GATE)
        tot_c = tot_ref[rows_c, :]
        gmax_c = gmax_ref[rows_c, :]
        m_new = jnp.maximum(tot_c + m_prev, gmax_c)
        decay = jnp.exp(tot_c + m_prev - m_new)
        gamma = jnp.exp(gmax_c - m_new)
        return a_ref[c], [rp_ref[j, rows_c, :] for j in range(n_parts)], e_ref[rows_c, :], decay, gamma, m_new

    def pair_step(d, p, c, pre, m_prev, h_ref):
        a, r_parts, e_row, decay, gamma, _ = pre
        c0 = d * ML_HEADS + 2 * p
        c1 = c0 + 1
        sl = slice(p * LANES, (p + 1) * LANES)
        q = q_ref[0, c, :, sl]
        kt = kt_ref[0, c, sl, :]
        v = v_ref[0, c, :, sl]
        neg_r, bcast = consts[d, p]

        def rows2(x):
            return jnp.concatenate([jnp.broadcast_to(x[c0:c0 + 1, :], (2 * SUBLANES, t_)),
                                    jnp.broadcast_to(x[c1:c1 + 1, :], (2 * SUBLANES, t_))], axis=1)

        dyn = jnp.where(sub16 == 0, rows2(r_parts[0]),
                        jnp.where(sub16 == 1, rows2(r_parts[1]),
                                  jnp.where(sub16 == 2, rows2(r_parts[2]), 0.0))).astype(BF16)
        rhs_e = jnp.concatenate([neg_r[:ones_row0], dyn, neg_r[ones_row0 + 2 * SUBLANES:]], axis=0)
        z = jnp.dot(a, jnp.concatenate([rhs_e, bcast], axis=1), preferred_element_type=F32)
        r_bc = z[:, 2 * LANES:3 * LANES]
        b_bc = z[:, 3 * LANES:]

        kt_heads = jnp.concatenate([jnp.where(rowhalf0, kt, zeros_b), jnp.where(rowhalf0, zeros_b, kt)], axis=1)
        qk = jnp.dot(q, kt_heads, preferred_element_type=F32)
        nd = jnp.zeros((t_, 2 * LANES), F32)
        for hh in range(2):
            hs = slice(hh * LANES, (hh + 1) * LANES)
            wts = jnp.exp(jnp.where(causal_masks[d], z[:, hs], -jnp.inf))
            v_h = jnp.where(half0, v, zeros_b) if hh == 0 else jnp.where(half0, zeros_b, v)
            nd = nd + jnp.dot((qk[:, hs] * wts).astype(BF16), jnp.concatenate([v_h, half_ones[hh]], axis=1),
                              preferred_element_type=F32)

        sn = sn_ref[2 * p + d]
        qs = jnp.dot(q, sn.astype(BF16), preferred_element_type=F32)
        m_bc = jnp.where(half0, m_prev[c0:c0 + 1, :], m_prev[c1:c1 + 1, :])
        p_bc = jnp.maximum(m_bc, r_bc)
        alpha = jnp.exp(r_bc - p_bc)
        beta = jnp.exp(m_bc - p_bc)
        num = alpha * nd[:, :LANES] + beta * qs[:, :LANES]
        den = alpha * nd[:, LANES:] + beta * qs[:, LANES:]
        h_ref[c, :, sl] = num / jnp.maximum(jnp.abs(den), jnp.exp(-(b_bc + p_bc)))

        wkt = (kt.astype(F32) * jnp.where(rowhalf0, e_row[c0:c0 + 1, :], e_row[c1:c1 + 1, :])).astype(BF16)
        upd = jnp.dot(wkt, jnp.concatenate([v, ones_b], axis=1), preferred_element_type=F32)
        dec_rows = jnp.where(rowhalf0, decay[c0:c0 + 1, :], decay[c1:c1 + 1, :])
        gam_rows = jnp.where(blockdiag, jnp.where(rowhalf0, gamma[c0:c0 + 1, :], gamma[c1:c1 + 1, :]), 0.0)
        sn_ref[2 * p + d] = jnp.concatenate([dec_rows * sn[:, :LANES] + gam_rows * upd[:, :LANES],
                                             dec_rows * sn[:, LANES:] + gam_rows * upd[:, LANES:]], axis=1)

    sn_ref[...] = jnp.zeros(sn_ref.shape, F32)

    def body(i, carry):
        cb = jnp.where(i < nctx, nctx - 1 - i, nchunk - 1 + nctx - i)
        new = []
        for d, c, h_ref in ((0, i, hf_ref), (1, cb, hb_ref)):
            pre = prep(d, c, carry[d])
            for p in range(npair):
                pair_step(d, p, c, pre, carry[d], h_ref)
            new.append(pre[-1])
        return tuple(new)

    m_init = jnp.zeros((N_GATE, t_), F32)
    lax.fori_loop(0, nchunk, body, (m_init, m_init))

    g = g_ref[...]

    def fin(c, _):
        hh = (hf_ref[c] + hb_ref[c]) * _sigmoid(mo_ref[0, c])
        outs = []
        for pair in range(npair):
            hp_ = hh[:, pair * LANES:(pair + 1) * LANES]
            s_all = jnp.sum(hp_, -1, keepdims=True)
            s_lo = jnp.sum(jnp.where(half0, hp_, 0.0), -1, keepdims=True)
            mu = jnp.where(half0, s_lo, s_all - s_lo) * (1.0 / ML_HEAD_DIM)
            dl = hp_ - mu
            d2 = dl * dl
            v_all = jnp.sum(d2, -1, keepdims=True)
            v_lo = jnp.sum(jnp.where(half0, d2, 0.0), -1, keepdims=True)
            var = jnp.where(half0, v_lo, v_all - v_lo) * (1.0 / ML_HEAD_DIM)
            outs.append(dl * lax.rsqrt(var + LN_EPS))
        o_ref[0, c] = (jnp.concatenate(outs, -1) * g).astype(o_ref.dtype)
        return 0

    lax.fori_loop(0, nchunk, fin, 0)


def _mlstm_call(mq, mkt, mv, mo, gir, gfr, g):
    b, ltot, w = mq.shape
    nc = ltot // CHUNK
    r4 = lambda a: a.reshape(b, nc, CHUNK, a.shape[-1])
    blk = lambda rows, width: pl.BlockSpec((1, nc, rows, width), lambda i: (i, 0, 0, 0))
    out = pl.pallas_call(
        functools.partial(_mlstm_kernel, nchunk=nc, nctx=CTX_LEN // CHUNK),
        out_shape=jax.ShapeDtypeStruct((b, nc, CHUNK, w), BF16),
        grid=(b,),
        in_specs=[blk(CHUNK, w), blk(w, CHUNK), blk(CHUNK, w), blk(CHUNK, w),
                  blk(N_GATE, CHUNK), blk(N_GATE, CHUNK),
                  pl.BlockSpec((1, w), lambda i: (0, 0))],
        out_specs=blk(CHUNK, w),
        scratch_shapes=[pltpu.VMEM((nc, CHUNK, w), F32), pltpu.VMEM((nc, CHUNK, w), F32),
                        pltpu.VMEM((2 * (ML_HEADS // 2), LANES, 2 * LANES), F32),
                        pltpu.VMEM((nc, CHUNK, LANES), BF16), pltpu.VMEM((3, nc * N_GATE, CHUNK), F32),
                        pltpu.VMEM((nc * N_GATE, CHUNK), F32), pltpu.VMEM((nc * N_GATE, CHUNK), F32),
                        pltpu.VMEM((nc * N_GATE, CHUNK), F32)],
        compiler_params=pltpu.CompilerParams(dimension_semantics=("parallel",),
                                             vmem_limit_bytes=VMEM_LIMIT_BYTES),
        name="mlstm",
    )(r4(mq), mkt, r4(mv), r4(mo), gir, gfr, g)
    return out.reshape(b, ltot, w)


def _post_kernel(x_ref, mod_ref, att_ref, z_ref, zp_ref, zn_ref, ml_ref, cw_ref, cb_ref, cg_ref, cbb_ref,
                 wo_ref, g1_ref, b1_ref, o_ref, zbuf_ref, *, t0, nt_all):
    t = pl.program_id(1) + t0
    left_ok = t >= 2
    right_ok = jnp.logical_and(t >= 1, t <= nt_all - 2)
    zbuf_ref[0:HALO, :] = jnp.where(left_ok, zp_ref[0], 0.0)
    zbuf_ref[HALO:HALO + TM, :] = z_ref[0]
    zbuf_ref[HALO + TM:, :] = jnp.where(right_ok, zn_ref[0], 0.0)
    acc = jnp.zeros((TM, CONV_CH), F32)
    off = HALO - CONV_WIDTH // 2
    for j in range(CONV_WIDTH):
        acc = acc + cw_ref[j:j + 1, :] * zbuf_ref[off + j:off + j + TM, :]
    cv = _layer_norm(acc + cb_ref[...], cg_ref[...], cbb_ref[...])
    cv = cv * _sigmoid(cv)

    n_att = MLA_HEADS * V_HEAD
    y = jnp.dot(att_ref[0], wo_ref[0:n_att, :], preferred_element_type=F32)
    y = y + jnp.dot(cv.astype(BF16), wo_ref[n_att:n_att + CONV_CH, :], preferred_element_type=F32)
    y = y + jnp.dot(ml_ref[0], wo_ref[n_att + CONV_CH:, :], preferred_element_type=F32)
    g1 = mod_ref[0, 2:3, :]
    o_ref[0] = _layer_norm(DEEPNORM_ALPHA * x_ref[0] + g1 * y, g1_ref[...], b1_ref[...])


def _post_call(xx, mod, att, z, ml, cw, cb, cg, cbb, wo, g1, b1, t0):
    b, ltot, d = xx.shape
    nt_all = ltot // TM
    nt = nt_all - t0
    ctx_row = mod.shape[0] - SUBLANES
    hpt = TM // HALO
    nhalo = ltot // HALO

    def tok(width, off):
        return pl.BlockSpec((1, TM, width), lambda i, t: (i, t + off, 0))

    def const2(shape):
        return pl.BlockSpec(shape, lambda i, t: (0, 0))

    return pl.pallas_call(
        functools.partial(_post_kernel, t0=t0, nt_all=nt_all),
        out_shape=jax.ShapeDtypeStruct((b, nt * TM, d), F32),
        grid=(b, nt),
        in_specs=[
            tok(d, t0),
            pl.BlockSpec((1, 6, d), lambda i, t: (jnp.where(t + t0 == 0, ctx_row, i), 0, 0)),
            tok(att.shape[-1], 0),
            tok(CONV_CH, t0),
            pl.BlockSpec((1, HALO, CONV_CH), lambda i, t: (i, jnp.maximum((t + t0) * hpt - 1, 0), 0)),
            pl.BlockSpec((1, HALO, CONV_CH), lambda i, t: (i, jnp.minimum((t + t0 + 1) * hpt, nhalo - 1), 0)),
            tok(ML_WIDTH, t0),
            const2(cw.shape), const2(cb.shape), const2(cg.shape), const2(cbb.shape),
            const2(wo.shape), const2(g1.shape), const2(b1.shape),
        ],
        out_specs=tok(d, 0),
        scratch_shapes=[pltpu.VMEM((TM + 2 * HALO, CONV_CH), F32)],
        compiler_params=pltpu.CompilerParams(dimension_semantics=("parallel", "parallel"),
                                             vmem_limit_bytes=VMEM_LIMIT_BYTES),
        name="out_proj",
    )(xx, mod, att, z, z, z, ml, cw, cb, cg, cbb, wo, g1, b1)


FF_CHUNK = 1024


def _mlp_kernel(x_ref, mod_ref, w1_ref, b1_ref, w2_ref, b2_ref, g_ref, b_ref, o_ref):
    x = x_ref[0]
    sh2 = mod_ref[0, 3:4, :]
    sc2 = mod_ref[0, 4:5, :]
    g2 = mod_ref[0, 5:6, :]
    u = (x * (1.0 + sc2) + sh2).astype(BF16)
    acc = jnp.zeros(x.shape, F32)
    for c in range(D_FF // FF_CHUNK):
        sl = slice(c * FF_CHUNK, (c + 1) * FF_CHUNK)
        h = jnp.maximum(jnp.dot(u, w1_ref[:, sl], preferred_element_type=F32) + b1_ref[:, sl], 0.0)
        acc = acc + jnp.dot((h * h).astype(BF16), w2_ref[sl, :], preferred_element_type=F32)
    o_ref[0] = _layer_norm(DEEPNORM_ALPHA * x + g2 * (acc + b2_ref[...]), g_ref[...], b_ref[...])


def _mlp_call(x1, mod, w1, b1, w2, b2, g, bb, t0_mod):
    b, ln, d = x1.shape
    nt = ln // TM
    ctx_row = mod.shape[0] - SUBLANES

    def const2(shape):
        return pl.BlockSpec(shape, lambda i, t: (0, 0), pipeline_mode=pl.Buffered(1))

    return pl.pallas_call(
        _mlp_kernel,
        out_shape=jax.ShapeDtypeStruct((b, ln, d), F32),
        grid=(b, nt),
        in_specs=[
            pl.BlockSpec((1, TM, d), lambda i, t: (i, t, 0)),
            pl.BlockSpec((1, 6, d), lambda i, t: (jnp.where(t + t0_mod == 0, ctx_row, i), 0, 0)),
            const2(w1.shape), const2(b1.shape), const2(w2.shape), const2(b2.shape),
            const2(g.shape), const2(bb.shape),
        ],
        out_specs=pl.BlockSpec((1, TM, d), lambda i, t: (i, t, 0)),
        compiler_params=pltpu.CompilerParams(dimension_semantics=("parallel", "parallel"),
                                             vmem_limit_bytes=VMEM_LIMIT_BYTES),
        name="mlp",
    )(x1, mod, w1, b1, w2, b2, g, bb)


def _rope_tables(seq):
    half = QK_ROPE // 2
    nf = half // 2
    pos = np.arange(seq)
    inv_freq = ROPE_THETA ** (-np.arange(nf, dtype=np.float32) / nf)
    cos = np.ones((CTX_LEN + seq, LANES), np.float32)
    slo = np.zeros((CTX_LEN + seq, LANES), np.float32)
    shi = np.zeros((CTX_LEN + seq, LANES), np.float32)
    for part, p in enumerate((pos // GRID_W, pos % GRID_W)):
        ang = p.astype(np.float32)[:, None] * inv_freq[None, :]
        c, s = np.cos(ang), np.sin(ang)
        base = QK_NOPE + part * half
        cos[CTX_LEN:, base:base + nf] = c
        cos[CTX_LEN:, base + nf:base + half] = c
        slo[CTX_LEN:, base:base + nf] = -s
        shi[CTX_LEN:, base + nf:base + half] = s
    return jnp.asarray(cos), jnp.asarray(slo), jnp.asarray(shi)


def _pad_in_weights(w_in):
    d = w_in.shape[0]
    offs = np.cumsum([0, Q_LORA, KV_LORA, QK_ROPE, 2 * CONV_CH, ML_WIDTH, ML_WIDTH, ML_WIDTH, ML_WIDTH])
    o_cq, o_ckv, o_kr, o_conv, o_mq, o_mk, o_mv, o_mo, o_mg = [int(o) for o in offs]
    z = lambda n: jnp.zeros((d, n), w_in.dtype)
    mg = w_in[:, o_mg:o_mg + 4 * ML_HEADS]
    h = ML_HEADS
    gi = jnp.concatenate([mg[:, 0:h], mg[:, 2 * h:3 * h]], 1)
    gf = jnp.concatenate([mg[:, h:2 * h], mg[:, 3 * h:4 * h]], 1)
    cols = [
        w_in[:, o_cq:o_ckv], w_in[:, o_ckv:o_kr],
        z(QK_NOPE), w_in[:, o_kr:o_conv], z(LANES - QK_NOPE - QK_ROPE),
        w_in[:, o_conv:o_mg],
        gi, z(LANES - N_GATE), gf, z(LANES - N_GATE),
    ]
    return jnp.concatenate(cols, 1).astype(BF16)


def _pad_gate_bias(b_gates):
    h = ML_HEADS
    pad = jnp.zeros((LANES - N_GATE,), b_gates.dtype)
    bgi = jnp.concatenate([b_gates[0:h], b_gates[2 * h:3 * h], pad])[None, :]
    bgf = jnp.concatenate([b_gates[h:2 * h], b_gates[3 * h:4 * h], pad])[None, :]
    return bgi, bgf


def _pad_mla_weights(w_uq, w_ukv):
    dq = QK_NOPE + QK_ROPE
    wq = w_uq.reshape(Q_LORA, MLA_HEADS, dq)
    wq = jnp.pad(wq, ((0, 0), (0, 0), (0, HEAD_PAD - dq))).reshape(Q_LORA, MLA_HEADS * HEAD_PAD)
    wkv = w_ukv.reshape(KV_LORA, MLA_HEADS, QK_NOPE + V_HEAD)
    wk = jnp.pad(wkv[:, :, :QK_NOPE], ((0, 0), (0, 0), (0, HEAD_PAD - QK_NOPE)))
    wk = wk.reshape(KV_LORA, MLA_HEADS * HEAD_PAD)
    wv = wkv[:, :, QK_NOPE:].reshape(KV_LORA, MLA_HEADS * V_HEAD)
    return wq.astype(BF16), wk.astype(BF16), wv.astype(BF16)


def kernel(x, c, ctx, c_ctx, w_ada, b_ada, w_in, g_qn, w_uq, g_kvn, w_ukv, conv_w, conv_b, conv_ln_g, conv_ln_b, b_gates, ml_norm_g, w_out, ln1_g, ln1_b, w_mlp1, b_mlp1, w_mlp2, b_mlp2, ln2_g, ln2_b):
    b, seq, d = x.shape
    depth = w_in.shape[0]
    assert ctx.shape[1] == CTX_LEN == TM and seq % TM == 0 and d == D_MODEL and depth == DEPTH
    row = lambda a: a[None, :]

    cc = jnp.concatenate([c, c_ctx[None, :], jnp.zeros((SUBLANES - 1, d), c.dtype)], 0)
    mod_all = _ada_call(cc, w_ada, b_ada).reshape(depth, cc.shape[0], 6, d)
    cos, slo, shi = _rope_tables(seq)

    xx = jnp.concatenate([ctx, x], 1)
    for l in range(depth):
        last = l == depth - 1
        t0 = 1 if last else 0
        mod = mod_all[l]
        wq, wk, wv = _pad_mla_weights(w_uq[l], w_ukv[l])
        bgi, bgf = _pad_gate_bias(b_gates[l])
        q, k, v, z, mq, mkt, mv, mo, gir, gfr = _in_call(
            xx, mod, _pad_in_weights(w_in[l]), row(g_qn[l]), row(g_kvn[l]), wq, wk, wv, bgi, bgf, cos, slo, shi)
        att = _attn_call(q, k, v, t0)
        ml = _mlstm_call(mq, mkt, mv, mo, gir, gfr, row(ml_norm_g[l]))
        x1 = _post_call(xx, mod, att, z, ml, conv_w[l], row(conv_b[l]), row(conv_ln_g[l]), row(conv_ln_b[l]),
                        w_out[l].astype(BF16), row(ln1_g[l]), row(ln1_b[l]), t0)
        xx = _mlp_call(x1, mod, w_mlp1[l].astype(BF16), row(b_mlp1[l]), w_mlp2[l].astype(BF16),
                       row(b_mlp2[l]), row(ln2_g[l]), row(ln2_b[l]), t0)
    return xx
```

```python
import functools

import numpy as np
import jax
import jax.numpy as jnp
from jax import lax
from jax.experimental import pallas as pl
from jax.experimental.pallas import tpu as pltpu

F32 = jnp.float32
BF16 = jnp.bfloat16

LANES = 128
SUBLANES = 8
VMEM_LIMIT_BYTES = 56 * 1024 * 1024

D_MODEL = 1024
GRID_W = 64
CTX_LEN = 256
MLA_HEADS = 8
QK_NOPE = 64
QK_ROPE = 32
V_HEAD = 64
Q_LORA = 256
KV_LORA = 128
CONV_CH = 256
CONV_WIDTH = 31
ML_HEADS = 4
ML_HEAD_DIM = 64
ML_WIDTH = ML_HEADS * ML_HEAD_DIM
CHUNK = 128
D_FF = 4 * D_MODEL
ROPE_THETA = 10000.0
LN_EPS = 1e-5
RMS_EPS = 1e-6
DEPTH = 2
DEEPNORM_ALPHA = (2 * DEPTH) ** 0.25

TM = 256
HALO = 16
HEAD_PAD = LANES

C_CQ = 0
C_CKV = C_CQ + Q_LORA
C_KR = C_CKV + KV_LORA
C_CONV = C_KR + LANES
C_MQ = C_CONV + 2 * CONV_CH
C_MK = C_MQ + ML_WIDTH
C_MV = C_MK + ML_WIDTH
C_MO = C_MV + ML_WIDTH
C_GI = C_MO + ML_WIDTH
C_GF = C_GI + LANES
IN_PAD = C_GF + LANES
N_GATE = 2 * ML_HEADS
Q_SCALE = (QK_NOPE + QK_ROPE) ** -0.5 * float(np.log2(np.e))


def _layer_norm(v, g, b):
    mu = jnp.mean(v, -1, keepdims=True)
    d = v - mu
    var = jnp.mean(d * d, -1, keepdims=True)
    return d * lax.rsqrt(var + LN_EPS) * g + b


def _rms_norm(v, g):
    return v * lax.rsqrt(jnp.mean(v * v, -1, keepdims=True) + RMS_EPS) * g


def _sigmoid(v):
    return 1.0 / (1.0 + jnp.exp(-v))


def _log_sigmoid(v):
    return jnp.minimum(v, 0.0) - jnp.log1p(jnp.exp(-jnp.abs(v)))


def _rope(t, cos, sin_lo, sin_hi):
    return t * cos + pltpu.roll(t, LANES - 8, 1) * sin_lo + pltpu.roll(t, 8, 1) * sin_hi


def _ada_kernel(c_ref, w_ref, b_ref, o_ref):
    c = c_ref[...]
    s = (c * _sigmoid(c)).astype(BF16)
    o_ref[0] = jnp.dot(s, w_ref[0].astype(BF16), preferred_element_type=F32) + b_ref[0]


def _ada_call(cc, w_ada, b_ada):
    depth, d, n6 = w_ada.shape
    rows = cc.shape[0]
    nblk = n6 // d
    return pl.pallas_call(
        _ada_kernel,
        out_shape=jax.ShapeDtypeStruct((depth, rows, n6), F32),
        grid=(depth, nblk),
        in_specs=[
            pl.BlockSpec((rows, d), lambda l, n: (0, 0)),
            pl.BlockSpec((1, d, d), lambda l, n: (l, 0, n)),
            pl.BlockSpec((1, 1, d), lambda l, n: (l, 0, n)),
        ],
        out_specs=pl.BlockSpec((1, rows, d), lambda l, n: (l, 0, n)),
        compiler_params=pltpu.CompilerParams(dimension_semantics=("parallel", "parallel")),
        name="ada_mod",
    )(cc, w_ada, b_ada.reshape(depth, 1, n6))


def _in_kernel(x_ref, mod_ref, w_ref, gq_ref, gkv_ref, wuq_ref, wuk_ref, wuv_ref, bgi_ref, bgf_ref,
               cos_ref, slo_ref, shi_ref,
               q_ref, k_ref, v_ref, z_ref, mq_ref, mkt_ref, mv_ref, mo_ref, gir_ref, gfr_ref):
    x = x_ref[0]
    sh1 = mod_ref[0, 0:1, :]
    sc1 = mod_ref[0, 1:2, :]
    xm = (x * (1.0 + sc1) + sh1).astype(BF16)

    def proj(lo, hi):
        return jnp.dot(xm, w_ref[:, lo:hi], preferred_element_type=F32)

    cos = cos_ref[...]
    slo = slo_ref[...]
    shi = shi_ref[...]

    cqn = _rms_norm(proj(C_CQ, C_CQ + Q_LORA), gq_ref[...]).astype(BF16)
    qf = jnp.dot(cqn, wuq_ref[...], preferred_element_type=F32) * Q_SCALE
    for h in range(MLA_HEADS):
        sl = slice(h * HEAD_PAD, (h + 1) * HEAD_PAD)
        q_ref[0, :, sl] = _rope(qf[:, sl], cos, slo, shi).astype(BF16)

    ckv_kr = proj(C_CKV, C_KR + LANES)
    ckvn = _rms_norm(ckv_kr[:, :KV_LORA], gkv_ref[...]).astype(BF16)
    kr = _rope(ckv_kr[:, KV_LORA:], cos, slo, shi)
    kf = jnp.dot(ckvn, wuk_ref[...], preferred_element_type=F32)
    for h in range(MLA_HEADS):
        sl = slice(h * HEAD_PAD, (h + 1) * HEAD_PAD)
        k_ref[0, :, sl] = (kf[:, sl] + kr).astype(BF16)
    v_ref[0] = jnp.dot(ckvn, wuv_ref[...], preferred_element_type=F32).astype(BF16)

    a = proj(C_CONV, C_CONV + CONV_CH)
    gt = proj(C_CONV + CONV_CH, C_CONV + 2 * CONV_CH)
    z_ref[0] = a * _sigmoid(gt)

    mq_ref[0] = proj(C_MQ, C_MQ + ML_WIDTH).astype(BF16)
    mk_t = (proj(C_MK, C_MK + ML_WIDTH) * (ML_HEAD_DIM ** -0.5)).T
    mv_ref[0] = proj(C_MV, C_MV + ML_WIDTH).astype(BF16)
    mo_ref[0] = proj(C_MO, C_MO + ML_WIDTH)

    gates = proj(C_GI, C_GF + LANES)
    gi_t = (gates[:, :LANES] + bgi_ref[...]).T
    gf_t = _log_sigmoid(gates[:, LANES:] + bgf_ref[...]).T
    for cc in range(TM // CHUNK):
        cs = slice(cc * CHUNK, (cc + 1) * CHUNK)
        mkt_ref[0, cc] = mk_t[:, cs].astype(BF16)
        gir_ref[0, cc] = gi_t[:N_GATE, cs]
        gfr_ref[0, cc] = gf_t[:N_GATE, cs]


def _in_call(xx, mod, w_in_p, gq, gkv, wuq, wuk, wuv, bgi, bgf, cos, slo, shi):
    b, ltot, d = xx.shape
    nt = ltot // TM
    nc = ltot // CHUNK
    ctx_row = mod.shape[0] - SUBLANES

    def tok(width):
        return pl.BlockSpec((1, TM, width), lambda i, t: (i, t, 0))

    def const2(shape):
        return pl.BlockSpec(shape, lambda i, t: (0, 0))

    tab = pl.BlockSpec((TM, LANES), lambda i, t: (t, 0))
    grow = pl.BlockSpec((1, TM // CHUNK, N_GATE, CHUNK), lambda i, t: (i, t, 0, 0))
    outs = [
        (jax.ShapeDtypeStruct((b, ltot, MLA_HEADS * HEAD_PAD), BF16), tok(MLA_HEADS * HEAD_PAD)),
        (jax.ShapeDtypeStruct((b, ltot, MLA_HEADS * HEAD_PAD), BF16), tok(MLA_HEADS * HEAD_PAD)),
        (jax.ShapeDtypeStruct((b, ltot, MLA_HEADS * V_HEAD), BF16), tok(MLA_HEADS * V_HEAD)),
        (jax.ShapeDtypeStruct((b, ltot, CONV_CH), F32), tok(CONV_CH)),
        (jax.ShapeDtypeStruct((b, ltot, ML_WIDTH), BF16), tok(ML_WIDTH)),
        (jax.ShapeDtypeStruct((b, nc, ML_WIDTH, CHUNK), BF16),
         pl.BlockSpec((1, TM // CHUNK, ML_WIDTH, CHUNK), lambda i, t: (i, t, 0, 0))),
        (jax.ShapeDtypeStruct((b, ltot, ML_WIDTH), BF16), tok(ML_WIDTH)),
        (jax.ShapeDtypeStruct((b, ltot, ML_WIDTH), F32), tok(ML_WIDTH)),
        (jax.ShapeDtypeStruct((b, nc, N_GATE, CHUNK), F32), grow),
        (jax.ShapeDtypeStruct((b, nc, N_GATE, CHUNK), F32), grow),
    ]
    return pl.pallas_call(
        _in_kernel,
        out_shape=[o[0] for o in outs],
        grid=(b, nt),
        in_specs=[
            tok(d),
            pl.BlockSpec((1, 6, d), lambda i, t: (jnp.where(t == 0, ctx_row, i), 0, 0)),
            const2(w_in_p.shape),
            const2(gq.shape), const2(gkv.shape),
            const2(wuq.shape), const2(wuk.shape), const2(wuv.shape),
            const2(bgi.shape), const2(bgf.shape),
            tab, tab, tab,
        ],
        out_specs=[o[1] for o in outs],
        compiler_params=pltpu.CompilerParams(dimension_semantics=("parallel", "parallel"),
                                             vmem_limit_bytes=VMEM_LIMIT_BYTES),
        name="in_proj",
    )(xx, mod, w_in_p, gq, gkv, wuq, wuk, wuv, bgi, bgf, cos, slo, shi)


ATT_PAIRS = 2


def _attn_kernel(q_ref, k_ref, v_ref, o_ref, vaug_ref, *, t0, ltot):
    pair_w = 2 * V_HEAD

    @pl.when(pl.program_id(2) == 0)
    def _():
        ones = jnp.ones((ltot, pair_w), BF16)
        for pp in range(ATT_PAIRS):
            vaug_ref[pp] = jnp.concatenate([v_ref[0, :, pp * pair_w:(pp + 1) * pair_w], ones], axis=1)

    def run(nk):
        for pp in range(ATT_PAIRS):
            outs = []
            for hh in range(2):
                h = 2 * pp + hh
                sl = slice(h * HEAD_PAD, (h + 1) * HEAD_PAD)
                s = lax.dot_general(q_ref[0, :, sl], k_ref[0, :nk, sl], (((1,), (1,)), ((), ())),
                                    preferred_element_type=F32)
                m = jnp.max(s, -1, keepdims=True)
                p = jnp.exp2((s - m).astype(BF16))
                o = jnp.dot(p, vaug_ref[pp, :nk, :], preferred_element_type=F32)
                outs.append(o[:, :pair_w] / o[:, pair_w:])
            lane = lax.broadcasted_iota(jnp.int32, outs[0].shape, 1)
            o_ref[0, :, pp * pair_w:(pp + 1) * pair_w] = jnp.where(lane < V_HEAD, outs[0], outs[1]).astype(o_ref.dtype)

    if t0 == 0:
        t = pl.program_id(2)

        @pl.when(t == 0)
        def _():
            run(CTX_LEN)

        @pl.when(t > 0)
        def _():
            run(ltot)
    else:
        run(ltot)


def _attn_call(q, k, v, t0):
    b, ltot, _ = q.shape
    nt = ltot // TM - t0
    ngroup = MLA_HEADS // (2 * ATT_PAIRS)
    qk_w = 2 * ATT_PAIRS * HEAD_PAD
    v_w = 2 * ATT_PAIRS * V_HEAD
    return pl.pallas_call(
        functools.partial(_attn_kernel, t0=t0, ltot=ltot),
        out_shape=jax.ShapeDtypeStruct((b, nt * TM, MLA_HEADS * V_HEAD), BF16),
        grid=(b, ngroup, nt),
        in_specs=[
            pl.BlockSpec((1, TM, qk_w), lambda i, j, t: (i, t + t0, j)),
            pl.BlockSpec((1, ltot, qk_w), lambda i, j, t: (i, 0, j)),
            pl.BlockSpec((1, ltot, v_w), lambda i, j, t: (i, 0, j)),
        ],
        out_specs=pl.BlockSpec((1, TM, v_w), lambda i, j, t: (i, t, j)),
        scratch_shapes=[pltpu.VMEM((ATT_PAIRS, ltot, 4 * V_HEAD), BF16)],
        compiler_params=pltpu.CompilerParams(dimension_semantics=("parallel", "parallel", "arbitrary"),
                                             vmem_limit_bytes=VMEM_LIMIT_BYTES),
        name="mla_attn",
    )(q, k, v)


def _mlstm_kernel(q_ref, kt_ref, v_ref, mo_ref, gi_ref, gf_ref, g_ref,
                  o_ref, hf_ref, hb_ref, sn_ref, a_ref, rp_ref, e_ref, tot_ref, gmax_ref, *, nchunk, nctx):
    t_ = CHUNK
    npair = ML_HEADS // 2
    row = lax.broadcasted_iota(jnp.int32, (t_, t_), 0)
    col = lax.broadcasted_iota(jnp.int32, (t_, t_), 1)
    causal_masks = (col <= row, col >= row)
    blockdiag = (row < ML_HEAD_DIM) == (col < ML_HEAD_DIM)
    half0 = lax.broadcasted_iota(jnp.int32, (1, LANES), 1) < ML_HEAD_DIM
    rowhalf0 = lax.broadcasted_iota(jnp.int32, (LANES, 1), 0) < ML_HEAD_DIM
    lane8 = lax.broadcasted_iota(jnp.int32, (1, t_), 1)
    ones8 = jnp.ones((N_GATE, t_), F32)
    n_parts = 3
    ones_row0 = 2 * n_parts * N_GATE
    zpad = jnp.zeros((t_ - ones_row0 - N_GATE, t_), F32)
    ones_b = jnp.ones((t_, LANES), BF16)
    zeros_b = jnp.zeros((t_, LANES), BF16)
    half_ones = tuple(jnp.broadcast_to(jnp.where(half0, on, 1.0 - on), (t_, LANES)).astype(BF16) for on in (1.0, 0.0))
    sub16 = lax.broadcasted_iota(jnp.int32, (2 * SUBLANES, 2 * LANES), 0)

    r2 = lax.broadcasted_iota(jnp.int32, (t_, 2 * LANES), 0)
    l2 = lax.broadcasted_iota(jnp.int32, (t_, 2 * LANES), 1)
    blk, gate = r2 >> 3, r2 & 7
    lblk, lhalf = l2 >> 7, (l2 >> 6) & 1
    consts = {}
    for d in range(2):
        for p in range(npair):
            c0 = d * ML_HEADS + 2 * p
            neg_r = jnp.where(jnp.logical_and(blk < n_parts, gate == c0 + lblk), -1.0, 0.0)
            group = jnp.where(blk < n_parts, 0, jnp.where(blk < 2 * n_parts, 1, 2))
            bcast = jnp.where(jnp.logical_and(group == lblk, gate == c0 + lhalf), 1.0, 0.0)
            consts[d, p] = (neg_r.astype(BF16), bcast.astype(BF16))

    def split3(x):
        hi = x.astype(BF16).astype(F32)
        rem = x - hi
        mid = rem.astype(BF16).astype(F32)
        return [hi, mid, rem - mid]

    def lane_scan(x, op, fill, reverse):
        k = 1
        while k < t_:
            if reverse:
                shifted, valid = pltpu.roll(x, t_ - k, 1), lane8 < t_ - k
            else:
                shifted, valid = pltpu.roll(x, k, 1), lane8 >= k
            x = op(x, jnp.where(valid, shifted, fill))
            k *= 2
        return x

    nrow = nchunk * N_GATE
    is_fwd_row = (lax.broadcasted_iota(jnp.int32, (nrow, t_), 0) & (N_GATE - 1)) < ML_HEADS
    lf = gf_ref[0].reshape(nrow, t_)
    lf_parts = jnp.concatenate([part.astype(BF16) for part in split3(lf)], axis=1)
    tri_up = jnp.where(row <= col, 1.0, 0.0).astype(BF16)
    tri_dn = jnp.where(row >= col, 1.0, 0.0).astype(BF16)
    b = jnp.where(is_fwd_row,
                  jnp.dot(lf_parts, jnp.concatenate([tri_up] * n_parts, axis=0), preferred_element_type=F32),
                  jnp.dot(lf_parts, jnp.concatenate([tri_dn] * n_parts, axis=0), preferred_element_type=F32))
    tot = jnp.sum(lf, axis=1, keepdims=True)
    r = gi_ref[0].reshape(nrow, t_) - b
    rmax = jnp.where(is_fwd_row, lane_scan(r, jnp.maximum, -jnp.inf, False),
                     lane_scan(r, jnp.maximum, -jnp.inf, True))
    rlast = jnp.max(r, axis=1, keepdims=True)
    col_parts = split3(rmax) + split3(b)
    for c in range(nchunk):
        rows_c = slice(c * N_GATE, (c + 1) * N_GATE)
        packed = jnp.concatenate([part[rows_c] for part in col_parts] + [ones8, zpad], axis=0)
        a_ref[c] = packed.T.astype(BF16)
    for j, part in enumerate(split3(r)):
        rp_ref[j] = part
    e_ref[...] = jnp.exp(r - rlast)
    tot_ref[...] = jnp.broadcast_to(tot, (nrow, t_))
    gmax_ref[...] = jnp.broadcast_to(tot + rlast, (nrow, t_))

    def prep(d, c, m_prev):
        rows_c = pl.ds(pl.multiple_of(c * N_GATE, N_GATE), N_GATE)
        tot_c = tot_ref[rows_c, :]
        gmax_c = gmax_ref[rows_c, :]
        m_new = jnp.maximum(tot_c + m_prev, gmax_c)
        decay = jnp.exp(tot_c + m_prev - m_new)
        gamma = jnp.exp(gmax_c - m_new)
        return a_ref[c], [rp_ref[j, rows_c, :] for j in range(n_parts)], e_ref[rows_c, :], decay, gamma, m_new

    def pair_step(d, p, c, pre, m_prev, h_ref):
        a, r_parts, e_row, decay, gamma, _ = pre
        c0 = d * ML_HEADS + 2 * p
        c1 = c0 + 1
        sl = slice(p * LANES, (p + 1) * LANES)
        q = q_ref[0, c, :, sl]
        kt = kt_ref[0, c, sl, :]
        v = v_ref[0, c, :, sl]
        neg_r, bcast = consts[d, p]

        def rows2(x):
            return jnp.concatenate([jnp.broadcast_to(x[c0:c0 + 1, :], (2 * SUBLANES, t_)),
                                    jnp.broadcast_to(x[c1:c1 + 1, :], (2 * SUBLANES, t_))], axis=1)

        dyn = jnp.where(sub16 == 0, rows2(r_parts[0]),
                        jnp.where(sub16 == 1, rows2(r_parts[1]),
                                  jnp.where(sub16 == 2, rows2(r_parts[2]), 0.0))).astype(BF16)
        rhs_e = jnp.concatenate([neg_r[:ones_row0], dyn, neg_r[ones_row0 + 2 * SUBLANES:]], axis=0)
        z = jnp.dot(a, jnp.concatenate([rhs_e, bcast], axis=1), preferred_element_type=F32)
        r_bc = z[:, 2 * LANES:3 * LANES]
        b_bc = z[:, 3 * LANES:]

        kt_heads = jnp.concatenate([jnp.where(rowhalf0, kt, zeros_b), jnp.where(rowhalf0, zeros_b, kt)], axis=1)
        qk = jnp.dot(q, kt_heads, preferred_element_type=F32)
        nd = jnp.zeros((t_, 2 * LANES), F32)
        for hh in range(2):
            hs = slice(hh * LANES, (hh + 1) * LANES)
            wts = jnp.exp(jnp.where(causal_masks[d], z[:, hs], -jnp.inf))
            v_h = jnp.where(half0, v, zeros_b) if hh == 0 else jnp.where(half0, zeros_b, v)
            nd = nd + jnp.dot((qk[:, hs] * wts).astype(BF16), jnp.concatenate([v_h, half_ones[hh]], axis=1),
                              preferred_element_type=F32)

        sn = sn_ref[2 * p + d]
        qs = jnp.dot(q, sn.astype(BF16), preferred_element_type=F32)
        m_bc = jnp.where(half0, m_prev[c0:c0 + 1, :], m_prev[c1:c1 + 1, :])
        p_bc = jnp.maximum(m_bc, r_bc)
        alpha = jnp.exp(r_bc - p_bc)
        beta = jnp.exp(m_bc - p_bc)
        num = alpha * nd[:, :LANES] + beta * qs[:, :LANES]
        den = alpha * nd[:, LANES:] + beta * qs[:, LANES:]
        h_ref[c, :, sl] = num / jnp.maximum(jnp.abs(den), jnp.exp(-(b_bc + p_bc)))

        wkt = (kt.astype(F32) * jnp.where(rowhalf0, e_row[c0:c0 + 1, :], e_row[c1:c1 + 1, :])).astype(BF16)
        upd = jnp.dot(wkt, jnp.concatenate([v, ones_b], axis=1), preferred_element_type=F32)
        dec_rows = jnp.where(rowhalf0, decay[c0:c0 + 1, :], decay[c1:c1 + 1, :])
        gam_rows = jnp.where(blockdiag, jnp.where(rowhalf0, gamma[c0:c0 + 1, :], gamma[c1:c1 + 1, :]), 0.0)
        sn_ref[2 * p + d] = jnp.concatenate([dec_rows * sn[:, :LANES] + gam_rows * upd[:, :LANES],
                                             dec_rows * sn[:, LANES:] + gam_rows * upd[:, LANES:]], axis=1)

    sn_ref[...] = jnp.zeros(sn_ref.shape, F32)

    def body(i, carry):
        cb = jnp.where(i < nctx, nctx - 1 - i, nchunk - 1 + nctx - i)
        new = []
        for d, c, h_ref in ((0, i, hf_ref), (1, cb, hb_ref)):
            pre = prep(d, c, carry[d])
            for p in range(npair):
                pair_step(d, p, c, pre, carry[d], h_ref)
            new.append(pre[-1])
        return tuple(new)

    m_init = jnp.zeros((N_GATE, t_), F32)
    lax.fori_loop(0, nchunk, body, (m_init, m_init))

    g = g_ref[...]

    def fin(c, _):
        hh = (hf_ref[c] + hb_ref[c]) * _sigmoid(mo_ref[0, c])
        outs = []
        for pair in range(npair):
            hp_ = hh[:, pair * LANES:(pair + 1) * LANES]
            s_all = jnp.sum(hp_, -1, keepdims=True)
            s_lo = jnp.sum(jnp.where(half0, hp_, 0.0), -1, keepdims=True)
            mu = jnp.where(half0, s_lo, s_all - s_lo) * (1.0 / ML_HEAD_DIM)
            dl = hp_ - mu
            d2 = dl * dl
            v_all = jnp.sum(d2, -1, keepdims=True)
            v_lo = jnp.sum(jnp.where(half0, d2, 0.0), -1, keepdims=True)
            var = jnp.where(half0, v_lo, v_all - v_lo) * (1.0 / ML_HEAD_DIM)
            outs.append(dl * lax.rsqrt(var + LN_EPS))
        o_ref[0, c] = (jnp.concatenate(outs, -1) * g).astype(o_ref.dtype)
        return 0

    lax.fori_loop(0, nchunk, fin, 0)


def _mlstm_call(mq, mkt, mv, mo, gir, gfr, g):
    b, ltot, w = mq.shape
    nc = ltot // CHUNK
    r4 = lambda a: a.reshape(b, nc, CHUNK, a.shape[-1])
    blk = lambda rows, width: pl.BlockSpec((1, nc, rows, width), lambda i: (i, 0, 0, 0))
    out = pl.pallas_call(
        functools.partial(_mlstm_kernel, nchunk=nc, nctx=CTX_LEN // CHUNK),
        out_shape=jax.ShapeDtypeStruct((b, nc, CHUNK, w), BF16),
        grid=(b,),
        in_specs=[blk(CHUNK, w), blk(w, CHUNK), blk(CHUNK, w), blk(CHUNK, w),
                  blk(N_GATE, CHUNK), blk(N_GATE, CHUNK),
                  pl.BlockSpec((1, w), lambda i: (0, 0))],
        out_specs=blk(CHUNK, w),
        scratch_shapes=[pltpu.VMEM((nc, CHUNK, w), F32), pltpu.VMEM((nc, CHUNK, w), F32),
                        pltpu.VMEM((2 * (ML_HEADS // 2), LANES, 2 * LANES), F32),
                        pltpu.VMEM((nc, CHUNK, LANES), BF16), pltpu.VMEM((3, nc * N_GATE, CHUNK), F32),
                        pltpu.VMEM((nc * N_GATE, CHUNK), F32), pltpu.VMEM((nc * N_GATE, CHUNK), F32),
                        pltpu.VMEM((nc * N_GATE, CHUNK), F32)],
        compiler_params=pltpu.CompilerParams(dimension_semantics=("parallel",),
                                             vmem_limit_bytes=VMEM_LIMIT_BYTES),
        name="mlstm",
    )(r4(mq), mkt, r4(mv), r4(mo), gir, gfr, g)
    return out.reshape(b, ltot, w)


def _post_kernel(x_ref, mod_ref, att_ref, z_ref, zp_ref, zn_ref, ml_ref, cw_ref, cb_ref, cg_ref, cbb_ref,
                 wo_ref, g1_ref, b1_ref, o_ref, zbuf_ref, zsh_ref, *, t0, nt_all):
    t = pl.program_id(1) + t0
    left_ok = t >= 2
    right_ok = jnp.logical_and(t >= 1, t <= nt_all - 2)
    zbuf_ref[0:HALO, :] = jnp.where(left_ok, zp_ref[0], 0.0)
    zbuf_ref[HALO:HALO + TM, :] = z_ref[0]
    zbuf_ref[HALO + TM:, :] = jnp.where(right_ok, zn_ref[0], 0.0)
    off = HALO - CONV_WIDTH // 2
    span = TM + ((off + CONV_WIDTH - 1) // SUBLANES) * SUBLANES
    for s in range(SUBLANES):
        zsh_ref[s] = zbuf_ref[s:s + span, :]
    acc = jnp.zeros((TM, CONV_CH), F32)
    for j in range(CONV_WIDTH):
        s, a = (off + j) % SUBLANES, ((off + j) // SUBLANES) * SUBLANES
        acc = acc + cw_ref[j:j + 1, :] * zsh_ref[s, a:a + TM, :]
    cv = _layer_norm(acc + cb_ref[...], cg_ref[...], cbb_ref[...])
    cv = cv * _sigmoid(cv)

    n_att = MLA_HEADS * V_HEAD
    y = jnp.dot(att_ref[0], wo_ref[0:n_att, :], preferred_element_type=F32)
    y = y + jnp.dot(cv.astype(BF16), wo_ref[n_att:n_att + CONV_CH, :], preferred_element_type=F32)
    y = y + jnp.dot(ml_ref[0], wo_ref[n_att + CONV_CH:, :], preferred_element_type=F32)
    g1 = mod_ref[0, 2:3, :]
    o_ref[0] = _layer_norm(DEEPNORM_ALPHA * x_ref[0] + g1 * y, g1_ref[...], b1_ref[...])


def _post_call(xx, mod, att, z, ml, cw, cb, cg, cbb, wo, g1, b1, t0):
    b, ltot, d = xx.shape
    nt_all = ltot // TM
    nt = nt_all - t0
    ctx_row = mod.shape[0] - SUBLANES
    hpt = TM // HALO
    nhalo = ltot // HALO

    def tok(width, off):
        return pl.BlockSpec((1, TM, width), lambda i, t: (i, t + off, 0))

    def const2(shape):
        return pl.BlockSpec(shape, lambda i, t: (0, 0))

    return pl.pallas_call(
        functools.partial(_post_kernel, t0=t0, nt_all=nt_all),
        out_shape=jax.ShapeDtypeStruct((b, nt * TM, d), F32),
        grid=(b, nt),
        in_specs=[
            tok(d, t0),
            pl.BlockSpec((1, 6, d), lambda i, t: (jnp.where(t + t0 == 0, ctx_row, i), 0, 0)),
            tok(att.shape[-1], 0),
            tok(CONV_CH, t0),
            pl.BlockSpec((1, HALO, CONV_CH), lambda i, t: (i, jnp.maximum((t + t0) * hpt - 1, 0), 0)),
            pl.BlockSpec((1, HALO, CONV_CH), lambda i, t: (i, jnp.minimum((t + t0 + 1) * hpt, nhalo - 1), 0)),
            tok(ML_WIDTH, t0),
            const2(cw.shape), const2(cb.shape), const2(cg.shape), const2(cbb.shape),
            const2(wo.shape), const2(g1.shape), const2(b1.shape),
        ],
        out_specs=tok(d, 0),
        scratch_shapes=[pltpu.VMEM((TM + 2 * HALO, CONV_CH), F32),
                        pltpu.VMEM((SUBLANES, TM + 2 * HALO - SUBLANES, CONV_CH), F32)],
        compiler_params=pltpu.CompilerParams(dimension_semantics=("parallel", "parallel"),
                                             vmem_limit_bytes=VMEM_LIMIT_BYTES),
        name="out_proj",
    )(xx, mod, att, z, z, z, ml, cw, cb, cg, cbb, wo, g1, b1)


FF_CHUNK = 1024


def _mlp_kernel(x_ref, mod_ref, w1_ref, b1_ref, w2_ref, b2_ref, g_ref, b_ref, o_ref):
    x = x_ref[0]
    sh2 = mod_ref[0, 3:4, :]
    sc2 = mod_ref[0, 4:5, :]
    g2 = mod_ref[0, 5:6, :]
    u = (x * (1.0 + sc2) + sh2).astype(BF16)
    acc = jnp.zeros(x.shape, F32)
    for c in range(D_FF // FF_CHUNK):
        sl = slice(c * FF_CHUNK, (c + 1) * FF_CHUNK)
        h = jnp.maximum(jnp.dot(u, w1_ref[:, sl], preferred_element_type=F32) + b1_ref[:, sl], 0.0)
        acc = acc + jnp.dot((h * h).astype(BF16), w2_ref[sl, :], preferred_element_type=F32)
    o_ref[0] = _layer_norm(DEEPNORM_ALPHA * x + g2 * (acc + b2_ref[...]), g_ref[...], b_ref[...])


def _mlp_call(x1, mod, w1, b1, w2, b2, g, bb, t0_mod):
    b, ln, d = x1.shape
    nt = ln // TM
    ctx_row = mod.shape[0] - SUBLANES

    def const2(shape):
        return pl.BlockSpec(shape, lambda i, t: (0, 0), pipeline_mode=pl.Buffered(1))

    return pl.pallas_call(
        _mlp_kernel,
        out_shape=jax.ShapeDtypeStruct((b, ln, d), F32),
        grid=(b, nt),
        in_specs=[
            pl.BlockSpec((1, TM, d), lambda i, t: (i, t, 0)),
            pl.BlockSpec((1, 6, d), lambda i, t: (jnp.where(t + t0_mod == 0, ctx_row, i), 0, 0)),
            const2(w1.shape), const2(b1.shape), const2(w2.shape), const2(b2.shape),
            const2(g.shape), const2(bb.shape),
        ],
        out_specs=pl.BlockSpec((1, TM, d), lambda i, t: (i, t, 0)),
        compiler_params=pltpu.CompilerParams(dimension_semantics=("parallel", "parallel"),
                                             vmem_limit_bytes=VMEM_LIMIT_BYTES),
        name="mlp",
    )(x1, mod, w1, b1, w2, b2, g, bb)


def _rope_tables(seq):
    half = QK_ROPE // 2
    nf = half // 2
    pos = np.arange(seq)
    inv_freq = ROPE_THETA ** (-np.arange(nf, dtype=np.float32) / nf)
    cos = np.ones((CTX_LEN + seq, LANES), np.float32)
    slo = np.zeros((CTX_LEN + seq, LANES), np.float32)
    shi = np.zeros((CTX_LEN + seq, LANES), np.float32)
    for part, p in enumerate((pos // GRID_W, pos % GRID_W)):
        ang = p.astype(np.float32)[:, None] * inv_freq[None, :]
        c, s = np.cos(ang), np.sin(ang)
        base = QK_NOPE + part * half
        cos[CTX_LEN:, base:base + nf] = c
        cos[CTX_LEN:, base + nf:base + half] = c
        slo[CTX_LEN:, base:base + nf] = -s
        shi[CTX_LEN:, base + nf:base + half] = s
    return jnp.asarray(cos), jnp.asarray(slo), jnp.asarray(shi)


def _pad_in_weights(w_in):
    d = w_in.shape[0]
    offs = np.cumsum([0, Q_LORA, KV_LORA, QK_ROPE, 2 * CONV_CH, ML_WIDTH, ML_WIDTH, ML_WIDTH, ML_WIDTH])
    o_cq, o_ckv, o_kr, o_conv, o_mq, o_mk, o_mv, o_mo, o_mg = [int(o) for o in offs]
    z = lambda n: jnp.zeros((d, n), w_in.dtype)
    mg = w_in[:, o_mg:o_mg + 4 * ML_HEADS]
    h = ML_HEADS
    gi = jnp.concatenate([mg[:, 0:h], mg[:, 2 * h:3 * h]], 1)
    gf = jnp.concatenate([mg[:, h:2 * h], mg[:, 3 * h:4 * h]], 1)
    cols = [
        w_in[:, o_cq:o_ckv], w_in[:, o_ckv:o_kr],
        z(QK_NOPE), w_in[:, o_kr:o_conv], z(LANES - QK_NOPE - QK_ROPE),
        w_in[:, o_conv:o_mg],
        gi, z(LANES - N_GATE), gf, z(LANES - N_GATE),
    ]
    return jnp.concatenate(cols, 1).astype(BF16)


def _pad_gate_bias(b_gates):
    h = ML_HEADS
    pad = jnp.zeros((LANES - N_GATE,), b_gates.dtype)
    bgi = jnp.concatenate([b_gates[0:h], b_gates[2 * h:3 * h], pad])[None, :]
    bgf = jnp.concatenate([b_gates[h:2 * h], b_gates[3 * h:4 * h], pad])[None, :]
    return bgi, bgf


def _pad_mla_weights(w_uq, w_ukv):
    dq = QK_NOPE + QK_ROPE
    wq = w_uq.reshape(Q_LORA, MLA_HEADS, dq)
    wq = jnp.pad(wq, ((0, 0), (0, 0), (0, HEAD_PAD - dq))).reshape(Q_LORA, MLA_HEADS * HEAD_PAD)
    wkv = w_ukv.reshape(KV_LORA, MLA_HEADS, QK_NOPE + V_HEAD)
    wk = jnp.pad(wkv[:, :, :QK_NOPE], ((0, 0), (0, 0), (0, HEAD_PAD - QK_NOPE)))
    wk = wk.reshape(KV_LORA, MLA_HEADS * HEAD_PAD)
    wv = wkv[:, :, QK_NOPE:].reshape(KV_LORA, MLA_HEADS * V_HEAD)
    return wq.astype(BF16), wk.astype(BF16), wv.astype(BF16)


def kernel(x, c, ctx, c_ctx, w_ada, b_ada, w_in, g_qn, w_uq, g_kvn, w_ukv, conv_w, conv_b, conv_ln_g, conv_ln_b, b_gates, ml_norm_g, w_out, ln1_g, ln1_b, w_mlp1, b_mlp1, w_mlp2, b_mlp2, ln2_g, ln2_b):
    b, seq, d = x.shape
    depth = w_in.shape[0]
    assert ctx.shape[1] == CTX_LEN == TM and seq % TM == 0 and d == D_MODEL and depth == DEPTH
    row = lambda a: a[None, :]

    cc = jnp.concatenate([c, c_ctx[None, :], jnp.zeros((SUBLANES - 1, d), c.dtype)], 0)
    mod_all = _ada_call(cc, w_ada, b_ada).reshape(depth, cc.shape[0], 6, d)
    cos, slo, shi = _rope_tables(seq)

    xx = jnp.concatenate([ctx, x], 1)
    for l in range(depth):
        last = l == depth - 1
        t0 = 1 if last else 0
        mod = mod_all[l]
        wq, wk, wv = _pad_mla_weights(w_uq[l], w_ukv[l])
        bgi, bgf = _pad_gate_bias(b_gates[l])
        q, k, v, z, mq, mkt, mv, mo, gir, gfr = _in_call(
            xx, mod, _pad_in_weights(w_in[l]), row(g_qn[l]), row(g_kvn[l]), wq, wk, wv, bgi, bgf, cos, slo, shi)
        att = _attn_call(q, k, v, t0)
        ml = _mlstm_call(mq, mkt, mv, mo, gir, gfr, row(ml_norm_g[l]))
        x1 = _post_call(xx, mod, att, z, ml, conv_w[l], row(conv_b[l]), row(conv_ln_g[l]), row(conv_ln_b[l]),
                        w_out[l].astype(BF16), row(ln1_g[l]), row(ln1_b[l]), t0)
        xx = _mlp_call(x1, mod, w_mlp1[l].astype(BF16), row(b_mlp1[l]), w_mlp2[l].astype(BF16),
                       row(b_mlp2[l]), row(ln2_g[l]), row(ln2_b[l]), t0)
    return xx
```

```python
import functools

import numpy as np
import jax
import jax.numpy as jnp
from jax import lax
from jax.experimental import pallas as pl
from jax.experimental.pallas import tpu as pltpu

F32 = jnp.float32
BF16 = jnp.bfloat16

LANES = 128
SUBLANES = 8
VMEM_LIMIT_BYTES = 56 * 1024 * 1024

D_MODEL = 1024
GRID_W = 64
CTX_LEN = 256
MLA_HEADS = 8
QK_NOPE = 64
QK_ROPE = 32
V_HEAD = 64
Q_LORA = 256
KV_LORA = 128
CONV_CH = 256
CONV_WIDTH = 31
ML_HEADS = 4
ML_HEAD_DIM = 64
ML_WIDTH = ML_HEADS * ML_HEAD_DIM
CHUNK = 128
D_FF = 4 * D_MODEL
ROPE_THETA = 10000.0
LN_EPS = 1e-5
RMS_EPS = 1e-6
DEPTH = 2
DEEPNORM_ALPHA = (2 * DEPTH) ** 0.25

TM = 256
HALO = 16
HEAD_PAD = LANES

C_CQ = 0
C_CKV = C_CQ + Q_LORA
C_KR = C_CKV + KV_LORA
C_CONV = C_KR + LANES
C_MQ = C_CONV + 2 * CONV_CH
C_MK = C_MQ + ML_WIDTH
C_MV = C_MK + ML_WIDTH
C_MO = C_MV + ML_WIDTH
C_GI = C_MO + ML_WIDTH
C_GF = C_GI + LANES
IN_PAD = C_GF + LANES
N_GATE = 2 * ML_HEADS
Q_SCALE = (QK_NOPE + QK_ROPE) ** -0.5 * float(np.log2(np.e))


def _wide_tile(rows):
    for tile in (768, 512, TM):
        if rows % tile == 0:
            return tile
    raise ValueError(f"no token tile divides {rows}")


def _layer_norm(v, g, b):
    mu = jnp.mean(v, -1, keepdims=True)
    d = v - mu
    var = jnp.mean(d * d, -1, keepdims=True)
    return d * lax.rsqrt(var + LN_EPS) * g + b


def _rms_norm(v, g):
    return v * lax.rsqrt(jnp.mean(v * v, -1, keepdims=True) + RMS_EPS) * g


def _sigmoid(v):
    return 1.0 / (1.0 + jnp.exp(-v))


def _log_sigmoid(v):
    return jnp.minimum(v, 0.0) - jnp.log1p(jnp.exp(-jnp.abs(v)))


def _rope(t, cos, sin_lo, sin_hi):
    return t * cos + pltpu.roll(t, LANES - 8, 1) * sin_lo + pltpu.roll(t, 8, 1) * sin_hi


def _ada_kernel(c_ref, w_ref, b_ref, o_ref):
    c = c_ref[...]
    s = (c * _sigmoid(c)).astype(BF16)
    o_ref[0] = jnp.dot(s, w_ref[0].astype(BF16), preferred_element_type=F32) + b_ref[0]


def _ada_call(cc, w_ada, b_ada):
    depth, d, n6 = w_ada.shape
    rows = cc.shape[0]
    nblk = n6 // d
    return pl.pallas_call(
        _ada_kernel,
        out_shape=jax.ShapeDtypeStruct((depth, rows, n6), F32),
        grid=(depth, nblk),
        in_specs=[
            pl.BlockSpec((rows, d), lambda l, n: (0, 0)),
            pl.BlockSpec((1, d, d), lambda l, n: (l, 0, n)),
            pl.BlockSpec((1, 1, d), lambda l, n: (l, 0, n)),
        ],
        out_specs=pl.BlockSpec((1, rows, d), lambda l, n: (l, 0, n)),
        compiler_params=pltpu.CompilerParams(dimension_semantics=("parallel", "parallel")),
        name="ada_mod",
    )(cc, w_ada, b_ada.reshape(depth, 1, n6))


def _mod_rows(mod_ref, modc_ref, idx, rows, ctx_rows):
    m = mod_ref[0, idx:idx + 1, :]
    if ctx_rows == 0:
        return m
    is_ctx = jnp.logical_and(lax.broadcasted_iota(jnp.int32, (rows, 1), 0) < ctx_rows, pl.program_id(1) == 0)
    return jnp.where(is_ctx, modc_ref[0, idx:idx + 1, :], m)


def _in_kernel(x_ref, mod_ref, modc_ref, w_ref, gq_ref, gkv_ref, wuq_ref, wuk_ref, wuv_ref, bgi_ref, bgf_ref,
               cos_ref, slo_ref, shi_ref,
               q_ref, k_ref, v_ref, z_ref, mq_ref, mkt_ref, mv_ref, mo_ref, gir_ref, gfr_ref):
    x = x_ref[0]
    tm = x.shape[0]
    sh1 = _mod_rows(mod_ref, modc_ref, 0, tm, CTX_LEN)
    sc1 = _mod_rows(mod_ref, modc_ref, 1, tm, CTX_LEN)
    xm = (x * (1.0 + sc1) + sh1).astype(BF16)

    def proj(lo, hi):
        return jnp.dot(xm, w_ref[:, lo:hi], preferred_element_type=F32)

    cos = cos_ref[...]
    slo = slo_ref[...]
    shi = shi_ref[...]

    cqn = _rms_norm(proj(C_CQ, C_CQ + Q_LORA), gq_ref[...]).astype(BF16)
    qf = jnp.dot(cqn, wuq_ref[...], preferred_element_type=F32) * Q_SCALE
    for h in range(MLA_HEADS):
        sl = slice(h * HEAD_PAD, (h + 1) * HEAD_PAD)
        q_ref[0, :, sl] = _rope(qf[:, sl], cos, slo, shi).astype(BF16)

    ckv_kr = proj(C_CKV, C_KR + LANES)
    ckvn = _rms_norm(ckv_kr[:, :KV_LORA], gkv_ref[...]).astype(BF16)
    kr = _rope(ckv_kr[:, KV_LORA:], cos, slo, shi)
    kf = jnp.dot(ckvn, wuk_ref[...], preferred_element_type=F32)
    for h in range(MLA_HEADS):
        sl = slice(h * HEAD_PAD, (h + 1) * HEAD_PAD)
        k_ref[0, :, sl] = (kf[:, sl] + kr).astype(BF16)
    v_ref[0] = jnp.dot(ckvn, wuv_ref[...], preferred_element_type=F32).astype(BF16)

    a = proj(C_CONV, C_CONV + CONV_CH)
    gt = proj(C_CONV + CONV_CH, C_CONV + 2 * CONV_CH)
    z_ref[0] = a * _sigmoid(gt)

    mq_ref[0] = proj(C_MQ, C_MQ + ML_WIDTH).astype(BF16)
    mk_t = (proj(C_MK, C_MK + ML_WIDTH) * (ML_HEAD_DIM ** -0.5)).T
    mv_ref[0] = proj(C_MV, C_MV + ML_WIDTH).astype(BF16)
    mo_ref[0] = proj(C_MO, C_MO + ML_WIDTH)

    gates = proj(C_GI, C_GF + LANES)
    gi_t = (gates[:, :LANES] + bgi_ref[...]).T
    gf_t = _log_sigmoid(gates[:, LANES:] + bgf_ref[...]).T
    for cc in range(tm // CHUNK):
        cs = slice(cc * CHUNK, (cc + 1) * CHUNK)
        mkt_ref[0, cc] = mk_t[:, cs].astype(BF16)
        gir_ref[0, cc] = gi_t[:N_GATE, cs]
        gfr_ref[0, cc] = gf_t[:N_GATE, cs]


def _in_call(xx, mod, w_in_p, gq, gkv, wuq, wuk, wuv, bgi, bgf, cos, slo, shi):
    b, ltot, d = xx.shape
    tm = _wide_tile(ltot)
    nt = ltot // tm
    nc = ltot // CHUNK
    ctx_row = mod.shape[0] - SUBLANES

    def tok(width):
        return pl.BlockSpec((1, tm, width), lambda i, t: (i, t, 0))

    def const2(shape):
        return pl.BlockSpec(shape, lambda i, t: (0, 0), pipeline_mode=pl.Buffered(1))

    tab = pl.BlockSpec((tm, LANES), lambda i, t: (t, 0))
    grow = pl.BlockSpec((1, tm // CHUNK, N_GATE, CHUNK), lambda i, t: (i, t, 0, 0))
    outs = [
        (jax.ShapeDtypeStruct((b, ltot, MLA_HEADS * HEAD_PAD), BF16), tok(MLA_HEADS * HEAD_PAD)),
        (jax.ShapeDtypeStruct((b, ltot, MLA_HEADS * HEAD_PAD), BF16), tok(MLA_HEADS * HEAD_PAD)),
        (jax.ShapeDtypeStruct((b, ltot, MLA_HEADS * V_HEAD), BF16), tok(MLA_HEADS * V_HEAD)),
        (jax.ShapeDtypeStruct((b, ltot, CONV_CH), F32), tok(CONV_CH)),
        (jax.ShapeDtypeStruct((b, ltot, ML_WIDTH), BF16), tok(ML_WIDTH)),
        (jax.ShapeDtypeStruct((b, nc, ML_WIDTH, CHUNK), BF16),
         pl.BlockSpec((1, tm // CHUNK, ML_WIDTH, CHUNK), lambda i, t: (i, t, 0, 0))),
        (jax.ShapeDtypeStruct((b, ltot, ML_WIDTH), BF16), tok(ML_WIDTH)),
        (jax.ShapeDtypeStruct((b, ltot, ML_WIDTH), F32), tok(ML_WIDTH)),
        (jax.ShapeDtypeStruct((b, nc, N_GATE, CHUNK), F32), grow),
        (jax.ShapeDtypeStruct((b, nc, N_GATE, CHUNK), F32), grow),
    ]
    return pl.pallas_call(
        _in_kernel,
        out_shape=[o[0] for o in outs],
        grid=(b, nt),
        in_specs=[
            tok(d),
            pl.BlockSpec((1, 6, d), lambda i, t: (i, 0, 0)),
            pl.BlockSpec((1, 6, d), lambda i, t: (ctx_row, 0, 0)),
            const2(w_in_p.shape),
            const2(gq.shape), const2(gkv.shape),
            const2(wuq.shape), const2(wuk.shape), const2(wuv.shape),
            const2(bgi.shape), const2(bgf.shape),
            tab, tab, tab,
        ],
        out_specs=[o[1] for o in outs],
        compiler_params=pltpu.CompilerParams(dimension_semantics=("parallel", "parallel"),
                                             vmem_limit_bytes=VMEM_LIMIT_BYTES),
        name="in_proj",
    )(xx, mod, mod, w_in_p, gq, gkv, wuq, wuk, wuv, bgi, bgf, cos, slo, shi)


ATT_PAIRS = 4


def _attn_kernel(q_ref, k_ref, v_ref, o_ref, vaug_ref, *, t0, ltot):
    pair_w = 2 * V_HEAD

    @pl.when(pl.program_id(2) == 0)
    def _():
        ones = jnp.ones((ltot, pair_w), BF16)
        for pp in range(ATT_PAIRS):
            vaug_ref[pp] = jnp.concatenate([v_ref[0, :, pp * pair_w:(pp + 1) * pair_w], ones], axis=1)

    def run(nk):
        def scores(h):
            sl = slice(h * HEAD_PAD, (h + 1) * HEAD_PAD)
            return lax.dot_general(q_ref[0, :, sl], k_ref[0, :nk, sl], (((1,), (1,)), ((), ())),
                                   preferred_element_type=F32)

        def probs(s):
            return jnp.exp2((s - jnp.max(s, -1, keepdims=True)).astype(BF16))

        def weighted(h, p):
            o = jnp.dot(p, vaug_ref[h // 2, :nk, :], preferred_element_type=F32)
            return o[:, :pair_w] / o[:, pair_w:]

        nh = 2 * ATT_PAIRS
        s_next = scores(0)
        p_prev = None
        outs = []
        for h in range(nh):
            s_cur = s_next
            if h + 1 < nh:
                s_next = scores(h + 1)
            p_cur = probs(s_cur)
            if p_prev is not None:
                outs.append(weighted(h - 1, p_prev))
            p_prev = p_cur
        outs.append(weighted(nh - 1, p_prev))
        lane = lax.broadcasted_iota(jnp.int32, outs[0].shape, 1)
        for pp in range(ATT_PAIRS):
            o_ref[0, :, pp * pair_w:(pp + 1) * pair_w] = jnp.where(
                lane < V_HEAD, outs[2 * pp], outs[2 * pp + 1]).astype(o_ref.dtype)

    if t0 == 0:
        t = pl.program_id(2)

        @pl.when(t == 0)
        def _():
            run(CTX_LEN)

        @pl.when(t > 0)
        def _():
            run(ltot)
    else:
        run(ltot)


def _attn_call(q, k, v, t0):
    b, ltot, _ = q.shape
    nt = ltot // TM - t0
    ngroup = MLA_HEADS // (2 * ATT_PAIRS)
    qk_w = 2 * ATT_PAIRS * HEAD_PAD
    v_w = 2 * ATT_PAIRS * V_HEAD
    return pl.pallas_call(
        functools.partial(_attn_kernel, t0=t0, ltot=ltot),
        out_shape=jax.ShapeDtypeStruct((b, nt * TM, MLA_HEADS * V_HEAD), BF16),
        grid=(b, ngroup, nt),
        in_specs=[
            pl.BlockSpec((1, TM, qk_w), lambda i, j, t: (i, t + t0, j)),
            pl.BlockSpec((1, ltot, qk_w), lambda i, j, t: (i, 0, j)),
            pl.BlockSpec((1, ltot, v_w), lambda i, j, t: (i, 0, j)),
        ],
        out_specs=pl.BlockSpec((1, TM, v_w), lambda i, j, t: (i, t, j)),
        scratch_shapes=[pltpu.VMEM((ATT_PAIRS, ltot, 4 * V_HEAD), BF16)],
        compiler_params=pltpu.CompilerParams(dimension_semantics=("parallel", "parallel", "arbitrary"),
                                             vmem_limit_bytes=VMEM_LIMIT_BYTES),
        name="mla_attn",
    )(q, k, v)


def _mlstm_kernel(q_ref, kt_ref, v_ref, mo_ref, gi_ref, gf_ref, g_ref,
                  o_ref, hf_ref, hb_ref, sn_ref, a_ref, rp_ref, e_ref, tot_ref, gmax_ref, *, nchunk, nctx):
    t_ = CHUNK
    npair = ML_HEADS // 2
    row = lax.broadcasted_iota(jnp.int32, (t_, t_), 0)
    col = lax.broadcasted_iota(jnp.int32, (t_, t_), 1)
    causal_masks = (col <= row, col >= row)
    blockdiag = (row < ML_HEAD_DIM) == (col < ML_HEAD_DIM)
    half0 = lax.broadcasted_iota(jnp.int32, (1, LANES), 1) < ML_HEAD_DIM
    rowhalf0 = lax.broadcasted_iota(jnp.int32, (LANES, 1), 0) < ML_HEAD_DIM
    lane8 = lax.broadcasted_iota(jnp.int32, (1, t_), 1)
    ones8 = jnp.ones((N_GATE, t_), F32)
    n_parts = 3
    ones_row0 = 2 * n_parts * N_GATE
    zpad = jnp.zeros((t_ - ones_row0 - N_GATE, t_), F32)
    ones_b = jnp.ones((t_, LANES), BF16)
    zeros_b = jnp.zeros((t_, LANES), BF16)
    half_ones = tuple(jnp.broadcast_to(jnp.where(half0, on, 1.0 - on), (t_, LANES)).astype(BF16) for on in (1.0, 0.0))
    sub16 = lax.broadcasted_iota(jnp.int32, (2 * SUBLANES, 2 * LANES), 0)

    r2 = lax.broadcasted_iota(jnp.int32, (t_, 2 * LANES), 0)
    l2 = lax.broadcasted_iota(jnp.int32, (t_, 2 * LANES), 1)
    blk, gate = r2 >> 3, r2 & 7
    lblk, lhalf = l2 >> 7, (l2 >> 6) & 1
    consts = {}
    for d in range(2):
        for p in range(npair):
            c0 = d * ML_HEADS + 2 * p
            neg_r = jnp.where(jnp.logical_and(blk < n_parts, gate == c0 + lblk), -1.0, 0.0)
            group = jnp.where(blk < n_parts, 0, jnp.where(blk < 2 * n_parts, 1, 2))
            bcast = jnp.where(jnp.logical_and(group == lblk, gate == c0 + lhalf), 1.0, 0.0)
            consts[d, p] = (neg_r.astype(BF16), bcast.astype(BF16))

    def split3(x):
        hi = x.astype(BF16).astype(F32)
        rem = x - hi
        mid = rem.astype(BF16).astype(F32)
        return [hi, mid, rem - mid]

    def lane_scan(x, op, fill, reverse):
        k = 1
        while k < t_:
            if reverse:
                shifted, valid = pltpu.roll(x, t_ - k, 1), lane8 < t_ - k
            else:
                shifted, valid = pltpu.roll(x, k, 1), lane8 >= k
            x = op(x, jnp.where(valid, shifted, fill))
            k *= 2
        return x

    nrow = nchunk * N_GATE
    is_fwd_row = (lax.broadcasted_iota(jnp.int32, (nrow, t_), 0) & (N_GATE - 1)) < ML_HEADS
    lf = gf_ref[0].reshape(nrow, t_)
    lf_parts = jnp.concatenate([part.astype(BF16) for part in split3(lf)], axis=1)
    tri_up = jnp.where(row <= col, 1.0, 0.0).astype(BF16)
    tri_dn = jnp.where(row >= col, 1.0, 0.0).astype(BF16)
    b = jnp.where(is_fwd_row,
                  jnp.dot(lf_parts, jnp.concatenate([tri_up] * n_parts, axis=0), preferred_element_type=F32),
                  jnp.dot(lf_parts, jnp.concatenate([tri_dn] * n_parts, axis=0), preferred_element_type=F32))
    tot = jnp.sum(lf, axis=1, keepdims=True)
    r = gi_ref[0].reshape(nrow, t_) - b
    rmax = jnp.where(is_fwd_row, lane_scan(r, jnp.maximum, -jnp.inf, False),
                     lane_scan(r, jnp.maximum, -jnp.inf, True))
    rlast = jnp.max(r, axis=1, keepdims=True)
    col_parts = split3(rmax) + split3(b)
    for c in range(nchunk):
        rows_c = slice(c * N_GATE, (c + 1) * N_GATE)
        packed = jnp.concatenate([part[rows_c] for part in col_parts] + [ones8, zpad], axis=0)
        a_ref[c] = packed.T.astype(BF16)
    for j, part in enumerate(split3(r)):
        rp_ref[j] = part
    e_ref[...] = jnp.exp(r - rlast)
    tot_ref[...] = jnp.broadcast_to(tot, (nrow, t_))
    gmax_ref[...] = jnp.broadcast_to(tot + rlast, (nrow, t_))

    def prep(d, c, m_prev):
        rows_c = pl.ds(pl.multiple_of(c * N_GATE, N_GATE), N_GATE)
        tot_c = tot_ref[rows_c, :]
        gmax_c = gmax_ref[rows_c, :]
        m_new = jnp.maximum(tot_c + m_prev, gmax_c)
        decay = jnp.exp(tot_c + m_prev - m_new)
        gamma = jnp.exp(gmax_c - m_new)
        return a_ref[c], [rp_ref[j, rows_c, :] for j in range(n_parts)], e_ref[rows_c, :], decay, gamma, m_new

    def stage_mxu(d, p, c, pre):
        a, r_parts, e_row, _, _, _ = pre
        c0 = d * ML_HEADS + 2 * p
        c1 = c0 + 1
        sl = slice(p * LANES, (p + 1) * LANES)
        q = q_ref[0, c, :, sl]
        kt = kt_ref[0, c, sl, :]
        v = v_ref[0, c, :, sl]
        neg_r, bcast = consts[d, p]

        def rows2(x):
            return jnp.concatenate([jnp.broadcast_to(x[c0:c0 + 1, :], (2 * SUBLANES, t_)),
                                    jnp.broadcast_to(x[c1:c1 + 1, :], (2 * SUBLANES, t_))], axis=1)

        dyn = jnp.where(sub16 == 0, rows2(r_parts[0]),
                        jnp.where(sub16 == 1, rows2(r_parts[1]),
                                  jnp.where(sub16 == 2, rows2(r_parts[2]), 0.0))).astype(BF16)
        rhs_e = jnp.concatenate([neg_r[:ones_row0], dyn, neg_r[ones_row0 + 2 * SUBLANES:]], axis=0)
        z = jnp.dot(a, jnp.concatenate([rhs_e, bcast], axis=1), preferred_element_type=F32)
        kt_heads = jnp.concatenate([jnp.where(rowhalf0, kt, zeros_b), jnp.where(rowhalf0, zeros_b, kt)], axis=1)
        qk = jnp.dot(q, kt_heads, preferred_element_type=F32)
        wkt = (kt.astype(F32) * jnp.where(rowhalf0, e_row[c0:c0 + 1, :], e_row[c1:c1 + 1, :])).astype(BF16)
        upd = jnp.dot(wkt, jnp.concatenate([v, ones_b], axis=1), preferred_element_type=F32)
        sn = sn_ref[2 * p + d]
        qs = jnp.dot(q, sn.astype(BF16), preferred_element_type=F32)
        return z, qk, upd, sn, qs, v

    def stage_intra(d, z, qk, v):
        nd = jnp.zeros((t_, 2 * LANES), F32)
        for hh in range(2):
            hs = slice(hh * LANES, (hh + 1) * LANES)
            wts = jnp.exp(jnp.where(causal_masks[d], z[:, hs], -jnp.inf))
            v_h = jnp.where(half0, v, zeros_b) if hh == 0 else jnp.where(half0, zeros_b, v)
            nd = nd + jnp.dot((qk[:, hs] * wts).astype(BF16), jnp.concatenate([v_h, half_ones[hh]], axis=1),
                              preferred_element_type=F32)
        return nd

    def stage_out(d, p, c, pre, m_prev, h_ref, z, upd, sn, qs, nd):
        _, _, _, decay, gamma, _ = pre
        c0 = d * ML_HEADS + 2 * p
        c1 = c0 + 1
        sl = slice(p * LANES, (p + 1) * LANES)
        r_bc = z[:, 2 * LANES:3 * LANES]
        b_bc = z[:, 3 * LANES:]
        m_bc = jnp.where(half0, m_prev[c0:c0 + 1, :], m_prev[c1:c1 + 1, :])
        p_bc = jnp.maximum(m_bc, r_bc)
        alpha = jnp.exp(r_bc - p_bc)
        beta = jnp.exp(m_bc - p_bc)
        num = alpha * nd[:, :LANES] + beta * qs[:, :LANES]
        den = alpha * nd[:, LANES:] + beta * qs[:, LANES:]
        h_ref[c, :, sl] = num / jnp.maximum(jnp.abs(den), jnp.exp(-(b_bc + p_bc)))

        dec_rows = jnp.where(rowhalf0, decay[c0:c0 + 1, :], decay[c1:c1 + 1, :])
        gam_rows = jnp.where(blockdiag, jnp.where(rowhalf0, gamma[c0:c0 + 1, :], gamma[c1:c1 + 1, :]), 0.0)
        sn_ref[2 * p + d] = jnp.concatenate([dec_rows * sn[:, :LANES] + gam_rows * upd[:, :LANES],
                                             dec_rows * sn[:, LANES:] + gam_rows * upd[:, LANES:]], axis=1)

    sn_ref[...] = jnp.zeros(sn_ref.shape, F32)

    def body(i, carry):
        cb = jnp.where(i < nctx, nctx - 1 - i, nchunk - 1 + nctx - i)
        streams = [(d, p, c, h_ref) for d, c, h_ref in ((0, i, hf_ref), (1, cb, hb_ref)) for p in range(npair)]
        pres = {0: prep(0, i, carry[0]), 1: prep(1, cb, carry[1])}
        first = [stage_mxu(d, p, c, pres[d]) for d, p, c, _ in streams]
        intra = [stage_intra(d, z, qk, v) for (d, _, _, _), (z, qk, _, _, _, v) in zip(streams, first)]
        for (d, p, c, h_ref), (z, _, upd, sn, qs, _), nd in zip(streams, first, intra):
            stage_out(d, p, c, pres[d], carry[d], h_ref, z, upd, sn, qs, nd)
        return pres[0][-1], pres[1][-1]

    m_init = jnp.zeros((N_GATE, t_), F32)
    lax.fori_loop(0, nchunk, body, (m_init, m_init), unroll=2)

    g = g_ref[...]
    head_shift = ML_HEAD_DIM.bit_length() - 1
    head_of_row = lax.broadcasted_iota(jnp.int32, (ML_WIDTH, ML_WIDTH), 0) >> head_shift
    head_of_col = lax.broadcasted_iota(jnp.int32, (ML_WIDTH, ML_WIDTH), 1) >> head_shift
    avg = jnp.where(head_of_row == head_of_col, 1.0 / ML_HEAD_DIM, 0.0).astype(BF16)
    avg2 = jnp.concatenate([avg, avg], axis=0)

    def head_mean(x):
        hi = x.astype(BF16)
        lo = (x - hi.astype(F32)).astype(BF16)
        return jnp.dot(jnp.concatenate([hi, lo], axis=1), avg2, preferred_element_type=F32)

    def fin(c, _):
        hh = (hf_ref[c] + hb_ref[c]) * _sigmoid(mo_ref[0, c])
        dl = hh - head_mean(hh)
        var = head_mean(dl * dl)
        o_ref[0, c] = (dl * lax.rsqrt(var + LN_EPS) * g).astype(o_ref.dtype)
        return 0

    lax.fori_loop(0, nchunk, fin, 0, unroll=3)


def _mlstm_call(mq, mkt, mv, mo, gir, gfr, g):
    b, ltot, w = mq.shape
    nc = ltot // CHUNK
    r4 = lambda a: a.reshape(b, nc, CHUNK, a.shape[-1])
    blk = lambda rows, width: pl.BlockSpec((1, nc, rows, width), lambda i: (i, 0, 0, 0))
    out = pl.pallas_call(
        functools.partial(_mlstm_kernel, nchunk=nc, nctx=CTX_LEN // CHUNK),
        out_shape=jax.ShapeDtypeStruct((b, nc, CHUNK, w), BF16),
        grid=(b,),
        in_specs=[blk(CHUNK, w), blk(w, CHUNK), blk(CHUNK, w), blk(CHUNK, w),
                  blk(N_GATE, CHUNK), blk(N_GATE, CHUNK),
                  pl.BlockSpec((1, w), lambda i: (0, 0))],
        out_specs=blk(CHUNK, w),
        scratch_shapes=[pltpu.VMEM((nc, CHUNK, w), F32), pltpu.VMEM((nc, CHUNK, w), F32),
                        pltpu.VMEM((2 * (ML_HEADS // 2), LANES, 2 * LANES), F32),
                        pltpu.VMEM((nc, CHUNK, LANES), BF16), pltpu.VMEM((3, nc * N_GATE, CHUNK), F32),
                        pltpu.VMEM((nc * N_GATE, CHUNK), F32), pltpu.VMEM((nc * N_GATE, CHUNK), F32),
                        pltpu.VMEM((nc * N_GATE, CHUNK), F32)],
        compiler_params=pltpu.CompilerParams(dimension_semantics=("parallel",),
                                             vmem_limit_bytes=VMEM_LIMIT_BYTES),
        name="mlstm",
    )(r4(mq), mkt, r4(mv), r4(mo), gir, gfr, g)
    return out.reshape(b, ltot, w)


def _post_kernel(x_ref, mod_ref, att_ref, z_ref, zp_ref, zn_ref, ml_ref, cw_ref, cb_ref, cg_ref, cbb_ref,
                 wo_ref, g1_ref, b1_ref, o_ref, zbuf_ref, zsh_ref, *, t0, nt_all):
    t = pl.program_id(1) + t0
    left_ok = t >= 2
    right_ok = jnp.logical_and(t >= 1, t <= nt_all - 2)
    zbuf_ref[0:HALO, :] = jnp.where(left_ok, zp_ref[0], 0.0)
    zbuf_ref[HALO:HALO + TM, :] = z_ref[0]
    zbuf_ref[HALO + TM:, :] = jnp.where(right_ok, zn_ref[0], 0.0)
    off = HALO - CONV_WIDTH // 2
    span = TM + ((off + CONV_WIDTH - 1) // SUBLANES) * SUBLANES
    for s in range(SUBLANES):
        zsh_ref[s] = zbuf_ref[s:s + span, :]
    acc = jnp.zeros((TM, CONV_CH), F32)
    for j in range(CONV_WIDTH):
        s, a = (off + j) % SUBLANES, ((off + j) // SUBLANES) * SUBLANES
        acc = acc + cw_ref[j:j + 1, :] * zsh_ref[s, a:a + TM, :]
    cv = _layer_norm(acc + cb_ref[...], cg_ref[...], cbb_ref[...])
    cv = cv * _sigmoid(cv)

    n_att = MLA_HEADS * V_HEAD
    y = jnp.dot(att_ref[0], wo_ref[0:n_att, :], preferred_element_type=F32)
    y = y + jnp.dot(cv.astype(BF16), wo_ref[n_att:n_att + CONV_CH, :], preferred_element_type=F32)
    y = y + jnp.dot(ml_ref[0], wo_ref[n_att + CONV_CH:, :], preferred_element_type=F32)
    g1 = mod_ref[0, 2:3, :]
    o_ref[0] = _layer_norm(DEEPNORM_ALPHA * x_ref[0] + g1 * y, g1_ref[...], b1_ref[...])


def _post_call(xx, mod, att, z, ml, cw, cb, cg, cbb, wo, g1, b1, t0):
    b, ltot, d = xx.shape
    nt_all = ltot // TM
    nt = nt_all - t0
    ctx_row = mod.shape[0] - SUBLANES
    hpt = TM // HALO
    nhalo = ltot // HALO

    def tok(width, off):
        return pl.BlockSpec((1, TM, width), lambda i, t: (i, t + off, 0))

    def const2(shape):
        return pl.BlockSpec(shape, lambda i, t: (0, 0))

    return pl.pallas_call(
        functools.partial(_post_kernel, t0=t0, nt_all=nt_all),
        out_shape=jax.ShapeDtypeStruct((b, nt * TM, d), F32),
        grid=(b, nt),
        in_specs=[
            tok(d, t0),
            pl.BlockSpec((1, 6, d), lambda i, t: (jnp.where(t + t0 == 0, ctx_row, i), 0, 0)),
            tok(att.shape[-1], 0),
            tok(CONV_CH, t0),
            pl.BlockSpec((1, HALO, CONV_CH), lambda i, t: (i, jnp.maximum((t + t0) * hpt - 1, 0), 0)),
            pl.BlockSpec((1, HALO, CONV_CH), lambda i, t: (i, jnp.minimum((t + t0 + 1) * hpt, nhalo - 1), 0)),
            tok(ML_WIDTH, t0),
            const2(cw.shape), const2(cb.shape), const2(cg.shape), const2(cbb.shape),
            const2(wo.shape), const2(g1.shape), const2(b1.shape),
        ],
        out_specs=tok(d, 0),
        scratch_shapes=[pltpu.VMEM((TM + 2 * HALO, CONV_CH), F32),
                        pltpu.VMEM((SUBLANES, TM + 2 * HALO - SUBLANES, CONV_CH), F32)],
        compiler_params=pltpu.CompilerParams(dimension_semantics=("parallel", "parallel"),
                                             vmem_limit_bytes=VMEM_LIMIT_BYTES),
        name="out_proj",
    )(xx, mod, att, z, z, z, ml, cw, cb, cg, cbb, wo, g1, b1)


FF_CHUNK = 1024


def _mlp_kernel(x_ref, mod_ref, modc_ref, w1_ref, b1_ref, w2_ref, b2_ref, g_ref, b_ref, o_ref, *, ctx_rows):
    x = x_ref[0]
    sh2 = _mod_rows(mod_ref, modc_ref, 3, x.shape[0], ctx_rows)
    sc2 = _mod_rows(mod_ref, modc_ref, 4, x.shape[0], ctx_rows)
    g2 = _mod_rows(mod_ref, modc_ref, 5, x.shape[0], ctx_rows)
    u = (x * (1.0 + sc2) + sh2).astype(BF16)
    acc = jnp.zeros(x.shape, F32)
    for c in range(D_FF // FF_CHUNK):
        sl = slice(c * FF_CHUNK, (c + 1) * FF_CHUNK)
        h = jnp.maximum(jnp.dot(u, w1_ref[:, sl], preferred_element_type=F32) + b1_ref[:, sl], 0.0)
        acc = acc + jnp.dot((h * h).astype(BF16), w2_ref[sl, :], preferred_element_type=F32)
    o_ref[0] = _layer_norm(DEEPNORM_ALPHA * x + g2 * (acc + b2_ref[...]), g_ref[...], b_ref[...])


def _mlp_call(x1, mod, w1, b1, w2, b2, g, bb, with_ctx):
    b, ln, d = x1.shape
    tm = _wide_tile(ln)
    nt = ln // tm
    ctx_row = mod.shape[0] - SUBLANES

    def const2(shape):
        return pl.BlockSpec(shape, lambda i, t: (0, 0), pipeline_mode=pl.Buffered(1))

    return pl.pallas_call(
        functools.partial(_mlp_kernel, ctx_rows=CTX_LEN if with_ctx else 0),
        out_shape=jax.ShapeDtypeStruct((b, ln, d), F32),
        grid=(b, nt),
        in_specs=[
            pl.BlockSpec((1, tm, d), lambda i, t: (i, t, 0)),
            pl.BlockSpec((1, 6, d), lambda i, t: (i, 0, 0)),
            pl.BlockSpec((1, 6, d), lambda i, t: (ctx_row, 0, 0)),
            const2(w1.shape), const2(b1.shape), const2(w2.shape), const2(b2.shape),
            const2(g.shape), const2(bb.shape),
        ],
        out_specs=pl.BlockSpec((1, tm, d), lambda i, t: (i, t, 0)),
        compiler_params=pltpu.CompilerParams(dimension_semantics=("parallel", "parallel"),
                                             vmem_limit_bytes=VMEM_LIMIT_BYTES),
        name="mlp",
    )(x1, mod, mod, w1, b1, w2, b2, g, bb)


def _rope_tables(seq):
    half = QK_ROPE // 2
    nf = half // 2
    pos = np.arange(seq)
    inv_freq = ROPE_THETA ** (-np.arange(nf, dtype=np.float32) / nf)
    cos = np.ones((CTX_LEN + seq, LANES), np.float32)
    slo = np.zeros((CTX_LEN + seq, LANES), np.float32)
    shi = np.zeros((CTX_LEN + seq, LANES), np.float32)
    for part, p in enumerate((pos // GRID_W, pos % GRID_W)):
        ang = p.astype(np.float32)[:, None] * inv_freq[None, :]
        c, s = np.cos(ang), np.sin(ang)
        base = QK_NOPE + part * half
        cos[CTX_LEN:, base:base + nf] = c
        cos[CTX_LEN:, base + nf:base + half] = c
        slo[CTX_LEN:, base:base + nf] = -s
        shi[CTX_LEN:, base + nf:base + half] = s
    return jnp.asarray(cos), jnp.asarray(slo), jnp.asarray(shi)


def _pad_in_weights(w_in):
    d = w_in.shape[0]
    offs = np.cumsum([0, Q_LORA, KV_LORA, QK_ROPE, 2 * CONV_CH, ML_WIDTH, ML_WIDTH, ML_WIDTH, ML_WIDTH])
    o_cq, o_ckv, o_kr, o_conv, o_mq, o_mk, o_mv, o_mo, o_mg = [int(o) for o in offs]
    z = lambda n: jnp.zeros((d, n), w_in.dtype)
    mg = w_in[:, o_mg:o_mg + 4 * ML_HEADS]
    h = ML_HEADS
    gi = jnp.concatenate([mg[:, 0:h], mg[:, 2 * h:3 * h]], 1)
    gf = jnp.concatenate([mg[:, h:2 * h], mg[:, 3 * h:4 * h]], 1)
    cols = [
        w_in[:, o_cq:o_ckv], w_in[:, o_ckv:o_kr],
        z(QK_NOPE), w_in[:, o_kr:o_conv], z(LANES - QK_NOPE - QK_ROPE),
        w_in[:, o_conv:o_mg],
        gi, z(LANES - N_GATE), gf, z(LANES - N_GATE),
    ]
    return jnp.concatenate(cols, 1).astype(BF16)


def _pad_gate_bias(b_gates):
    h = ML_HEADS
    pad = jnp.zeros((LANES - N_GATE,), b_gates.dtype)
    bgi = jnp.concatenate([b_gates[0:h], b_gates[2 * h:3 * h], pad])[None, :]
    bgf = jnp.concatenate([b_gates[h:2 * h], b_gates[3 * h:4 * h], pad])[None, :]
    return bgi, bgf


def _pad_mla_weights(w_uq, w_ukv):
    dq = QK_NOPE + QK_ROPE
    wq = w_uq.reshape(Q_LORA, MLA_HEADS, dq)
    wq = jnp.pad(wq, ((0, 0), (0, 0), (0, HEAD_PAD - dq))).reshape(Q_LORA, MLA_HEADS * HEAD_PAD)
    wkv = w_ukv.reshape(KV_LORA, MLA_HEADS, QK_NOPE + V_HEAD)
    wk = jnp.pad(wkv[:, :, :QK_NOPE], ((0, 0), (0, 0), (0, HEAD_PAD - QK_NOPE)))
    wk = wk.reshape(KV_LORA, MLA_HEADS * HEAD_PAD)
    wv = wkv[:, :, QK_NOPE:].reshape(KV_LORA, MLA_HEADS * V_HEAD)
    return wq.astype(BF16), wk.astype(BF16), wv.astype(BF16)


def kernel(x, c, ctx, c_ctx, w_ada, b_ada, w_in, g_qn, w_uq, g_kvn, w_ukv, conv_w, conv_b, conv_ln_g, conv_ln_b, b_gates, ml_norm_g, w_out, ln1_g, ln1_b, w_mlp1, b_mlp1, w_mlp2, b_mlp2, ln2_g, ln2_b):
    b, seq, d = x.shape
    depth = w_in.shape[0]
    assert ctx.shape[1] == CTX_LEN == TM and seq % TM == 0 and d == D_MODEL and depth == DEPTH
    row = lambda a: a[None, :]

    cc = jnp.concatenate([c, c_ctx[None, :], jnp.zeros((SUBLANES - 1, d), c.dtype)], 0)
    mod_all = _ada_call(cc, w_ada, b_ada).reshape(depth, cc.shape[0], 6, d)
    cos, slo, shi = _rope_tables(seq)

    xx = jnp.concatenate([ctx, x], 1)
    for l in range(depth):
        last = l == depth - 1
        t0 = 1 if last else 0
        mod = mod_all[l]
        wq, wk, wv = _pad_mla_weights(w_uq[l], w_ukv[l])
        bgi, bgf = _pad_gate_bias(b_gates[l])
        q, k, v, z, mq, mkt, mv, mo, gir, gfr = _in_call(
            xx, mod, _pad_in_weights(w_in[l]), row(g_qn[l]), row(g_kvn[l]), wq, wk, wv, bgi, bgf, cos, slo, shi)
        att = _attn_call(q, k, v, t0)
        ml = _mlstm_call(mq, mkt, mv, mo, gir, gfr, row(ml_norm_g[l]))
        x1 = _post_call(xx, mod, att, z, ml, conv_w[l], row(conv_b[l]), row(conv_ln_g[l]), row(conv_ln_b[l]),
                        w_out[l].astype(BF16), row(ln1_g[l]), row(ln1_b[l]), t0)
        xx = _mlp_call(x1, mod, w_mlp1[l].astype(BF16), row(b_mlp1[l]), w_mlp2[l].astype(BF16),
                       row(b_mlp2[l]), row(ln2_g[l]), row(ln2_b[l]), with_ctx=not last)
    return xx
```

```python
import functools

import numpy as np
import jax
import jax.numpy as jnp
from jax import lax
from jax.experimental import pallas as pl
from jax.experimental.pallas import tpu as pltpu

F32 = jnp.float32
BF16 = jnp.bfloat16

LANES = 128
SUBLANES = 8
VMEM_LIMIT_BYTES = 56 * 1024 * 1024

D_MODEL = 1024
GRID_W = 64
CTX_LEN = 256
MLA_HEADS = 8
QK_NOPE = 64
QK_ROPE = 32
V_HEAD = 64
Q_LORA = 256
KV_LORA = 128
CONV_CH = 256
CONV_WIDTH = 31
ML_HEADS = 4
ML_HEAD_DIM = 64
ML_WIDTH = ML_HEADS * ML_HEAD_DIM
CHUNK = 128
D_FF = 4 * D_MODEL
ROPE_THETA = 10000.0
LN_EPS = 1e-5
RMS_EPS = 1e-6
DEPTH = 2
DEEPNORM_ALPHA = (2 * DEPTH) ** 0.25

TM = 256
HALO = 16
HEAD_PAD = LANES

C_CQ = 0
C_CKV = C_CQ + Q_LORA
C_KRG = C_CKV + KV_LORA
C_CONV = C_KRG + LANES
C_MQ = C_CONV + 2 * CONV_CH
C_MK = C_MQ + ML_WIDTH
C_MV = C_MK + ML_WIDTH
C_MO = C_MV + ML_WIDTH
IN_PAD = C_MO + ML_WIDTH
N_GATE = 2 * ML_HEADS
Q_SCALE = (QK_NOPE + QK_ROPE) ** -0.5 * float(np.log2(np.e))


def _wide_tile(rows):
    for tile in (768, 512, TM):
        if rows % tile == 0:
            return tile
    raise ValueError(f"no token tile divides {rows}")


def _layer_norm(v, g, b):
    mu = jnp.mean(v, -1, keepdims=True)
    d = v - mu
    var = jnp.mean(d * d, -1, keepdims=True)
    return d * lax.rsqrt(var + LN_EPS) * g + b


def _rms_norm(v, g):
    return v * lax.rsqrt(jnp.mean(v * v, -1, keepdims=True) + RMS_EPS) * g


def _sigmoid(v):
    return 1.0 / (1.0 + jnp.exp(-v))


def _log_sigmoid(v):
    return jnp.minimum(v, 0.0) - jnp.log1p(jnp.exp(-jnp.abs(v)))


def _rope(t, cos, sin_lo, sin_hi):
    return t * cos + pltpu.roll(t, LANES - 8, 1) * sin_lo + pltpu.roll(t, 8, 1) * sin_hi


def _ada_kernel(c_ref, w_ref, b_ref, o_ref):
    c = c_ref[...]
    s = (c * _sigmoid(c)).astype(BF16)
    o_ref[0] = jnp.dot(s, w_ref[0].astype(BF16), preferred_element_type=F32) + b_ref[0]


def _ada_call(cc, w_ada, b_ada):
    depth, d, n6 = w_ada.shape
    rows = cc.shape[0]
    nblk = n6 // d
    return pl.pallas_call(
        _ada_kernel,
        out_shape=jax.ShapeDtypeStruct((depth, rows, n6), F32),
        grid=(depth, nblk),
        in_specs=[
            pl.BlockSpec((rows, d), lambda l, n: (0, 0)),
            pl.BlockSpec((1, d, d), lambda l, n: (l, 0, n)),
            pl.BlockSpec((1, 1, d), lambda l, n: (l, 0, n)),
        ],
        out_specs=pl.BlockSpec((1, rows, d), lambda l, n: (l, 0, n)),
        compiler_params=pltpu.CompilerParams(dimension_semantics=("parallel", "parallel")),
        name="ada_mod",
    )(cc, w_ada, b_ada.reshape(depth, 1, n6))


def _mod_rows(mod_ref, modc_ref, idx, rows, ctx_rows):
    m = mod_ref[0, idx:idx + 1, :]
    if ctx_rows == 0:
        return m
    is_ctx = jnp.logical_and(lax.broadcasted_iota(jnp.int32, (rows, 1), 0) < ctx_rows, pl.program_id(1) == 0)
    return jnp.where(is_ctx, modc_ref[0, idx:idx + 1, :], m)


def _in_kernel(*refs, n_x):
    x_refs = refs[:n_x]
    (mod_ref, modc_ref, w_ref, gq_ref, gkv_ref, wuq_ref, wuk_ref, wuv_ref, bg_ref, cos_ref, slo_ref, shi_ref,
     q_ref, k_ref, v_ref, z_ref, mq_ref, mkt_ref, mv_ref, mo_ref, gir_ref, gfr_ref) = refs[n_x:]
    if n_x == 1:
        x = x_refs[0][0]
        tm = x.shape[0]
        sh1 = _mod_rows(mod_ref, modc_ref, 0, tm, CTX_LEN)
        sc1 = _mod_rows(mod_ref, modc_ref, 1, tm, CTX_LEN)
        xm = (x * (1.0 + sc1) + sh1).astype(BF16)
    else:
        ctx_ref, first_ref = x_refs[0], x_refs[1]
        is_ctx = pl.program_id(1) == 0
        pieces = []
        for j, piece_ref in enumerate(x_refs[1:]):
            sh1, sc1 = mod_ref[0, 0:1, :], mod_ref[0, 1:2, :]
            piece = piece_ref[0]
            if j == 0:
                sh1 = jnp.where(is_ctx, modc_ref[0, 0:1, :], sh1)
                sc1 = jnp.where(is_ctx, modc_ref[0, 1:2, :], sc1)
                piece = jnp.where(is_ctx, ctx_ref[0], first_ref[0])
            pieces.append((piece * (1.0 + sc1) + sh1).astype(BF16))
        xm = jnp.concatenate(pieces, axis=0)
        tm = xm.shape[0]

    def proj(lo, hi):
        return jnp.dot(xm, w_ref[:, lo:hi], preferred_element_type=F32)

    cos = cos_ref[...]
    slo = slo_ref[...]
    shi = shi_ref[...]

    cq = proj(C_CQ, C_CQ + Q_LORA)
    ckv_krg = proj(C_CKV, C_KRG + LANES)
    a = proj(C_CONV, C_CONV + CONV_CH)
    cqn = _rms_norm(cq, gq_ref[...]).astype(BF16)
    gt = proj(C_CONV + CONV_CH, C_CONV + 2 * CONV_CH)
    ckvn = _rms_norm(ckv_krg[:, :KV_LORA], gkv_ref[...]).astype(BF16)
    krg = ckv_krg[:, KV_LORA:]

    qf = jnp.dot(cqn, wuq_ref[...], preferred_element_type=F32) * Q_SCALE
    z_ref[0] = a * _sigmoid(gt)
    mq = proj(C_MQ, C_MQ + ML_WIDTH)
    kf = jnp.dot(ckvn, wuk_ref[...], preferred_element_type=F32)
    for h in range(MLA_HEADS):
        sl = slice(h * HEAD_PAD, (h + 1) * HEAD_PAD)
        q_ref[0, :, sl] = _rope(qf[:, sl], cos, slo, shi).astype(BF16)
    mq_ref[0] = mq.astype(BF16)

    mk = proj(C_MK, C_MK + ML_WIDTH)
    vf = jnp.dot(ckvn, wuv_ref[...], preferred_element_type=F32)
    lane = lax.broadcasted_iota(jnp.int32, (1, LANES), 1)
    is_rope_lane = jnp.logical_and(lane >= QK_NOPE, lane < QK_NOPE + QK_ROPE)
    kr = jnp.where(is_rope_lane, _rope(krg, cos, slo, shi), 0.0)
    for h in range(MLA_HEADS):
        sl = slice(h * HEAD_PAD, (h + 1) * HEAD_PAD)
        k_ref[0, :, sl] = (kf[:, sl] + kr).astype(BF16)
    mv = proj(C_MV, C_MV + ML_WIDTH)
    mo = proj(C_MO, C_MO + ML_WIDTH)
    v_ref[0] = vf.astype(BF16)

    mk_t = (mk * (ML_HEAD_DIM ** -0.5)).T
    g_t = (krg + bg_ref[...]).T
    gi_t = g_t[:N_GATE]
    gf_t = _log_sigmoid(g_t[N_GATE:2 * N_GATE])
    for cc in range(tm // CHUNK):
        cs = slice(cc * CHUNK, (cc + 1) * CHUNK)
        mkt_ref[0, cc] = mk_t[:, cs].astype(BF16)
        gir_ref[0, cc] = gi_t[:, cs]
        gfr_ref[0, cc] = gf_t[:, cs]
    mv_ref[0] = mv.astype(BF16)
    mo_ref[0] = mo


def _in_call(stream, mod, w_in_p, gq, gkv, wuq, wuk, wuv, bg, cos, slo, shi):
    split = isinstance(stream, tuple)
    if split:
        ctx, x = stream
        b, seq, d = x.shape
        ltot = CTX_LEN + seq
    else:
        b, ltot, d = stream.shape
    tm = _wide_tile(ltot)
    nt = ltot // tm
    nc = ltot // CHUNK
    ctx_row = mod.shape[0] - SUBLANES

    def tok(width):
        return pl.BlockSpec((1, tm, width), lambda i, t: (i, t, 0))

    def const2(shape):
        return pl.BlockSpec(shape, lambda i, t: (0, 0), pipeline_mode=pl.Buffered(1))

    tab = pl.BlockSpec((tm, LANES), lambda i, t: (t, 0))
    grow = pl.BlockSpec((1, tm // CHUNK, N_GATE, CHUNK), lambda i, t: (i, t, 0, 0))
    outs = [
        (jax.ShapeDtypeStruct((b, ltot, MLA_HEADS * HEAD_PAD), BF16), tok(MLA_HEADS * HEAD_PAD)),
        (jax.ShapeDtypeStruct((b, ltot, MLA_HEADS * HEAD_PAD), BF16), tok(MLA_HEADS * HEAD_PAD)),
        (jax.ShapeDtypeStruct((b, ltot, MLA_HEADS * V_HEAD), BF16), tok(MLA_HEADS * V_HEAD)),
        (jax.ShapeDtypeStruct((b, ltot, CONV_CH), F32), tok(CONV_CH)),
        (jax.ShapeDtypeStruct((b, ltot, ML_WIDTH), BF16), tok(ML_WIDTH)),
        (jax.ShapeDtypeStruct((b, nc, ML_WIDTH, CHUNK), BF16),
         pl.BlockSpec((1, tm // CHUNK, ML_WIDTH, CHUNK), lambda i, t: (i, t, 0, 0))),
        (jax.ShapeDtypeStruct((b, ltot, ML_WIDTH), BF16), tok(ML_WIDTH)),
        (jax.ShapeDtypeStruct((b, ltot, ML_WIDTH), F32), tok(ML_WIDTH)),
        (jax.ShapeDtypeStruct((b, nc, N_GATE, CHUNK), F32), grow),
        (jax.ShapeDtypeStruct((b, nc, N_GATE, CHUNK), F32), grow),
    ]
    if split:
        ppt = tm // TM
        x_args = [ctx] + [x] * ppt
        x_specs = [pl.BlockSpec((1, TM, d), lambda i, t: (i, 0, 0))] + [
            pl.BlockSpec((1, TM, d), lambda i, t, j=j: (i, jnp.maximum(t * ppt + j - 1, 0), 0)) for j in range(ppt)]
    else:
        x_args, x_specs = [stream], [tok(d)]
    return pl.pallas_call(
        functools.partial(_in_kernel, n_x=len(x_args)),
        out_shape=[o[0] for o in outs],
        grid=(b, nt),
        in_specs=x_specs + [
            pl.BlockSpec((1, 6, d), lambda i, t: (i, 0, 0)),
            pl.BlockSpec((1, 6, d), lambda i, t: (ctx_row, 0, 0)),
            const2(w_in_p.shape),
            const2(gq.shape), const2(gkv.shape),
            const2(wuq.shape), const2(wuk.shape), const2(wuv.shape),
            const2(bg.shape),
            tab, tab, tab,
        ],
        out_specs=[o[1] for o in outs],
        compiler_params=pltpu.CompilerParams(dimension_semantics=("parallel", "parallel"),
                                             vmem_limit_bytes=VMEM_LIMIT_BYTES),
        name="in_proj",
    )(*x_args, mod, mod, w_in_p, gq, gkv, wuq, wuk, wuv, bg, cos, slo, shi)


ATT_PAIRS = 4


def _attn_kernel(q_ref, k_ref, v_ref, o_ref, vaug_ref, *, t0, ltot):
    pair_w = 2 * V_HEAD
    nh = 2 * ATT_PAIRS

    @pl.when(pl.program_id(2) == 0)
    def _():
        ones = jnp.ones((ltot, pair_w), BF16)
        for pp in range(ATT_PAIRS):
            vaug_ref[pp] = jnp.concatenate([v_ref[0, :, pp * pair_w:(pp + 1) * pair_w], ones], axis=1)

    def run(nk):
        def scores(h):
            sl = slice(h * HEAD_PAD, (h + 1) * HEAD_PAD)
            return lax.dot_general(q_ref[0, :, sl], k_ref[0, :nk, sl], (((1,), (1,)), ((), ())),
                                   preferred_element_type=F32)

        def probs(s):
            return jnp.exp2((s - jnp.max(s, -1, keepdims=True)).astype(BF16))

        def weighted(h, p):
            o = jnp.dot(p, vaug_ref[h // 2, :nk, :], preferred_element_type=F32)
            return o[:, :pair_w] / o[:, pair_w:]

        s_next = scores(0)
        p_prev = None
        outs = []
        for h in range(nh):
            s_cur = s_next
            if h + 1 < nh:
                s_next = scores(h + 1)
            p_cur = probs(s_cur)
            if p_prev is not None:
                outs.append(weighted(h - 1, p_prev))
            p_prev = p_cur
        outs.append(weighted(nh - 1, p_prev))
        lane = lax.broadcasted_iota(jnp.int32, outs[0].shape, 1)
        for pp in range(ATT_PAIRS):
            o_ref[0, :, pp * pair_w:(pp + 1) * pair_w] = jnp.where(
                lane < V_HEAD, outs[2 * pp], outs[2 * pp + 1]).astype(o_ref.dtype)

    if t0 == 0:
        t = pl.program_id(2)

        @pl.when(t == 0)
        def _():
            run(CTX_LEN)

        @pl.when(t > 0)
        def _():
            run(ltot)
    else:
        run(ltot)


def _attn_call(q, k, v, t0):
    b, ltot, _ = q.shape
    nt = ltot // TM - t0
    ngroup = MLA_HEADS // (2 * ATT_PAIRS)
    qk_w = 2 * ATT_PAIRS * HEAD_PAD
    v_w = 2 * ATT_PAIRS * V_HEAD
    return pl.pallas_call(
        functools.partial(_attn_kernel, t0=t0, ltot=ltot),
        out_shape=jax.ShapeDtypeStruct((b, nt * TM, MLA_HEADS * V_HEAD), BF16),
        grid=(b, ngroup, nt),
        in_specs=[
            pl.BlockSpec((1, TM, qk_w), lambda i, j, t: (i, t + t0, j)),
            pl.BlockSpec((1, ltot, qk_w), lambda i, j, t: (i, 0, j)),
            pl.BlockSpec((1, ltot, v_w), lambda i, j, t: (i, 0, j)),
        ],
        out_specs=pl.BlockSpec((1, TM, v_w), lambda i, j, t: (i, t, j)),
        scratch_shapes=[pltpu.VMEM((ATT_PAIRS, ltot, 4 * V_HEAD), BF16)],
        compiler_params=pltpu.CompilerParams(dimension_semantics=("parallel", "parallel", "arbitrary"),
                                             vmem_limit_bytes=VMEM_LIMIT_BYTES),
        name="mla_attn",
    )(q, k, v)


def _mlstm_kernel(q_ref, kt_ref, v_ref, mo_ref, gi_ref, gf_ref, g_ref,
                  o_ref, hf_ref, hb_ref, sn_ref, a_ref, rp_ref, e_ref, tot_ref, gmax_ref, *, nchunk, nctx):
    t_ = CHUNK
    npair = ML_HEADS // 2
    row = lax.broadcasted_iota(jnp.int32, (t_, t_), 0)
    col = lax.broadcasted_iota(jnp.int32, (t_, t_), 1)
    causal_masks = (col <= row, col >= row)
    blockdiag = (row < ML_HEAD_DIM) == (col < ML_HEAD_DIM)
    half0 = lax.broadcasted_iota(jnp.int32, (1, LANES), 1) < ML_HEAD_DIM
    rowhalf0 = lax.broadcasted_iota(jnp.int32, (LANES, 1), 0) < ML_HEAD_DIM
    lane8 = lax.broadcasted_iota(jnp.int32, (1, t_), 1)
    ones8 = jnp.ones((N_GATE, t_), F32)
    n_parts = 3
    ones_row0 = 2 * n_parts * N_GATE
    zpad = jnp.zeros((t_ - ones_row0 - N_GATE, t_), F32)
    ones_b = jnp.ones((t_, LANES), BF16)
    zeros_b = jnp.zeros((t_, LANES), BF16)
    half_ones = tuple(jnp.broadcast_to(jnp.where(half0, on, 1.0 - on), (t_, LANES)).astype(BF16) for on in (1.0, 0.0))
    sub16 = lax.broadcasted_iota(jnp.int32, (2 * SUBLANES, 2 * LANES), 0)

    r2 = lax.broadcasted_iota(jnp.int32, (t_, 2 * LANES), 0)
    l2 = lax.broadcasted_iota(jnp.int32, (t_, 2 * LANES), 1)
    blk, gate = r2 >> 3, r2 & 7
    lblk, lhalf = l2 >> 7, (l2 >> 6) & 1
    consts = {}
    for d in range(2):
        for p in range(npair):
            c0 = d * ML_HEADS + 2 * p
            neg_r = jnp.where(jnp.logical_and(blk < n_parts, gate == c0 + lblk), -1.0, 0.0)
            group = jnp.where(blk < n_parts, 0, jnp.where(blk < 2 * n_parts, 1, 2))
            bcast = jnp.where(jnp.logical_and(group == lblk, gate == c0 + lhalf), 1.0, 0.0)
            consts[d, p] = (neg_r.astype(BF16), bcast.astype(BF16))

    def split3(x):
        hi = x.astype(BF16).astype(F32)
        rem = x - hi
        mid = rem.astype(BF16).astype(F32)
        return [hi, mid, rem - mid]

    def lane_scan(x, op, fill, reverse):
        k = 1
        while k < t_:
            if reverse:
                shifted, valid = pltpu.roll(x, t_ - k, 1), lane8 < t_ - k
            else:
                shifted, valid = pltpu.roll(x, k, 1), lane8 >= k
            x = op(x, jnp.where(valid, shifted, fill))
            k *= 2
        return x

    nrow = nchunk * N_GATE
    is_fwd_row = (lax.broadcasted_iota(jnp.int32, (nrow, t_), 0) & (N_GATE - 1)) < ML_HEADS
    lf = gf_ref[0].reshape(nrow, t_)
    lf_parts = jnp.concatenate([part.astype(BF16) for part in split3(lf)], axis=1)
    tri_up = jnp.where(row <= col, 1.0, 0.0).astype(BF16)
    tri_dn = jnp.where(row >= col, 1.0, 0.0).astype(BF16)
    b = jnp.where(is_fwd_row,
                  jnp.dot(lf_parts, jnp.concatenate([tri_up] * n_parts, axis=0), preferred_element_type=F32),
                  jnp.dot(lf_parts, jnp.concatenate([tri_dn] * n_parts, axis=0), preferred_element_type=F32))
    tot = jnp.sum(lf, axis=1, keepdims=True)
    r = gi_ref[0].reshape(nrow, t_) - b
    rmax = jnp.where(is_fwd_row, lane_scan(r, jnp.maximum, -jnp.inf, False),
                     lane_scan(r, jnp.maximum, -jnp.inf, True))
    rlast = jnp.max(r, axis=1, keepdims=True)
    col_parts = split3(rmax) + split3(b)
    for c in range(nchunk):
        rows_c = slice(c * N_GATE, (c + 1) * N_GATE)
        packed = jnp.concatenate([part[rows_c] for part in col_parts] + [ones8, zpad], axis=0)
        a_ref[c] = packed.T.astype(BF16)
    for j, part in enumerate(split3(r)):
        rp_ref[j] = part
    e_ref[...] = jnp.exp(r - rlast)
    tot_ref[...] = jnp.broadcast_to(tot, (nrow, t_))
    gmax_ref[...] = jnp.broadcast_to(tot + rlast, (nrow, t_))

    def prep(d, c, m_prev):
        rows_c = pl.ds(pl.multiple_of(c * N_GATE, N_GATE), N_GATE)
        tot_c = tot_ref[rows_c, :]
        gmax_c = gmax_ref[rows_c, :]
        m_new = jnp.maximum(tot_c + m_prev, gmax_c)
        decay = jnp.exp(tot_c + m_prev - m_new)
        gamma = jnp.exp(gmax_c - m_new)
        return a_ref[c], [rp_ref[j, rows_c, :] for j in range(n_parts)], e_ref[rows_c, :], decay, gamma, m_new

    def stage_mxu(d, p, c, pre):
        a, r_parts, e_row, _, _, _ = pre
        c0 = d * ML_HEADS + 2 * p
        c1 = c0 + 1
        sl = slice(p * LANES, (p + 1) * LANES)
        q = q_ref[0, c, :, sl]
        kt = kt_ref[0, c, sl, :]
        v = v_ref[0, c, :, sl]
        neg_r, bcast = consts[d, p]

        def rows2(x):
            return jnp.concatenate([jnp.broadcast_to(x[c0:c0 + 1, :], (2 * SUBLANES, t_)),
                                    jnp.broadcast_to(x[c1:c1 + 1, :], (2 * SUBLANES, t_))], axis=1)

        dyn = jnp.where(sub16 == 0, rows2(r_parts[0]),
                        jnp.where(sub16 == 1, rows2(r_parts[1]),
                                  jnp.where(sub16 == 2, rows2(r_parts[2]), 0.0))).astype(BF16)
        rhs_e = jnp.concatenate([neg_r[:ones_row0], dyn, neg_r[ones_row0 + 2 * SUBLANES:]], axis=0)
        z = jnp.dot(a, jnp.concatenate([rhs_e, bcast], axis=1), preferred_element_type=F32)
        kt_heads = jnp.concatenate([jnp.where(rowhalf0, kt, zeros_b), jnp.where(rowhalf0, zeros_b, kt)], axis=1)
        qk = jnp.dot(q, kt_heads, preferred_element_type=F32)
        wkt = (kt.astype(F32) * jnp.where(rowhalf0, e_row[c0:c0 + 1, :], e_row[c1:c1 + 1, :])).astype(BF16)
        upd = jnp.dot(wkt, jnp.concatenate([v, ones_b], axis=1), preferred_element_type=F32)
        sn = sn_ref[2 * p + d]
        qs = jnp.dot(q, sn.astype(BF16), preferred_element_type=F32)
        return z, qk, upd, sn, qs, v

    def stage_intra(d, z, qk, v):
        nd = jnp.zeros((t_, 2 * LANES), F32)
        for hh in range(2):
            hs = slice(hh * LANES, (hh + 1) * LANES)
            wts = jnp.exp(jnp.where(causal_masks[d], z[:, hs], -jnp.inf))
            v_h = jnp.where(half0, v, zeros_b) if hh == 0 else jnp.where(half0, zeros_b, v)
            nd = nd + jnp.dot((qk[:, hs] * wts).astype(BF16), jnp.concatenate([v_h, half_ones[hh]], axis=1),
                              preferred_element_type=F32)
        return nd

    def stage_out(d, p, c, pre, m_prev, h_ref, z, upd, sn, qs, nd):
        _, _, _, decay, gamma, _ = pre
        c0 = d * ML_HEADS + 2 * p
        c1 = c0 + 1
        sl = slice(p * LANES, (p + 1) * LANES)
        r_bc = z[:, 2 * LANES:3 * LANES]
        b_bc = z[:, 3 * LANES:]
        m_bc = jnp.where(half0, m_prev[c0:c0 + 1, :], m_prev[c1:c1 + 1, :])
        p_bc = jnp.maximum(m_bc, r_bc)
        alpha = jnp.exp(r_bc - p_bc)
        beta = jnp.exp(m_bc - p_bc)
        num = alpha * nd[:, :LANES] + beta * qs[:, :LANES]
        den = alpha * nd[:, LANES:] + beta * qs[:, LANES:]
        h_ref[c, :, sl] = num / jnp.maximum(jnp.abs(den), jnp.exp(-(b_bc + p_bc)))

        dec_rows = jnp.where(rowhalf0, decay[c0:c0 + 1, :], decay[c1:c1 + 1, :])
        gam_rows = jnp.where(blockdiag, jnp.where(rowhalf0, gamma[c0:c0 + 1, :], gamma[c1:c1 + 1, :]), 0.0)
        sn_ref[2 * p + d] = jnp.concatenate([dec_rows * sn[:, :LANES] + gam_rows * upd[:, :LANES],
                                             dec_rows * sn[:, LANES:] + gam_rows * upd[:, LANES:]], axis=1)

    sn_ref[...] = jnp.zeros(sn_ref.shape, F32)

    def body(i, carry):
        cb = jnp.where(i < nctx, nctx - 1 - i, nchunk - 1 + nctx - i)
        streams = [(d, p, c, h_ref) for d, c, h_ref in ((0, i, hf_ref), (1, cb, hb_ref)) for p in range(npair)]
        pres = {0: prep(0, i, carry[0]), 1: prep(1, cb, carry[1])}
        first = [stage_mxu(d, p, c, pres[d]) for d, p, c, _ in streams]
        intra = [stage_intra(d, z, qk, v) for (d, _, _, _), (z, qk, _, _, _, v) in zip(streams, first)]
        for (d, p, c, h_ref), (z, _, upd, sn, qs, _), nd in zip(streams, first, intra):
            stage_out(d, p, c, pres[d], carry[d], h_ref, z, upd, sn, qs, nd)
        return pres[0][-1], pres[1][-1]

    m_init = jnp.zeros((N_GATE, t_), F32)
    lax.fori_loop(0, nchunk, body, (m_init, m_init), unroll=2)

    g = g_ref[...]
    head_shift = ML_HEAD_DIM.bit_length() - 1
    head_of_row = lax.broadcasted_iota(jnp.int32, (ML_WIDTH, ML_WIDTH), 0) >> head_shift
    head_of_col = lax.broadcasted_iota(jnp.int32, (ML_WIDTH, ML_WIDTH), 1) >> head_shift
    avg = jnp.where(head_of_row == head_of_col, 1.0 / ML_HEAD_DIM, 0.0).astype(BF16)
    avg2 = jnp.concatenate([avg, avg], axis=0)

    def head_mean(x):
        hi = x.astype(BF16)
        lo = (x - hi.astype(F32)).astype(BF16)
        return jnp.dot(jnp.concatenate([hi, lo], axis=1), avg2, preferred_element_type=F32)

    def fin(c, _):
        hh = (hf_ref[c] + hb_ref[c]) * _sigmoid(mo_ref[0, c])
        dl = hh - head_mean(hh)
        var = head_mean(dl * dl)
        o_ref[0, c] = (dl * lax.rsqrt(var + LN_EPS) * g).astype(o_ref.dtype)
        return 0

    lax.fori_loop(0, nchunk, fin, 0, unroll=3)


def _mlstm_call(mq, mkt, mv, mo, gir, gfr, g):
    b, ltot, w = mq.shape
    nc = ltot // CHUNK
    r4 = lambda a: a.reshape(b, nc, CHUNK, a.shape[-1])
    blk = lambda rows, width: pl.BlockSpec((1, nc, rows, width), lambda i: (i, 0, 0, 0))
    out = pl.pallas_call(
        functools.partial(_mlstm_kernel, nchunk=nc, nctx=CTX_LEN // CHUNK),
        out_shape=jax.ShapeDtypeStruct((b, nc, CHUNK, w), BF16),
        grid=(b,),
        in_specs=[blk(CHUNK, w), blk(w, CHUNK), blk(CHUNK, w), blk(CHUNK, w),
                  blk(N_GATE, CHUNK), blk(N_GATE, CHUNK),
                  pl.BlockSpec((1, w), lambda i: (0, 0))],
        out_specs=blk(CHUNK, w),
        scratch_shapes=[pltpu.VMEM((nc, CHUNK, w), F32), pltpu.VMEM((nc, CHUNK, w), F32),
                        pltpu.VMEM((2 * (ML_HEADS // 2), LANES, 2 * LANES), F32),
                        pltpu.VMEM((nc, CHUNK, LANES), BF16), pltpu.VMEM((3, nc * N_GATE, CHUNK), F32),
                        pltpu.VMEM((nc * N_GATE, CHUNK), F32), pltpu.VMEM((nc * N_GATE, CHUNK), F32),
                        pltpu.VMEM((nc * N_GATE, CHUNK), F32)],
        compiler_params=pltpu.CompilerParams(dimension_semantics=("parallel",),
                                             vmem_limit_bytes=VMEM_LIMIT_BYTES),
        name="mlstm",
    )(r4(mq), mkt, r4(mv), r4(mo), gir, gfr, g)
    return out.reshape(b, ltot, w)


def _post_kernel(*refs, t0, nt_all, split):
    if split:
        ctx_ref, x_ref = refs[:2]
        resid = jnp.where(pl.program_id(1) == 0, ctx_ref[0], x_ref[0])
    else:
        resid = refs[0][0]
    (mod_ref, att_ref, z_ref, zp_ref, zn_ref, ml_ref, cw_ref, cb_ref, cg_ref, cbb_ref,
     wo_ref, g1_ref, b1_ref, o_ref, zbuf_ref, zsh_ref) = refs[2 if split else 1:]
    t = pl.program_id(1) + t0
    n_att = MLA_HEADS * V_HEAD
    y = jnp.dot(att_ref[0], wo_ref[0:n_att, :], preferred_element_type=F32)
    y = y + jnp.dot(ml_ref[0], wo_ref[n_att + CONV_CH:, :], preferred_element_type=F32)
    left_ok = t >= 2
    right_ok = jnp.logical_and(t >= 1, t <= nt_all - 2)
    zbuf_ref[0:HALO, :] = jnp.where(left_ok, zp_ref[0], 0.0)
    zbuf_ref[HALO:HALO + TM, :] = z_ref[0]
    zbuf_ref[HALO + TM:, :] = jnp.where(right_ok, zn_ref[0], 0.0)
    off = HALO - CONV_WIDTH // 2
    span = TM + ((off + CONV_WIDTH - 1) // SUBLANES) * SUBLANES
    for s in range(SUBLANES):
        zsh_ref[s] = zbuf_ref[s:s + span, :]
    acc = jnp.zeros((TM, CONV_CH), F32)
    for j in range(CONV_WIDTH):
        s, a = (off + j) % SUBLANES, ((off + j) // SUBLANES) * SUBLANES
        acc = acc + cw_ref[j:j + 1, :] * zsh_ref[s, a:a + TM, :]
    cv = _layer_norm(acc + cb_ref[...], cg_ref[...], cbb_ref[...])
    cv = cv * _sigmoid(cv)

    y = y + jnp.dot(cv.astype(BF16), wo_ref[n_att:n_att + CONV_CH, :], preferred_element_type=F32)
    g1 = mod_ref[0, 2:3, :]
    o_ref[0] = _layer_norm(DEEPNORM_ALPHA * resid + g1 * y, g1_ref[...], b1_ref[...])


def _post_call(stream, mod, att, z, ml, cw, cb, cg, cbb, wo, g1, b1, t0):
    split = isinstance(stream, tuple)
    if split:
        ctx, x = stream
        b, seq, d = x.shape
        ltot = CTX_LEN + seq
    else:
        b, ltot, d = stream.shape
    nt_all = ltot // TM
    nt = nt_all - t0
    ctx_row = mod.shape[0] - SUBLANES
    hpt = TM // HALO
    nhalo = ltot // HALO

    def tok(width, off):
        return pl.BlockSpec((1, TM, width), lambda i, t: (i, t + off, 0))

    def const2(shape):
        return pl.BlockSpec(shape, lambda i, t: (0, 0))

    if split:
        assert t0 == 0
        x_args = [ctx, x]
        x_specs = [pl.BlockSpec((1, TM, d), lambda i, t: (i, 0, 0)),
                   pl.BlockSpec((1, TM, d), lambda i, t: (i, jnp.maximum(t - 1, 0), 0))]
    else:
        x_args, x_specs = [stream], [tok(d, t0)]
    return pl.pallas_call(
        functools.partial(_post_kernel, t0=t0, nt_all=nt_all, split=split),
        out_shape=jax.ShapeDtypeStruct((b, nt * TM, d), F32),
        grid=(b, nt),
        in_specs=x_specs + [
            pl.BlockSpec((1, 6, d), lambda i, t: (jnp.where(t + t0 == 0, ctx_row, i), 0, 0)),
            tok(att.shape[-1], 0),
            tok(CONV_CH, t0),
            pl.BlockSpec((1, HALO, CONV_CH), lambda i, t: (i, jnp.maximum((t + t0) * hpt - 1, 0), 0)),
            pl.BlockSpec((1, HALO, CONV_CH), lambda i, t: (i, jnp.minimum((t + t0 + 1) * hpt, nhalo - 1), 0)),
            tok(ML_WIDTH, t0),
            const2(cw.shape), const2(cb.shape), const2(cg.shape), const2(cbb.shape),
            const2(wo.shape), const2(g1.shape), const2(b1.shape),
        ],
        out_specs=tok(d, 0),
        scratch_shapes=[pltpu.VMEM((TM + 2 * HALO, CONV_CH), F32),
                        pltpu.VMEM((SUBLANES, TM + 2 * HALO - SUBLANES, CONV_CH), F32)],
        compiler_params=pltpu.CompilerParams(dimension_semantics=("parallel", "parallel"),
                                             vmem_limit_bytes=VMEM_LIMIT_BYTES),
        name="out_proj",
    )(*x_args, mod, att, z, z, z, ml, cw, cb, cg, cbb, wo, g1, b1)


FF_CHUNK = 1024


def _mlp_kernel(x_ref, mod_ref, modc_ref, w1_ref, b1_ref, w2_ref, b2_ref, g_ref, b_ref, o_ref, *, ctx_rows):
    x = x_ref[0]
    sh2 = _mod_rows(mod_ref, modc_ref, 3, x.shape[0], ctx_rows)
    sc2 = _mod_rows(mod_ref, modc_ref, 4, x.shape[0], ctx_rows)
    g2 = _mod_rows(mod_ref, modc_ref, 5, x.shape[0], ctx_rows)
    u = (x * (1.0 + sc2) + sh2).astype(BF16)
    def hidden(c):
        sl = slice(c * FF_CHUNK, (c + 1) * FF_CHUNK)
        return jnp.dot(u, w1_ref[:, sl], preferred_element_type=F32) + b1_ref[:, sl]

    acc = jnp.zeros(x.shape, F32)
    n_chunk = D_FF // FF_CHUNK
    pre = hidden(0)
    for c in range(n_chunk):
        cur = pre
        if c + 1 < n_chunk:
            pre = hidden(c + 1)
        h = jnp.maximum(cur, 0.0)
        acc = acc + jnp.dot((h * h).astype(BF16), w2_ref[c * FF_CHUNK:(c + 1) * FF_CHUNK, :],
                            preferred_element_type=F32)
    o_ref[0] = _layer_norm(DEEPNORM_ALPHA * x + g2 * (acc + b2_ref[...]), g_ref[...], b_ref[...])


def _mlp_call(x1, mod, w1, b1, w2, b2, g, bb, with_ctx):
    b, ln, d = x1.shape
    tm = _wide_tile(ln)
    nt = ln // tm
    ctx_row = mod.shape[0] - SUBLANES

    def const2(shape):
        return pl.BlockSpec(shape, lambda i, t: (0, 0), pipeline_mode=pl.Buffered(1))

    return pl.pallas_call(
        functools.partial(_mlp_kernel, ctx_rows=CTX_LEN if with_ctx else 0),
        out_shape=jax.ShapeDtypeStruct((b, ln, d), F32),
        grid=(b, nt),
        in_specs=[
            pl.BlockSpec((1, tm, d), lambda i, t: (i, t, 0)),
            pl.BlockSpec((1, 6, d), lambda i, t: (i, 0, 0)),
            pl.BlockSpec((1, 6, d), lambda i, t: (ctx_row, 0, 0)),
            const2(w1.shape), const2(b1.shape), const2(w2.shape), const2(b2.shape),
            const2(g.shape), const2(bb.shape),
        ],
        out_specs=pl.BlockSpec((1, tm, d), lambda i, t: (i, t, 0)),
        compiler_params=pltpu.CompilerParams(dimension_semantics=("parallel", "parallel"),
                                             vmem_limit_bytes=VMEM_LIMIT_BYTES),
        name="mlp",
    )(x1, mod, mod, w1, b1, w2, b2, g, bb)


def _rope_tables(seq):
    half = QK_ROPE // 2
    nf = half // 2
    pos = np.arange(seq)
    inv_freq = ROPE_THETA ** (-np.arange(nf, dtype=np.float32) / nf)
    cos = np.ones((CTX_LEN + seq, LANES), np.float32)
    slo = np.zeros((CTX_LEN + seq, LANES), np.float32)
    shi = np.zeros((CTX_LEN + seq, LANES), np.float32)
    for part, p in enumerate((pos // GRID_W, pos % GRID_W)):
        ang = p.astype(np.float32)[:, None] * inv_freq[None, :]
        c, s = np.cos(ang), np.sin(ang)
        base = QK_NOPE + part * half
        cos[CTX_LEN:, base:base + nf] = c
        cos[CTX_LEN:, base + nf:base + half] = c
        slo[CTX_LEN:, base:base + nf] = -s
        shi[CTX_LEN:, base + nf:base + half] = s
    return jnp.asarray(cos), jnp.asarray(slo), jnp.asarray(shi)


def _pad_in_weights(w_in):
    d = w_in.shape[0]
    offs = np.cumsum([0, Q_LORA, KV_LORA, QK_ROPE, 2 * CONV_CH, ML_WIDTH, ML_WIDTH, ML_WIDTH, ML_WIDTH])
    o_cq, o_ckv, o_kr, o_conv, o_mq, o_mk, o_mv, o_mo, o_mg = [int(o) for o in offs]
    z = lambda n: jnp.zeros((d, n), w_in.dtype)
    mg = w_in[:, o_mg:o_mg + 4 * ML_HEADS]
    h = ML_HEADS
    gi = jnp.concatenate([mg[:, 0:h], mg[:, 2 * h:3 * h]], 1)
    gf = jnp.concatenate([mg[:, h:2 * h], mg[:, 3 * h:4 * h]], 1)
    cols = [
        w_in[:, o_cq:o_ckv], w_in[:, o_ckv:o_kr],
        gi, gf, z(QK_NOPE - 2 * N_GATE), w_in[:, o_kr:o_conv], z(LANES - QK_NOPE - QK_ROPE),
        w_in[:, o_conv:o_mg],
    ]
    return jnp.concatenate(cols, 1).astype(BF16)


def _pad_gate_bias(b_gates):
    h = ML_HEADS
    pad = jnp.zeros((LANES - 2 * N_GATE,), b_gates.dtype)
    return jnp.concatenate([b_gates[0:h], b_gates[2 * h:3 * h], b_gates[h:2 * h], b_gates[3 * h:4 * h], pad])[None, :]


def _pad_mla_weights(w_uq, w_ukv):
    dq = QK_NOPE + QK_ROPE
    wq = w_uq.reshape(Q_LORA, MLA_HEADS, dq)
    wq = jnp.pad(wq, ((0, 0), (0, 0), (0, HEAD_PAD - dq))).reshape(Q_LORA, MLA_HEADS * HEAD_PAD)
    wkv = w_ukv.reshape(KV_LORA, MLA_HEADS, QK_NOPE + V_HEAD)
    wk = jnp.pad(wkv[:, :, :QK_NOPE], ((0, 0), (0, 0), (0, HEAD_PAD - QK_NOPE)))
    wk = wk.reshape(KV_LORA, MLA_HEADS * HEAD_PAD)
    wv = wkv[:, :, QK_NOPE:].reshape(KV_LORA, MLA_HEADS * V_HEAD)
    return wq.astype(BF16), wk.astype(BF16), wv.astype(BF16)


def kernel(x, c, ctx, c_ctx, w_ada, b_ada, w_in, g_qn, w_uq, g_kvn, w_ukv, conv_w, conv_b, conv_ln_g, conv_ln_b, b_gates, ml_norm_g, w_out, ln1_g, ln1_b, w_mlp1, b_mlp1, w_mlp2, b_mlp2, ln2_g, ln2_b):
    b, seq, d = x.shape
    depth = w_in.shape[0]
    assert ctx.shape[1] == CTX_LEN == TM and seq % TM == 0 and d == D_MODEL and depth == DEPTH
    row = lambda a: a[None, :]

    cc = jnp.concatenate([c, c_ctx[None, :], jnp.zeros((SUBLANES - 1, d), c.dtype)], 0)
    mod_all = _ada_call(cc, w_ada, b_ada).reshape(depth, cc.shape[0], 6, d)
    cos, slo, shi = _rope_tables(seq)

    xx = (ctx, x)
    for l in range(depth):
        last = l == depth - 1
        t0 = 1 if last else 0
        mod = mod_all[l]
        wq, wk, wv = _pad_mla_weights(w_uq[l], w_ukv[l])
        q, k, v, z, mq, mkt, mv, mo, gir, gfr = _in_call(
            xx, mod, _pad_in_weights(w_in[l]), row(g_qn[l]), row(g_kvn[l]), wq, wk, wv,
            _pad_gate_bias(b_gates[l]), cos, slo, shi)
        att = _attn_call(q, k, v, t0)
        ml = _mlstm_call(mq, mkt, mv, mo, gir, gfr, row(ml_norm_g[l]))
        x1 = _post_call(xx, mod, att, z, ml, conv_w[l], row(conv_b[l]), row(conv_ln_g[l]), row(conv_ln_b[l]),
                        w_out[l].astype(BF16), row(ln1_g[l]), row(ln1_b[l]), t0)
        xx = _mlp_call(x1, mod, w_mlp1[l].astype(BF16), row(b_mlp1[l]), w_mlp2[l].astype(BF16),
                       row(b_mlp2[l]), row(ln2_g[l]), row(ln2_b[l]), with_ctx=not last)
    return xx
```

```python
import functools

import numpy as np
import jax
import jax.numpy as jnp
from jax import lax
from jax.experimental import pallas as pl
from jax.experimental.pallas import tpu as pltpu

F32 = jnp.float32
BF16 = jnp.bfloat16

LANES = 128
SUBLANES = 8
VMEM_LIMIT_BYTES = 56 * 1024 * 1024

D_MODEL = 1024
GRID_W = 64
CTX_LEN = 256
MLA_HEADS = 8
QK_NOPE = 64
QK_ROPE = 32
V_HEAD = 64
Q_LORA = 256
KV_LORA = 128
CONV_CH = 256
CONV_WIDTH = 31
ML_HEADS = 4
ML_HEAD_DIM = 64
ML_WIDTH = ML_HEADS * ML_HEAD_DIM
CHUNK = 128
D_FF = 4 * D_MODEL
ROPE_THETA = 10000.0
LN_EPS = 1e-5
RMS_EPS = 1e-6
DEPTH = 2
DEEPNORM_ALPHA = (2 * DEPTH) ** 0.25

TM = 256
HALO = 16
HEAD_PAD = LANES

C_CQ = 0
C_CKV = C_CQ + Q_LORA
C_KRG = C_CKV + KV_LORA
C_CONV = C_KRG + LANES
C_MQ = C_CONV + 2 * CONV_CH
C_MK = C_MQ + ML_WIDTH
C_MV = C_MK + ML_WIDTH
C_MO = C_MV + ML_WIDTH
IN_PAD = C_MO + ML_WIDTH
N_GATE = 2 * ML_HEADS
LOG2_E = float(np.log2(np.e))
Q_SCALE = (QK_NOPE + QK_ROPE) ** -0.5 * LOG2_E


def _wide_tile(rows):
    for tile in (768, 512, TM):
        if rows % tile == 0:
            return tile
    raise ValueError(f"no token tile divides {rows}")


def _layer_norm(v, g, b):
    mu = jnp.mean(v, -1, keepdims=True)
    d = v - mu
    var = jnp.mean(d * d, -1, keepdims=True)
    return d * lax.rsqrt(var + LN_EPS) * g + b


def _rms_norm(v, g):
    return v * lax.rsqrt(jnp.mean(v * v, -1, keepdims=True) + RMS_EPS) * g


def _sigmoid(v):
    return 1.0 / (1.0 + jnp.exp(-v))


def _log_sigmoid(v):
    return jnp.minimum(v, 0.0) - jnp.log1p(jnp.exp(-jnp.abs(v)))


def _rope(t, cos, sin_lo, sin_hi):
    return t * cos + pltpu.roll(t, LANES - 8, 1) * sin_lo + pltpu.roll(t, 8, 1) * sin_hi


def _ada_kernel(c_ref, w_ref, b_ref, o_ref):
    c = c_ref[...]
    s = (c * _sigmoid(c)).astype(BF16)
    o_ref[0] = jnp.dot(s, w_ref[0].astype(BF16), preferred_element_type=F32) + b_ref[0]


def _ada_call(cc, w_ada, b_ada):
    depth, d, n6 = w_ada.shape
    rows = cc.shape[0]
    nblk = n6 // d
    return pl.pallas_call(
        _ada_kernel,
        out_shape=jax.ShapeDtypeStruct((depth, rows, n6), F32),
        grid=(depth, nblk),
        in_specs=[
            pl.BlockSpec((rows, d), lambda l, n: (0, 0)),
            pl.BlockSpec((1, d, d), lambda l, n: (l, 0, n)),
            pl.BlockSpec((1, 1, d), lambda l, n: (l, 0, n)),
        ],
        out_specs=pl.BlockSpec((1, rows, d), lambda l, n: (l, 0, n)),
        compiler_params=pltpu.CompilerParams(dimension_semantics=("parallel", "parallel")),
        name="ada_mod",
    )(cc, w_ada, b_ada.reshape(depth, 1, n6))


def _mod_rows(mod_ref, modc_ref, idx, rows, ctx_rows):
    m = mod_ref[0, idx:idx + 1, :]
    if ctx_rows == 0:
        return m
    is_ctx = jnp.logical_and(lax.broadcasted_iota(jnp.int32, (rows, 1), 0) < ctx_rows, pl.program_id(1) == 0)
    return jnp.where(is_ctx, modc_ref[0, idx:idx + 1, :], m)


def _in_kernel(*refs, n_x):
    x_refs = refs[:n_x]
    (mod_ref, modc_ref, w_ref, gq_ref, gkv_ref, wuq_ref, wuk_ref, wuv_ref, bg_ref, cos_ref, slo_ref, shi_ref,
     q_ref, k_ref, v_ref, z_ref, mq_ref, mkt_ref, mv_ref, mo_ref, gir_ref, gfr_ref) = refs[n_x:]
    if n_x == 1:
        x = x_refs[0][0]
        tm = x.shape[0]
        sh1 = _mod_rows(mod_ref, modc_ref, 0, tm, CTX_LEN)
        sc1 = _mod_rows(mod_ref, modc_ref, 1, tm, CTX_LEN)
        xm = (x * (1.0 + sc1) + sh1).astype(BF16)
    else:
        ctx_ref, first_ref = x_refs[0], x_refs[1]
        is_ctx = pl.program_id(1) == 0
        pieces = []
        for j, piece_ref in enumerate(x_refs[1:]):
            sh1, sc1 = mod_ref[0, 0:1, :], mod_ref[0, 1:2, :]
            piece = piece_ref[0]
            if j == 0:
                sh1 = jnp.where(is_ctx, modc_ref[0, 0:1, :], sh1)
                sc1 = jnp.where(is_ctx, modc_ref[0, 1:2, :], sc1)
                piece = jnp.where(is_ctx, ctx_ref[0], first_ref[0])
            pieces.append((piece * (1.0 + sc1) + sh1).astype(BF16))
        xm = jnp.concatenate(pieces, axis=0)
        tm = xm.shape[0]

    def proj(lo, hi):
        return jnp.dot(xm, w_ref[:, lo:hi], preferred_element_type=F32)

    cos = cos_ref[...]
    slo = slo_ref[...]
    shi = shi_ref[...]

    cq = proj(C_CQ, C_CQ + Q_LORA)
    ckv_krg = proj(C_CKV, C_KRG + LANES)
    a = proj(C_CONV, C_CONV + CONV_CH)
    cqn = _rms_norm(cq, gq_ref[...]).astype(BF16)
    gt = proj(C_CONV + CONV_CH, C_CONV + 2 * CONV_CH)
    ckvn = _rms_norm(ckv_krg[:, :KV_LORA], gkv_ref[...]).astype(BF16)
    krg = ckv_krg[:, KV_LORA:]

    qf = jnp.dot(cqn, wuq_ref[...], preferred_element_type=F32) * Q_SCALE
    z_ref[0] = a * _sigmoid(gt)
    mq = proj(C_MQ, C_MQ + ML_WIDTH)
    kf = jnp.dot(ckvn, wuk_ref[...], preferred_element_type=F32)
    for h in range(MLA_HEADS):
        sl = slice(h * HEAD_PAD, (h + 1) * HEAD_PAD)
        q_ref[0, :, sl] = _rope(qf[:, sl], cos, slo, shi).astype(BF16)
    mq_ref[0] = mq.astype(BF16)

    mk = proj(C_MK, C_MK + ML_WIDTH)
    vf = jnp.dot(ckvn, wuv_ref[...], preferred_element_type=F32)
    lane = lax.broadcasted_iota(jnp.int32, (1, LANES), 1)
    is_rope_lane = jnp.logical_and(lane >= QK_NOPE, lane < QK_NOPE + QK_ROPE)
    kr = jnp.where(is_rope_lane, _rope(krg, cos, slo, shi), 0.0)
    for h in range(MLA_HEADS):
        sl = slice(h * HEAD_PAD, (h + 1) * HEAD_PAD)
        k_ref[0, :, sl] = (kf[:, sl] + kr).astype(BF16)
    mv = proj(C_MV, C_MV + ML_WIDTH)
    mo = proj(C_MO, C_MO + ML_WIDTH)
    v_ref[0] = vf.astype(BF16)

    mk_t = (mk * (ML_HEAD_DIM ** -0.5)).T
    g_t = (krg + bg_ref[...]).T
    gi_t = g_t[:N_GATE]
    gf_t = _log_sigmoid(g_t[N_GATE:2 * N_GATE])
    for cc in range(tm // CHUNK):
        cs = slice(cc * CHUNK, (cc + 1) * CHUNK)
        mkt_ref[0, cc] = mk_t[:, cs].astype(BF16)
        gir_ref[0, cc] = gi_t[:, cs]
        gfr_ref[0, cc] = gf_t[:, cs]
    mv_ref[0] = mv.astype(BF16)
    mo_ref[0] = mo


def _in_call(stream, mod, w_in_p, gq, gkv, wuq, wuk, wuv, bg, cos, slo, shi):
    split = isinstance(stream, tuple)
    if split:
        ctx, x = stream
        b, seq, d = x.shape
        ltot = CTX_LEN + seq
    else:
        b, ltot, d = stream.shape
    tm = _wide_tile(ltot)
    nt = ltot // tm
    nc = ltot // CHUNK
    ctx_row = mod.shape[0] - SUBLANES

    def tok(width):
        return pl.BlockSpec((1, tm, width), lambda i, t: (i, t, 0))

    def const2(shape):
        return pl.BlockSpec(shape, lambda i, t: (0, 0), pipeline_mode=pl.Buffered(1))

    tab = pl.BlockSpec((tm, LANES), lambda i, t: (t, 0))
    grow = pl.BlockSpec((1, tm // CHUNK, N_GATE, CHUNK), lambda i, t: (i, t, 0, 0))
    outs = [
        (jax.ShapeDtypeStruct((b, ltot, MLA_HEADS * HEAD_PAD), BF16), tok(MLA_HEADS * HEAD_PAD)),
        (jax.ShapeDtypeStruct((b, ltot, MLA_HEADS * HEAD_PAD), BF16), tok(MLA_HEADS * HEAD_PAD)),
        (jax.ShapeDtypeStruct((b, ltot, MLA_HEADS * V_HEAD), BF16), tok(MLA_HEADS * V_HEAD)),
        (jax.ShapeDtypeStruct((b, ltot, CONV_CH), F32), tok(CONV_CH)),
        (jax.ShapeDtypeStruct((b, ltot, ML_WIDTH), BF16), tok(ML_WIDTH)),
        (jax.ShapeDtypeStruct((b, nc, ML_WIDTH, CHUNK), BF16),
         pl.BlockSpec((1, tm // CHUNK, ML_WIDTH, CHUNK), lambda i, t: (i, t, 0, 0))),
        (jax.ShapeDtypeStruct((b, ltot, ML_WIDTH), BF16), tok(ML_WIDTH)),
        (jax.ShapeDtypeStruct((b, ltot, ML_WIDTH), F32), tok(ML_WIDTH)),
        (jax.ShapeDtypeStruct((b, nc, N_GATE, CHUNK), F32), grow),
        (jax.ShapeDtypeStruct((b, nc, N_GATE, CHUNK), F32), grow),
    ]
    if split:
        ppt = tm // TM
        x_args = [ctx] + [x] * ppt
        x_specs = [pl.BlockSpec((1, TM, d), lambda i, t: (i, 0, 0))] + [
            pl.BlockSpec((1, TM, d), lambda i, t, j=j: (i, jnp.maximum(t * ppt + j - 1, 0), 0)) for j in range(ppt)]
    else:
        x_args, x_specs = [stream], [tok(d)]
    return pl.pallas_call(
        functools.partial(_in_kernel, n_x=len(x_args)),
        out_shape=[o[0] for o in outs],
        grid=(b, nt),
        in_specs=x_specs + [
            pl.BlockSpec((1, 6, d), lambda i, t: (i, 0, 0)),
            pl.BlockSpec((1, 6, d), lambda i, t: (ctx_row, 0, 0)),
            const2(w_in_p.shape),
            const2(gq.shape), const2(gkv.shape),
            const2(wuq.shape), const2(wuk.shape), const2(wuv.shape),
            const2(bg.shape),
            tab, tab, tab,
        ],
        out_specs=[o[1] for o in outs],
        compiler_params=pltpu.CompilerParams(dimension_semantics=("parallel", "parallel"),
                                             vmem_limit_bytes=VMEM_LIMIT_BYTES),
        name="in_proj",
    )(*x_args, mod, mod, w_in_p, gq, gkv, wuq, wuk, wuv, bg, cos, slo, shi)


ATT_PAIRS = 4


def _attn_kernel(q_ref, k_ref, v_ref, o_ref, vaug_ref, *, t0, ltot):
    pair_w = 2 * V_HEAD
    nh = 2 * ATT_PAIRS

    @pl.when(pl.program_id(2) == 0)
    def _():
        ones = jnp.ones((ltot, pair_w), BF16)
        for pp in range(ATT_PAIRS):
            vaug_ref[pp] = jnp.concatenate([v_ref[0, :, pp * pair_w:(pp + 1) * pair_w], ones], axis=1)

    def run(nk):
        def scores(h):
            sl = slice(h * HEAD_PAD, (h + 1) * HEAD_PAD)
            return lax.dot_general(q_ref[0, :, sl], k_ref[0, :nk, sl], (((1,), (1,)), ((), ())),
                                   preferred_element_type=F32)

        def probs(s):
            return jnp.exp2((s - jnp.max(s, -1, keepdims=True)).astype(BF16))

        def weighted(h, p):
            o = jnp.dot(p, vaug_ref[h // 2, :nk, :], preferred_element_type=F32)
            return o[:, :pair_w] / o[:, pair_w:]

        s_next = scores(0)
        p_prev = None
        outs = []
        for h in range(nh):
            s_cur = s_next
            if h + 1 < nh:
                s_next = scores(h + 1)
            p_cur = probs(s_cur)
            if p_prev is not None:
                outs.append(weighted(h - 1, p_prev))
            p_prev = p_cur
        outs.append(weighted(nh - 1, p_prev))
        lane = lax.broadcasted_iota(jnp.int32, outs[0].shape, 1)
        for pp in range(ATT_PAIRS):
            o_ref[0, :, pp * pair_w:(pp + 1) * pair_w] = jnp.where(
                lane < V_HEAD, outs[2 * pp], outs[2 * pp + 1]).astype(o_ref.dtype)

    if t0 == 0:
        t = pl.program_id(2)

        @pl.when(t == 0)
        def _():
            run(CTX_LEN)

        @pl.when(t > 0)
        def _():
            run(ltot)
    else:
        run(ltot)


def _attn_call(q, k, v, t0):
    b, ltot, _ = q.shape
    nt = ltot // TM - t0
    ngroup = MLA_HEADS // (2 * ATT_PAIRS)
    qk_w = 2 * ATT_PAIRS * HEAD_PAD
    v_w = 2 * ATT_PAIRS * V_HEAD
    return pl.pallas_call(
        functools.partial(_attn_kernel, t0=t0, ltot=ltot),
        out_shape=jax.ShapeDtypeStruct((b, nt * TM, MLA_HEADS * V_HEAD), BF16),
        grid=(b, ngroup, nt),
        in_specs=[
            pl.BlockSpec((1, TM, qk_w), lambda i, j, t: (i, t + t0, j)),
            pl.BlockSpec((1, ltot, qk_w), lambda i, j, t: (i, 0, j)),
            pl.BlockSpec((1, ltot, v_w), lambda i, j, t: (i, 0, j)),
        ],
        out_specs=pl.BlockSpec((1, TM, v_w), lambda i, j, t: (i, t, j)),
        scratch_shapes=[pltpu.VMEM((ATT_PAIRS, ltot, 4 * V_HEAD), BF16)],
        compiler_params=pltpu.CompilerParams(dimension_semantics=("parallel", "parallel", "arbitrary"),
                                             vmem_limit_bytes=VMEM_LIMIT_BYTES),
        name="mla_attn",
    )(q, k, v)


def _mlstm_kernel(q_ref, kt_ref, v_ref, mo_ref, gi_ref, gf_ref, g_ref,
                  o_ref, hf_ref, hb_ref, sn_ref, a_ref, r_ref, e_ref, tot_ref, gmax_ref, *, nchunk, nctx):
    t_ = CHUNK
    npair = ML_HEADS // 2
    row = lax.broadcasted_iota(jnp.int32, (t_, t_), 0)
    col = lax.broadcasted_iota(jnp.int32, (t_, t_), 1)
    causal_masks = (col <= row, col >= row)
    blockdiag = (row < ML_HEAD_DIM) == (col < ML_HEAD_DIM)
    half0 = lax.broadcasted_iota(jnp.int32, (1, LANES), 1) < ML_HEAD_DIM
    rowhalf0 = lax.broadcasted_iota(jnp.int32, (LANES, 1), 0) < ML_HEAD_DIM
    lane8 = lax.broadcasted_iota(jnp.int32, (1, t_), 1)
    n_parts = 3
    zpad = jnp.zeros((t_ - 2 * n_parts * N_GATE, t_), F32)
    ones_b = jnp.ones((t_, LANES), BF16)
    zeros_b = jnp.zeros((t_, LANES), BF16)
    half_ones = tuple(jnp.broadcast_to(jnp.where(half0, on, 1.0 - on), (t_, LANES)).astype(BF16) for on in (1.0, 0.0))

    r2 = lax.broadcasted_iota(jnp.int32, (t_, 3 * LANES), 0)
    l2 = lax.broadcasted_iota(jnp.int32, (t_, 3 * LANES), 1)
    blk, gate = r2 >> 3, r2 & 7
    lblk, lhalf = l2 >> 7, (l2 >> 6) & 1
    is_r_part = blk < n_parts
    is_b_part = jnp.logical_and(blk >= n_parts, blk < 2 * n_parts)
    consts = {}
    for d in range(2):
        for p in range(npair):
            c0 = d * ML_HEADS + 2 * p
            want_gate = jnp.where(lblk == 2, c0 + lhalf, c0 + lblk)
            want_part = jnp.where(lblk == 2, is_b_part.astype(jnp.int32), is_r_part.astype(jnp.int32)) == 1
            consts[d, p] = jnp.where(jnp.logical_and(want_part, gate == want_gate), 1.0, 0.0).astype(BF16)

    def split3(x):
        hi = x.astype(BF16).astype(F32)
        rem = x - hi
        mid = rem.astype(BF16).astype(F32)
        return [hi, mid, rem - mid]

    def lane_scan(x, op, fill, reverse):
        k = 1
        while k < t_:
            if reverse:
                shifted, valid = pltpu.roll(x, t_ - k, 1), lane8 < t_ - k
            else:
                shifted, valid = pltpu.roll(x, k, 1), lane8 >= k
            x = op(x, jnp.where(valid, shifted, fill))
            k *= 2
        return x

    nrow = nchunk * N_GATE
    is_fwd_row = (lax.broadcasted_iota(jnp.int32, (nrow, t_), 0) & (N_GATE - 1)) < ML_HEADS
    lf = gf_ref[0].reshape(nrow, t_) * LOG2_E
    lf_parts = jnp.concatenate([part.astype(BF16) for part in split3(lf)], axis=1)
    tri_up = jnp.where(row <= col, 1.0, 0.0).astype(BF16)
    tri_dn = jnp.where(row >= col, 1.0, 0.0).astype(BF16)
    b = jnp.where(is_fwd_row,
                  jnp.dot(lf_parts, jnp.concatenate([tri_up] * n_parts, axis=0), preferred_element_type=F32),
                  jnp.dot(lf_parts, jnp.concatenate([tri_dn] * n_parts, axis=0), preferred_element_type=F32))
    tot = jnp.sum(lf, axis=1, keepdims=True)
    r = gi_ref[0].reshape(nrow, t_) * LOG2_E - b
    rmax = jnp.where(is_fwd_row, lane_scan(r, jnp.maximum, -jnp.inf, False),
                     lane_scan(r, jnp.maximum, -jnp.inf, True))
    rlast = jnp.max(r, axis=1, keepdims=True)
    col_parts = split3(rmax) + split3(b)
    for c in range(nchunk):
        rows_c = slice(c * N_GATE, (c + 1) * N_GATE)
        packed = jnp.concatenate([part[rows_c] for part in col_parts] + [zpad], axis=0)
        a_ref[c] = packed.T.astype(BF16)
    r_ref[...] = r
    e_ref[...] = jnp.exp2(r - rlast)
    tot_ref[...] = jnp.broadcast_to(tot, (nrow, t_))
    gmax_ref[...] = jnp.broadcast_to(tot + rlast, (nrow, t_))

    def prep(d, c, m_prev):
        rows_c = pl.ds(pl.multiple_of(c * N_GATE, N_GATE), N_GATE)
        tot_c = tot_ref[rows_c, :]
        gmax_c = gmax_ref[rows_c, :]
        m_new = jnp.maximum(tot_c + m_prev, gmax_c)
        decay = jnp.exp2(tot_c + m_prev - m_new)
        gamma = jnp.exp2(gmax_c - m_new)
        return a_ref[c], r_ref[rows_c, :], e_ref[rows_c, :], decay, gamma, m_new

    def stage_mxu(d, p, c, pre):
        a, _, e_row, _, _, _ = pre
        c0 = d * ML_HEADS + 2 * p
        c1 = c0 + 1
        sl = slice(p * LANES, (p + 1) * LANES)
        q = q_ref[0, c, :, sl]
        kt = kt_ref[0, c, sl, :]
        v = v_ref[0, c, :, sl]
        z = jnp.dot(a, consts[d, p], preferred_element_type=F32)
        kt_heads = jnp.concatenate([jnp.where(rowhalf0, kt, zeros_b), jnp.where(rowhalf0, zeros_b, kt)], axis=1)
        qk = jnp.dot(q, kt_heads, preferred_element_type=F32)
        wkt = kt * jnp.where(rowhalf0, e_row[c0:c0 + 1, :], e_row[c1:c1 + 1, :]).astype(BF16)
        upd = jnp.dot(wkt, jnp.concatenate([v, ones_b], axis=1), preferred_element_type=F32)
        sn = sn_ref[2 * p + d]
        qs = jnp.dot(q, sn.astype(BF16), preferred_element_type=F32)
        return z, qk, upd, sn, qs, v

    def stage_intra(d, p, r_rows, z, qk, v):
        c0 = d * ML_HEADS + 2 * p
        nd = jnp.zeros((t_, 2 * LANES), F32)
        for hh in range(2):
            hs = slice(hh * LANES, (hh + 1) * LANES)
            wts = jnp.exp2(jnp.where(causal_masks[d], r_rows[c0 + hh:c0 + hh + 1, :] - z[:, hs], -jnp.inf))
            v_h = jnp.where(half0, v, zeros_b) if hh == 0 else jnp.where(half0, zeros_b, v)
            nd = nd + jnp.dot((qk[:, hs] * wts).astype(BF16), jnp.concatenate([v_h, half_ones[hh]], axis=1),
                              preferred_element_type=F32)
        return nd

    def stage_out(d, p, c, pre, m_prev, h_ref, z, upd, sn, qs, nd):
        _, _, _, decay, gamma, _ = pre
        c0 = d * ML_HEADS + 2 * p
        c1 = c0 + 1
        sl = slice(p * LANES, (p + 1) * LANES)
        r_bc = jnp.where(half0, z[:, :LANES], z[:, LANES:2 * LANES])
        b_bc = z[:, 2 * LANES:]
        m_bc = jnp.where(half0, m_prev[c0:c0 + 1, :], m_prev[c1:c1 + 1, :])
        p_bc = jnp.maximum(m_bc, r_bc)
        alpha = jnp.exp2(r_bc - p_bc)
        beta = jnp.exp2(m_bc - p_bc)
        num = alpha * nd[:, :LANES] + beta * qs[:, :LANES]
        den = alpha * nd[:, LANES:] + beta * qs[:, LANES:]
        h_ref[c, :, sl] = num / jnp.maximum(jnp.abs(den), jnp.exp2(-(b_bc + p_bc)))

        dec_rows = jnp.where(rowhalf0, decay[c0:c0 + 1, :], decay[c1:c1 + 1, :])
        gam_rows = jnp.where(blockdiag, jnp.where(rowhalf0, gamma[c0:c0 + 1, :], gamma[c1:c1 + 1, :]), 0.0)
        sn_ref[2 * p + d] = jnp.concatenate([dec_rows * sn[:, :LANES] + gam_rows * upd[:, :LANES],
                                             dec_rows * sn[:, LANES:] + gam_rows * upd[:, LANES:]], axis=1)

    sn_ref[...] = jnp.zeros(sn_ref.shape, F32)

    def body(i, carry):
        cb = jnp.where(i < nctx, nctx - 1 - i, nchunk - 1 + nctx - i)
        streams = [(d, p, c, h_ref) for d, c, h_ref in ((0, i, hf_ref), (1, cb, hb_ref)) for p in range(npair)]
        pres = {0: prep(0, i, carry[0]), 1: prep(1, cb, carry[1])}
        first = [stage_mxu(d, p, c, pres[d]) for d, p, c, _ in streams]
        intra = [stage_intra(d, p, pres[d][1], z, qk, v)
                 for (d, p, _, _), (z, qk, _, _, _, v) in zip(streams, first)]
        for (d, p, c, h_ref), (z, _, upd, sn, qs, _), nd in zip(streams, first, intra):
            stage_out(d, p, c, pres[d], carry[d], h_ref, z, upd, sn, qs, nd)
        return pres[0][-1], pres[1][-1]

    m_init = jnp.zeros((N_GATE, t_), F32)
    lax.fori_loop(0, nchunk, body, (m_init, m_init), unroll=2)

    g = g_ref[...]
    head_shift = ML_HEAD_DIM.bit_length() - 1
    head_of_row = lax.broadcasted_iota(jnp.int32, (ML_WIDTH, ML_WIDTH), 0) >> head_shift
    head_of_col = lax.broadcasted_iota(jnp.int32, (ML_WIDTH, ML_WIDTH), 1) >> head_shift
    avg = jnp.where(head_of_row == head_of_col, 1.0 / ML_HEAD_DIM, 0.0).astype(BF16)
    avg2 = jnp.concatenate([avg, avg], axis=0)

    def head_mean(x):
        hi = x.astype(BF16)
        lo = (x - hi.astype(F32)).astype(BF16)
        return jnp.dot(jnp.concatenate([hi, lo], axis=1), avg2, preferred_element_type=F32)

    def fin(c, _):
        hh = (hf_ref[c] + hb_ref[c]) * _sigmoid(mo_ref[0, c])
        dl = hh - head_mean(hh)
        var = head_mean(dl * dl)
        o_ref[0, c] = (dl * lax.rsqrt(var + LN_EPS) * g).astype(o_ref.dtype)
        return 0

    lax.fori_loop(0, nchunk, fin, 0, unroll=6)


def _mlstm_call(mq, mkt, mv, mo, gir, gfr, g):
    b, ltot, w = mq.shape
    nc = ltot // CHUNK
    r4 = lambda a: a.reshape(b, nc, CHUNK, a.shape[-1])
    blk = lambda rows, width: pl.BlockSpec((1, nc, rows, width), lambda i: (i, 0, 0, 0))
    out = pl.pallas_call(
        functools.partial(_mlstm_kernel, nchunk=nc, nctx=CTX_LEN // CHUNK),
        out_shape=jax.ShapeDtypeStruct((b, nc, CHUNK, w), BF16),
        grid=(b,),
        in_specs=[blk(CHUNK, w), blk(w, CHUNK), blk(CHUNK, w), blk(CHUNK, w),
                  blk(N_GATE, CHUNK), blk(N_GATE, CHUNK),
                  pl.BlockSpec((1, w), lambda i: (0, 0))],
        out_specs=blk(CHUNK, w),
        scratch_shapes=[pltpu.VMEM((nc, CHUNK, w), F32), pltpu.VMEM((nc, CHUNK, w), F32),
                        pltpu.VMEM((2 * (ML_HEADS // 2), LANES, 2 * LANES), F32),
                        pltpu.VMEM((nc, CHUNK, LANES), BF16), pltpu.VMEM((nc * N_GATE, CHUNK), F32),
                        pltpu.VMEM((nc * N_GATE, CHUNK), F32), pltpu.VMEM((nc * N_GATE, CHUNK), F32),
                        pltpu.VMEM((nc * N_GATE, CHUNK), F32)],
        compiler_params=pltpu.CompilerParams(dimension_semantics=("parallel",),
                                             vmem_limit_bytes=VMEM_LIMIT_BYTES),
        name="mlstm",
    )(r4(mq), mkt, r4(mv), r4(mo), gir, gfr, g)
    return out.reshape(b, ltot, w)


def _post_kernel(*refs, t0, nt_all, split):
    if split:
        ctx_ref, x_ref = refs[:2]
        resid = jnp.where(pl.program_id(1) == 0, ctx_ref[0], x_ref[0])
    else:
        resid = refs[0][0]
    (mod_ref, att_ref, z_ref, zp_ref, zn_ref, ml_ref, cw_ref, cb_ref, cg_ref, cbb_ref,
     wo_ref, g1_ref, b1_ref, o_ref, zbuf_ref, zsh_ref) = refs[2 if split else 1:]
    t = pl.program_id(1) + t0
    n_att = MLA_HEADS * V_HEAD
    y = jnp.dot(att_ref[0], wo_ref[0:n_att, :], preferred_element_type=F32)
    y = y + jnp.dot(ml_ref[0], wo_ref[n_att + CONV_CH:, :], preferred_element_type=F32)
    left_ok = t >= 2
    right_ok = jnp.logical_and(t >= 1, t <= nt_all - 2)
    zbuf_ref[0:HALO, :] = jnp.where(left_ok, zp_ref[0], 0.0)
    zbuf_ref[HALO:HALO + TM, :] = z_ref[0]
    zbuf_ref[HALO + TM:, :] = jnp.where(right_ok, zn_ref[0], 0.0)
    off = HALO - CONV_WIDTH // 2
    span = TM + ((off + CONV_WIDTH - 1) // SUBLANES) * SUBLANES
    for s in range(SUBLANES):
        zsh_ref[s] = zbuf_ref[s:s + span, :]
    acc = jnp.zeros((TM, CONV_CH), F32)
    for j in range(CONV_WIDTH):
        s, a = (off + j) % SUBLANES, ((off + j) // SUBLANES) * SUBLANES
        acc = acc + cw_ref[j:j + 1, :] * zsh_ref[s, a:a + TM, :]
    cv = _layer_norm(acc + cb_ref[...], cg_ref[...], cbb_ref[...])
    cv = cv * _sigmoid(cv)

    y = y + jnp.dot(cv.astype(BF16), wo_ref[n_att:n_att + CONV_CH, :], preferred_element_type=F32)
    g1 = mod_ref[0, 2:3, :]
    o_ref[0] = _layer_norm(DEEPNORM_ALPHA * resid + g1 * y, g1_ref[...], b1_ref[...])


def _post_call(stream, mod, att, z, ml, cw, cb, cg, cbb, wo, g1, b1, t0):
    split = isinstance(stream, tuple)
    if split:
        ctx, x = stream
        b, seq, d = x.shape
        ltot = CTX_LEN + seq
    else:
        b, ltot, d = stream.shape
    nt_all = ltot // TM
    nt = nt_all - t0
    ctx_row = mod.shape[0] - SUBLANES
    hpt = TM // HALO
    nhalo = ltot // HALO

    def tok(width, off):
        return pl.BlockSpec((1, TM, width), lambda i, t: (i, t + off, 0))

    def const2(shape):
        return pl.BlockSpec(shape, lambda i, t: (0, 0))

    if split:
        assert t0 == 0
        x_args = [ctx, x]
        x_specs = [pl.BlockSpec((1, TM, d), lambda i, t: (i, 0, 0)),
                   pl.BlockSpec((1, TM, d), lambda i, t: (i, jnp.maximum(t - 1, 0), 0))]
    else:
        x_args, x_specs = [stream], [tok(d, t0)]
    return pl.pallas_call(
        functools.partial(_post_kernel, t0=t0, nt_all=nt_all, split=split),
        out_shape=jax.ShapeDtypeStruct((b, nt * TM, d), F32),
        grid=(b, nt),
        in_specs=x_specs + [
            pl.BlockSpec((1, 6, d), lambda i, t: (jnp.where(t + t0 == 0, ctx_row, i), 0, 0)),
            tok(att.shape[-1], 0),
            tok(CONV_CH, t0),
            pl.BlockSpec((1, HALO, CONV_CH), lambda i, t: (i, jnp.maximum((t + t0) * hpt - 1, 0), 0)),
            pl.BlockSpec((1, HALO, CONV_CH), lambda i, t: (i, jnp.minimum((t + t0 + 1) * hpt, nhalo - 1), 0)),
            tok(ML_WIDTH, t0),
            const2(cw.shape), const2(cb.shape), const2(cg.shape), const2(cbb.shape),
            const2(wo.shape), const2(g1.shape), const2(b1.shape),
        ],
        out_specs=tok(d, 0),
        scratch_shapes=[pltpu.VMEM((TM + 2 * HALO, CONV_CH), F32),
                        pltpu.VMEM((SUBLANES, TM + 2 * HALO - SUBLANES, CONV_CH), F32)],
        compiler_params=pltpu.CompilerParams(dimension_semantics=("parallel", "parallel"),
                                             vmem_limit_bytes=VMEM_LIMIT_BYTES),
        name="out_proj",
    )(*x_args, mod, att, z, z, z, ml, cw, cb, cg, cbb, wo, g1, b1)


FF_CHUNK = 1024


def _mlp_kernel(x_ref, mod_ref, modc_ref, w1_ref, b1_ref, w2_ref, b2_ref, g_ref, b_ref, o_ref, *, ctx_rows):
    x = x_ref[0]
    sh2 = _mod_rows(mod_ref, modc_ref, 3, x.shape[0], ctx_rows)
    sc2 = _mod_rows(mod_ref, modc_ref, 4, x.shape[0], ctx_rows)
    g2 = _mod_rows(mod_ref, modc_ref, 5, x.shape[0], ctx_rows)
    u = (x * (1.0 + sc2) + sh2).astype(BF16)
    def hidden(c):
        sl = slice(c * FF_CHUNK, (c + 1) * FF_CHUNK)
        return jnp.dot(u, w1_ref[:, sl], preferred_element_type=F32) + b1_ref[:, sl]

    acc = jnp.zeros(x.shape, F32)
    n_chunk = D_FF // FF_CHUNK
    pre = hidden(0)
    for c in range(n_chunk):
        cur = pre
        if c + 1 < n_chunk:
            pre = hidden(c + 1)
        h = jnp.maximum(cur, 0.0)
        acc = acc + jnp.dot((h * h).astype(BF16), w2_ref[c * FF_CHUNK:(c + 1) * FF_CHUNK, :],
                            preferred_element_type=F32)
    o_ref[0] = _layer_norm(DEEPNORM_ALPHA * x + g2 * (acc + b2_ref[...]), g_ref[...], b_ref[...])


def _mlp_call(x1, mod, w1, b1, w2, b2, g, bb, with_ctx):
    b, ln, d = x1.shape
    tm = _wide_tile(ln)
    nt = ln // tm
    ctx_row = mod.shape[0] - SUBLANES

    def const2(shape):
        return pl.BlockSpec(shape, lambda i, t: (0, 0), pipeline_mode=pl.Buffered(1))

    return pl.pallas_call(
        functools.partial(_mlp_kernel, ctx_rows=CTX_LEN if with_ctx else 0),
        out_shape=jax.ShapeDtypeStruct((b, ln, d), F32),
        grid=(b, nt),
        in_specs=[
            pl.BlockSpec((1, tm, d), lambda i, t: (i, t, 0)),
            pl.BlockSpec((1, 6, d), lambda i, t: (i, 0, 0)),
            pl.BlockSpec((1, 6, d), lambda i, t: (ctx_row, 0, 0)),
            const2(w1.shape), const2(b1.shape), const2(w2.shape), const2(b2.shape),
            const2(g.shape), const2(bb.shape),
        ],
        out_specs=pl.BlockSpec((1, tm, d), lambda i, t: (i, t, 0)),
        compiler_params=pltpu.CompilerParams(dimension_semantics=("parallel", "parallel"),
                                             vmem_limit_bytes=VMEM_LIMIT_BYTES),
        name="mlp",
    )(x1, mod, mod, w1, b1, w2, b2, g, bb)


def _rope_tables(seq):
    half = QK_ROPE // 2
    nf = half // 2
    pos = np.arange(seq)
    inv_freq = ROPE_THETA ** (-np.arange(nf, dtype=np.float32) / nf)
    cos = np.ones((CTX_LEN + seq, LANES), np.float32)
    slo = np.zeros((CTX_LEN + seq, LANES), np.float32)
    shi = np.zeros((CTX_LEN + seq, LANES), np.float32)
    for part, p in enumerate((pos // GRID_W, pos % GRID_W)):
        ang = p.astype(np.float32)[:, None] * inv_freq[None, :]
        c, s = np.cos(ang), np.sin(ang)
        base = QK_NOPE + part * half
        cos[CTX_LEN:, base:base + nf] = c
        cos[CTX_LEN:, base + nf:base + half] = c
        slo[CTX_LEN:, base:base + nf] = -s
        shi[CTX_LEN:, base + nf:base + half] = s
    return jnp.asarray(cos), jnp.asarray(slo), jnp.asarray(shi)


def _pad_in_weights(w_in):
    d = w_in.shape[0]
    offs = np.cumsum([0, Q_LORA, KV_LORA, QK_ROPE, 2 * CONV_CH, ML_WIDTH, ML_WIDTH, ML_WIDTH, ML_WIDTH])
    o_cq, o_ckv, o_kr, o_conv, o_mq, o_mk, o_mv, o_mo, o_mg = [int(o) for o in offs]
    z = lambda n: jnp.zeros((d, n), w_in.dtype)
    mg = w_in[:, o_mg:o_mg + 4 * ML_HEADS]
    h = ML_HEADS
    gi = jnp.concatenate([mg[:, 0:h], mg[:, 2 * h:3 * h]], 1)
    gf = jnp.concatenate([mg[:, h:2 * h], mg[:, 3 * h:4 * h]], 1)
    cols = [
        w_in[:, o_cq:o_ckv], w_in[:, o_ckv:o_kr],
        gi, gf, z(QK_NOPE - 2 * N_GATE), w_in[:, o_kr:o_conv], z(LANES - QK_NOPE - QK_ROPE),
        w_in[:, o_conv:o_mg],
    ]
    return jnp.concatenate(cols, 1).astype(BF16)


def _pad_gate_bias(b_gates):
    h = ML_HEADS
    pad = jnp.zeros((LANES - 2 * N_GATE,), b_gates.dtype)
    return jnp.concatenate([b_gates[0:h], b_gates[2 * h:3 * h], b_gates[h:2 * h], b_gates[3 * h:4 * h], pad])[None, :]


def _pad_mla_weights(w_uq, w_ukv):
    dq = QK_NOPE + QK_ROPE
    wq = w_uq.reshape(Q_LORA, MLA_HEADS, dq)
    wq = jnp.pad(wq, ((0, 0), (0, 0), (0, HEAD_PAD - dq))).reshape(Q_LORA, MLA_HEADS * HEAD_PAD)
    wkv = w_ukv.reshape(KV_LORA, MLA_HEADS, QK_NOPE + V_HEAD)
    wk = jnp.pad(wkv[:, :, :QK_NOPE], ((0, 0), (0, 0), (0, HEAD_PAD - QK_NOPE)))
    wk = wk.reshape(KV_LORA, MLA_HEADS * HEAD_PAD)
    wv = wkv[:, :, QK_NOPE:].reshape(KV_LORA, MLA_HEADS * V_HEAD)
    return wq.astype(BF16), wk.astype(BF16), wv.astype(BF16)


def kernel(x, c, ctx, c_ctx, w_ada, b_ada, w_in, g_qn, w_uq, g_kvn, w_ukv, conv_w, conv_b, conv_ln_g, conv_ln_b, b_gates, ml_norm_g, w_out, ln1_g, ln1_b, w_mlp1, b_mlp1, w_mlp2, b_mlp2, ln2_g, ln2_b):
    b, seq, d = x.shape
    depth = w_in.shape[0]
    assert ctx.shape[1] == CTX_LEN == TM and seq % TM == 0 and d == D_MODEL and depth == DEPTH
    row = lambda a: a[None, :]

    cc = jnp.concatenate([c, c_ctx[None, :], jnp.zeros((SUBLANES - 1, d), c.dtype)], 0)
    mod_all = _ada_call(cc, w_ada, b_ada).reshape(depth, cc.shape[0], 6, d)
    cos, slo, shi = _rope_tables(seq)

    xx = (ctx, x)
    for l in range(depth):
        last = l == depth - 1
        t0 = 1 if last else 0
        mod = mod_all[l]
        wq, wk, wv = _pad_mla_weights(w_uq[l], w_ukv[l])
        q, k, v, z, mq, mkt, mv, mo, gir, gfr = _in_call(
            xx, mod, _pad_in_weights(w_in[l]), row(g_qn[l]), row(g_kvn[l]), wq, wk, wv,
            _pad_gate_bias(b_gates[l]), cos, slo, shi)
        att = _attn_call(q, k, v, t0)
        ml = _mlstm_call(mq, mkt, mv, mo, gir, gfr, row(ml_norm_g[l]))
        x1 = _post_call(xx, mod, att, z, ml, conv_w[l], row(conv_b[l]), row(conv_ln_g[l]), row(conv_ln_b[l]),
                        w_out[l].astype(BF16), row(ln1_g[l]), row(ln1_b[l]), t0)
        xx = _mlp_call(x1, mod, w_mlp1[l].astype(BF16), row(b_mlp1[l]), w_mlp2[l].astype(BF16),
                       row(b_mlp2[l]), row(ln2_g[l]), row(ln2_b[l]), with_ctx=not last)
    return xx
```

```python
import functools

import numpy as np
import jax
import jax.numpy as jnp
from jax import lax
from jax.experimental import pallas as pl
from jax.experimental.pallas import tpu as pltpu

F32 = jnp.float32
BF16 = jnp.bfloat16

LANES = 128
SUBLANES = 8
VMEM_LIMIT_BYTES = 56 * 1024 * 1024

D_MODEL = 1024
GRID_W = 64
CTX_LEN = 256
MLA_HEADS = 8
QK_NOPE = 64
QK_ROPE = 32
V_HEAD = 64
Q_LORA = 256
KV_LORA = 128
CONV_CH = 256
CONV_WIDTH = 31
ML_HEADS = 4
ML_HEAD_DIM = 64
ML_WIDTH = ML_HEADS * ML_HEAD_DIM
CHUNK = 128
D_FF = 4 * D_MODEL
ROPE_THETA = 10000.0
LN_EPS = 1e-5
RMS_EPS = 1e-6
DEPTH = 2
DEEPNORM_ALPHA = (2 * DEPTH) ** 0.25

TM = 256
HALO = 16
HEAD_PAD = LANES

C_CQ = 0
C_CKV = C_CQ + Q_LORA
C_KRG = C_CKV + KV_LORA
C_CONV = C_KRG + LANES
C_MQ = C_CONV + 2 * CONV_CH
C_MK = C_MQ + ML_WIDTH
C_MV = C_MK + ML_WIDTH
C_MO = C_MV + ML_WIDTH
N_GATE = 2 * ML_HEADS
ROPE_PAIR_DIST = QK_ROPE // 4
WIDE_TILES = (768, 512)
LOG2_E = float(np.log2(np.e))
Q_SCALE = (QK_NOPE + QK_ROPE) ** -0.5 * LOG2_E


def _wide_tile(rows):
    for tile in WIDE_TILES + (TM,):
        if rows % tile == 0:
            return tile
    raise ValueError(f"no token tile divides {rows}")


def _layer_norm(v, g, b):
    mu = jnp.mean(v, -1, keepdims=True)
    d = v - mu
    var = jnp.mean(d * d, -1, keepdims=True)
    return d * lax.rsqrt(var + LN_EPS) * g + b


def _rms_norm(v, g):
    return v * lax.rsqrt(jnp.mean(v * v, -1, keepdims=True) + RMS_EPS) * g


def _sigmoid(v):
    return 1.0 / (1.0 + jnp.exp(-v))


def _log_sigmoid(v):
    return jnp.minimum(v, 0.0) - jnp.log1p(jnp.exp(-jnp.abs(v)))


def _rope(t, cos, sin_lo, sin_hi):
    return (t * cos + pltpu.roll(t, LANES - ROPE_PAIR_DIST, 1) * sin_lo
            + pltpu.roll(t, ROPE_PAIR_DIST, 1) * sin_hi)


def _ada_kernel(c_ref, w_ref, b_ref, o_ref):
    c = c_ref[...]
    s = (c * _sigmoid(c)).astype(BF16)
    o_ref[0] = jnp.dot(s, w_ref[0].astype(BF16), preferred_element_type=F32) + b_ref[0]


def _ada_call(cc, w_ada, b_ada):
    depth, d, n6 = w_ada.shape
    rows = cc.shape[0]
    nblk = n6 // d
    return pl.pallas_call(
        _ada_kernel,
        out_shape=jax.ShapeDtypeStruct((depth, rows, n6), F32),
        grid=(depth, nblk),
        in_specs=[
            pl.BlockSpec((rows, d), lambda l, n: (0, 0)),
            pl.BlockSpec((1, d, d), lambda l, n: (l, 0, n)),
            pl.BlockSpec((1, 1, d), lambda l, n: (l, 0, n)),
        ],
        out_specs=pl.BlockSpec((1, rows, d), lambda l, n: (l, 0, n)),
        compiler_params=pltpu.CompilerParams(dimension_semantics=("parallel", "parallel")),
        name="ada_mod",
    )(cc, w_ada, b_ada.reshape(depth, 1, n6))


def _mod_rows(mod_ref, modc_ref, idx, rows, ctx_rows):
    m = mod_ref[0, idx:idx + 1, :]
    if ctx_rows == 0:
        return m
    is_ctx = jnp.logical_and(lax.broadcasted_iota(jnp.int32, (rows, 1), 0) < ctx_rows, pl.program_id(1) == 0)
    return jnp.where(is_ctx, modc_ref[0, idx:idx + 1, :], m)


def _in_kernel(*refs, n_x):
    x_refs = refs[:n_x]
    (mod_ref, modc_ref, w_ref, gq_ref, gkv_ref, wuq_ref, wuk_ref, wuv_ref, bg_ref, cos_ref, slo_ref, shi_ref,
     q_ref, k_ref, v_ref, z_ref, mq_ref, mkt_ref, mv_ref, mo_ref, gir_ref, gfr_ref) = refs[n_x:]
    if n_x == 1:
        x = x_refs[0][0]
        tm = x.shape[0]
        sh1 = _mod_rows(mod_ref, modc_ref, 0, tm, CTX_LEN)
        sc1 = _mod_rows(mod_ref, modc_ref, 1, tm, CTX_LEN)
        xm = (x * (1.0 + sc1) + sh1).astype(BF16)
    else:
        ctx_ref, first_ref = x_refs[0], x_refs[1]
        is_ctx = pl.program_id(1) == 0
        pieces = []
        for j, piece_ref in enumerate(x_refs[1:]):
            sh1, sc1 = mod_ref[0, 0:1, :], mod_ref[0, 1:2, :]
            piece = piece_ref[0]
            if j == 0:
                sh1 = jnp.where(is_ctx, modc_ref[0, 0:1, :], sh1)
                sc1 = jnp.where(is_ctx, modc_ref[0, 1:2, :], sc1)
                piece = jnp.where(is_ctx, ctx_ref[0], first_ref[0])
            pieces.append((piece * (1.0 + sc1) + sh1).astype(BF16))
        xm = jnp.concatenate(pieces, axis=0)
        tm = xm.shape[0]

    def proj(lo, hi):
        return jnp.dot(xm, w_ref[:, lo:hi], preferred_element_type=F32)

    cos = cos_ref[...]
    slo = slo_ref[...]
    shi = shi_ref[...]

    cq = proj(C_CQ, C_CQ + Q_LORA)
    ckv_krg = proj(C_CKV, C_KRG + LANES)
    a = proj(C_CONV, C_CONV + CONV_CH)
    cqn = _rms_norm(cq, gq_ref[...]).astype(BF16)
    gt = proj(C_CONV + CONV_CH, C_CONV + 2 * CONV_CH)
    ckvn = _rms_norm(ckv_krg[:, :KV_LORA], gkv_ref[...]).astype(BF16)
    krg = ckv_krg[:, KV_LORA:]

    qf = jnp.dot(cqn, wuq_ref[...], preferred_element_type=F32) * Q_SCALE
    z_ref[0] = a * _sigmoid(gt)
    mq = proj(C_MQ, C_MQ + ML_WIDTH)
    kf = jnp.dot(ckvn, wuk_ref[...], preferred_element_type=F32)
    for h in range(MLA_HEADS):
        sl = slice(h * HEAD_PAD, (h + 1) * HEAD_PAD)
        q_ref[0, :, sl] = _rope(qf[:, sl], cos, slo, shi).astype(BF16)
    mq_ref[0] = mq.astype(BF16)

    mk = proj(C_MK, C_MK + ML_WIDTH)
    vf = jnp.dot(ckvn, wuv_ref[...], preferred_element_type=F32)
    lane = lax.broadcasted_iota(jnp.int32, (1, LANES), 1)
    is_rope_lane = jnp.logical_and(lane >= QK_NOPE, lane < QK_NOPE + QK_ROPE)
    kr = jnp.where(is_rope_lane, _rope(krg, cos, slo, shi), 0.0)
    for h in range(MLA_HEADS):
        sl = slice(h * HEAD_PAD, (h + 1) * HEAD_PAD)
        k_ref[0, :, sl] = (kf[:, sl] + kr).astype(BF16)
    mv = proj(C_MV, C_MV + ML_WIDTH)
    mo = proj(C_MO, C_MO + ML_WIDTH)
    v_ref[0] = vf.astype(BF16)

    mk_t = (mk * (ML_HEAD_DIM ** -0.5)).T
    g_t = (krg + bg_ref[...]).T
    gi_t = g_t[:N_GATE]
    gf_t = _log_sigmoid(g_t[N_GATE:2 * N_GATE])
    for cc in range(tm // CHUNK):
        cs = slice(cc * CHUNK, (cc + 1) * CHUNK)
        mkt_ref[0, cc] = mk_t[:, cs].astype(BF16)
        gir_ref[0, cc] = gi_t[:, cs]
        gfr_ref[0, cc] = gf_t[:, cs]
    mv_ref[0] = mv.astype(BF16)
    mo_ref[0] = mo


def _in_call(stream, mod, w_in_p, gq, gkv, wuq, wuk, wuv, bg, cos, slo, shi):
    split = isinstance(stream, tuple)
    if split:
        ctx, x = stream
        b, seq, d = x.shape
        ltot = CTX_LEN + seq
    else:
        b, ltot, d = stream.shape
    tm = _wide_tile(ltot)
    nt = ltot // tm
    nc = ltot // CHUNK
    ctx_row = mod.shape[0] - SUBLANES

    def tok(width):
        return pl.BlockSpec((1, tm, width), lambda i, t: (i, t, 0))

    def const2(shape):
        return pl.BlockSpec(shape, lambda i, t: (0, 0), pipeline_mode=pl.Buffered(1))

    tab = pl.BlockSpec((tm, LANES), lambda i, t: (t, 0))
    grow = pl.BlockSpec((1, tm // CHUNK, N_GATE, CHUNK), lambda i, t: (i, t, 0, 0))
    outs = [
        (jax.ShapeDtypeStruct((b, ltot, MLA_HEADS * HEAD_PAD), BF16), tok(MLA_HEADS * HEAD_PAD)),
        (jax.ShapeDtypeStruct((b, ltot, MLA_HEADS * HEAD_PAD), BF16), tok(MLA_HEADS * HEAD_PAD)),
        (jax.ShapeDtypeStruct((b, ltot, MLA_HEADS * V_HEAD), BF16), tok(MLA_HEADS * V_HEAD)),
        (jax.ShapeDtypeStruct((b, ltot, CONV_CH), F32), tok(CONV_CH)),
        (jax.ShapeDtypeStruct((b, ltot, ML_WIDTH), BF16), tok(ML_WIDTH)),
        (jax.ShapeDtypeStruct((b, nc, ML_WIDTH, CHUNK), BF16),
         pl.BlockSpec((1, tm // CHUNK, ML_WIDTH, CHUNK), lambda i, t: (i, t, 0, 0))),
        (jax.ShapeDtypeStruct((b, ltot, ML_WIDTH), BF16), tok(ML_WIDTH)),
        (jax.ShapeDtypeStruct((b, ltot, ML_WIDTH), F32), tok(ML_WIDTH)),
        (jax.ShapeDtypeStruct((b, nc, N_GATE, CHUNK), F32), grow),
        (jax.ShapeDtypeStruct((b, nc, N_GATE, CHUNK), F32), grow),
    ]
    if split:
        ppt = tm // TM
        x_args = [ctx] + [x] * ppt
        x_specs = [pl.BlockSpec((1, TM, d), lambda i, t: (i, 0, 0))] + [
            pl.BlockSpec((1, TM, d), lambda i, t, j=j: (i, jnp.maximum(t * ppt + j - 1, 0), 0)) for j in range(ppt)]
    else:
        x_args, x_specs = [stream], [tok(d)]
    return pl.pallas_call(
        functools.partial(_in_kernel, n_x=len(x_args)),
        out_shape=[o[0] for o in outs],
        grid=(b, nt),
        in_specs=x_specs + [
            pl.BlockSpec((1, 6, d), lambda i, t: (i, 0, 0)),
            pl.BlockSpec((1, 6, d), lambda i, t: (ctx_row, 0, 0)),
            const2(w_in_p.shape),
            const2(gq.shape), const2(gkv.shape),
            const2(wuq.shape), const2(wuk.shape), const2(wuv.shape),
            const2(bg.shape),
            tab, tab, tab,
        ],
        out_specs=[o[1] for o in outs],
        compiler_params=pltpu.CompilerParams(dimension_semantics=("parallel", "parallel"),
                                             vmem_limit_bytes=VMEM_LIMIT_BYTES),
        name="in_proj",
    )(*x_args, mod, mod, w_in_p, gq, gkv, wuq, wuk, wuv, bg, cos, slo, shi)


ATT_PAIRS = 4


def _attn_kernel(*refs, with_ctx, ltot):
    qa_ref, qb_ref, k_ref, v_ref = refs[:4]
    if with_ctx:
        octx_ref, o_ref, vaug_ref = refs[4:]
    else:
        o_ref, vaug_ref = refs[4:]
    pair_w = 2 * V_HEAD
    nh = 2 * ATT_PAIRS

    @pl.when(pl.program_id(2) == 0)
    def _():
        ones = jnp.ones((ltot, pair_w), BF16)
        for pp in range(ATT_PAIRS):
            vaug_ref[pp] = jnp.concatenate([v_ref[0, :, pp * pair_w:(pp + 1) * pair_w], ones], axis=1)

    def run(q_refs, nk, out_ref):
        chains = [(q_ref, h) for q_ref in q_refs for h in range(nh)]

        def scores(chain):
            q_ref, h = chain
            sl = slice(h * HEAD_PAD, (h + 1) * HEAD_PAD)
            return lax.dot_general(q_ref[0, :, sl], k_ref[0, :nk, sl], (((1,), (1,)), ((), ())),
                                   preferred_element_type=F32)

        def probs(s):
            return jnp.exp2((s - jnp.max(s, -1, keepdims=True)).astype(BF16))

        def weighted(chain, p):
            o = jnp.dot(p, vaug_ref[chain[1] // 2, :nk, :], preferred_element_type=F32)
            return o[:, :pair_w] / o[:, pair_w:]

        s_next = scores(chains[0])
        p_prev = None
        outs = []
        for c, chain in enumerate(chains):
            s_cur = s_next
            if c + 1 < len(chains):
                s_next = scores(chains[c + 1])
            p_cur = probs(s_cur)
            if p_prev is not None:
                outs.append(weighted(chains[c - 1], p_prev))
            p_prev = p_cur
        outs.append(weighted(chains[-1], p_prev))
        lane = lax.broadcasted_iota(jnp.int32, outs[0].shape, 1)
        for qi in range(len(q_refs)):
            for pp in range(ATT_PAIRS):
                pair = jnp.where(lane < V_HEAD, outs[qi * nh + 2 * pp], outs[qi * nh + 2 * pp + 1])
                out_ref[0, qi * TM:(qi + 1) * TM, pp * pair_w:(pp + 1) * pair_w] = pair.astype(out_ref.dtype)

    if with_ctx:
        t = pl.program_id(2)

        @pl.when(t == 0)
        def _():
            run([qa_ref], CTX_LEN, octx_ref)

        @pl.when(t > 0)
        def _():
            run([qa_ref, qb_ref], ltot, o_ref)
    else:
        run([qa_ref, qb_ref], ltot, o_ref)


def _attn_call(q, k, v, with_ctx):
    b, ltot, _ = q.shape
    ctx_tiles = CTX_LEN // TM
    seq = ltot - CTX_LEN
    assert ctx_tiles == 1 and seq % (2 * TM) == 0
    lead = 1 if with_ctx else 0
    nt = seq // (2 * TM) + lead
    ngroup = MLA_HEADS // (2 * ATT_PAIRS)
    qk_w = 2 * ATT_PAIRS * HEAD_PAD
    v_w = 2 * ATT_PAIRS * V_HEAD

    def q_tile(which):
        return lambda i, j, t: (i, jnp.where(t < lead, which, 2 * (t - lead) + ctx_tiles + which), j)

    lat_shape = jax.ShapeDtypeStruct((b, seq, MLA_HEADS * V_HEAD), BF16)
    lat_spec = pl.BlockSpec((1, 2 * TM, v_w), lambda i, j, t: (i, jnp.maximum(t - lead, 0), j))
    if with_ctx:
        out_shape = [jax.ShapeDtypeStruct((b, CTX_LEN, MLA_HEADS * V_HEAD), BF16), lat_shape]
        out_specs = [pl.BlockSpec((1, TM, v_w), lambda i, j, t: (i, 0, j)), lat_spec]
    else:
        out_shape, out_specs = lat_shape, lat_spec
    return pl.pallas_call(
        functools.partial(_attn_kernel, with_ctx=with_ctx, ltot=ltot),
        out_shape=out_shape,
        grid=(b, ngroup, nt),
        in_specs=[
            pl.BlockSpec((1, TM, qk_w), q_tile(0)),
            pl.BlockSpec((1, TM, qk_w), q_tile(1)),
            pl.BlockSpec((1, ltot, qk_w), lambda i, j, t: (i, 0, j)),
            pl.BlockSpec((1, ltot, v_w), lambda i, j, t: (i, 0, j)),
        ],
        out_specs=out_specs,
        scratch_shapes=[pltpu.VMEM((ATT_PAIRS, ltot, 4 * V_HEAD), BF16)],
        compiler_params=pltpu.CompilerParams(dimension_semantics=("parallel", "parallel", "arbitrary"),
                                             vmem_limit_bytes=VMEM_LIMIT_BYTES),
        name="mla_attn",
    )(q, q, k, v)


def _mlstm_kernel(q_ref, kt_ref, v_ref, mo_ref, gi_ref, gf_ref, g_ref,
                  o_ref, hf_ref, hb_ref, sn_ref, a_ref, r_ref, e_ref, tot_ref, gmax_ref, *, nchunk, nctx):
    t_ = CHUNK
    npair = ML_HEADS // 2
    row = lax.broadcasted_iota(jnp.int32, (t_, t_), 0)
    col = lax.broadcasted_iota(jnp.int32, (t_, t_), 1)
    causal_masks = (col <= row, col >= row)
    blockdiag = (row < ML_HEAD_DIM) == (col < ML_HEAD_DIM)
    half0 = lax.broadcasted_iota(jnp.int32, (1, LANES), 1) < ML_HEAD_DIM
    rowhalf0 = lax.broadcasted_iota(jnp.int32, (LANES, 1), 0) < ML_HEAD_DIM
    lane8 = lax.broadcasted_iota(jnp.int32, (1, t_), 1)
    n_parts = 3
    zpad = jnp.zeros((t_ - 2 * n_parts * N_GATE, t_), F32)
    ones_b = jnp.ones((t_, LANES), BF16)
    zeros_b = jnp.zeros((t_, LANES), BF16)
    half_ones = tuple(jnp.broadcast_to(jnp.where(half0, on, 1.0 - on), (t_, LANES)).astype(BF16) for on in (1.0, 0.0))

    r2 = lax.broadcasted_iota(jnp.int32, (t_, 3 * LANES), 0)
    l2 = lax.broadcasted_iota(jnp.int32, (t_, 3 * LANES), 1)
    blk, gate = r2 >> 3, r2 & 7
    lblk, lhalf = l2 >> 7, (l2 >> 6) & 1
    is_r_part = blk < n_parts
    is_b_part = jnp.logical_and(blk >= n_parts, blk < 2 * n_parts)
    consts = {}
    for d in range(2):
        for p in range(npair):
            c0 = d * ML_HEADS + 2 * p
            want_gate = jnp.where(lblk == 2, c0 + lhalf, c0 + lblk)
            want_part = jnp.where(lblk == 2, is_b_part.astype(jnp.int32), is_r_part.astype(jnp.int32)) == 1
            consts[d, p] = jnp.where(jnp.logical_and(want_part, gate == want_gate), 1.0, 0.0).astype(BF16)

    def split3(x):
        hi = x.astype(BF16).astype(F32)
        rem = x - hi
        mid = rem.astype(BF16).astype(F32)
        return [hi, mid, rem - mid]

    def lane_scan(x, op, fill, reverse):
        k = 1
        while k < t_:
            if reverse:
                shifted, valid = pltpu.roll(x, t_ - k, 1), lane8 < t_ - k
            else:
                shifted, valid = pltpu.roll(x, k, 1), lane8 >= k
            x = op(x, jnp.where(valid, shifted, fill))
            k *= 2
        return x

    nrow = nchunk * N_GATE
    is_fwd_row = (lax.broadcasted_iota(jnp.int32, (nrow, t_), 0) & (N_GATE - 1)) < ML_HEADS
    lf = gf_ref[0].reshape(nrow, t_) * LOG2_E
    lf_parts = jnp.concatenate([part.astype(BF16) for part in split3(lf)], axis=1)
    tri_up = jnp.where(row <= col, 1.0, 0.0).astype(BF16)
    tri_dn = jnp.where(row >= col, 1.0, 0.0).astype(BF16)
    b = jnp.where(is_fwd_row,
                  jnp.dot(lf_parts, jnp.concatenate([tri_up] * n_parts, axis=0), preferred_element_type=F32),
                  jnp.dot(lf_parts, jnp.concatenate([tri_dn] * n_parts, axis=0), preferred_element_type=F32))
    tot = jnp.sum(lf, axis=1, keepdims=True)
    r = gi_ref[0].reshape(nrow, t_) * LOG2_E - b
    rmax = jnp.where(is_fwd_row, lane_scan(r, jnp.maximum, -jnp.inf, False),
                     lane_scan(r, jnp.maximum, -jnp.inf, True))
    rlast = jnp.max(r, axis=1, keepdims=True)
    col_parts = split3(rmax) + split3(b)
    for c in range(nchunk):
        rows_c = slice(c * N_GATE, (c + 1) * N_GATE)
        packed = jnp.concatenate([part[rows_c] for part in col_parts] + [zpad], axis=0)
        a_ref[c] = packed.T.astype(BF16)
    r_ref[...] = r
    e_ref[...] = jnp.exp2(r - rlast)
    tot_ref[...] = jnp.broadcast_to(tot, (nrow, t_))
    gmax_ref[...] = jnp.broadcast_to(tot + rlast, (nrow, t_))

    def prep(d, c, m_prev):
        rows_c = pl.ds(pl.multiple_of(c * N_GATE, N_GATE), N_GATE)
        tot_c = tot_ref[rows_c, :]
        gmax_c = gmax_ref[rows_c, :]
        m_new = jnp.maximum(tot_c + m_prev, gmax_c)
        decay = jnp.exp2(tot_c + m_prev - m_new)
        gamma = jnp.exp2(gmax_c - m_new)
        return a_ref[c], r_ref[rows_c, :], e_ref[rows_c, :], decay, gamma, m_new

    def stage_mxu(d, p, c, pre):
        a, _, e_row, _, _, _ = pre
        c0 = d * ML_HEADS + 2 * p
        c1 = c0 + 1
        sl = slice(p * LANES, (p + 1) * LANES)
        q = q_ref[0, c, :, sl]
        kt = kt_ref[0, c, sl, :]
        v = v_ref[0, c, :, sl]
        z = jnp.dot(a, consts[d, p], preferred_element_type=F32)
        kt_heads = jnp.concatenate([jnp.where(rowhalf0, kt, zeros_b), jnp.where(rowhalf0, zeros_b, kt)], axis=1)
        qk = jnp.dot(q, kt_heads, preferred_element_type=F32)
        wkt = kt * jnp.where(rowhalf0, e_row[c0:c0 + 1, :], e_row[c1:c1 + 1, :]).astype(BF16)
        upd = jnp.dot(wkt, jnp.concatenate([v, ones_b], axis=1), preferred_element_type=F32)
        sn = sn_ref[2 * p + d]
        qs = jnp.dot(q, sn.astype(BF16), preferred_element_type=F32)
        return z, qk, upd, sn, qs, v

    def stage_intra(d, p, r_rows, z, qk, v):
        c0 = d * ML_HEADS + 2 * p
        nd = jnp.zeros((t_, 2 * LANES), F32)
        for hh in range(2):
            hs = slice(hh * LANES, (hh + 1) * LANES)
            wts = jnp.exp2(jnp.where(causal_masks[d], r_rows[c0 + hh:c0 + hh + 1, :] - z[:, hs], -jnp.inf))
            v_h = jnp.where(half0, v, zeros_b) if hh == 0 else jnp.where(half0, zeros_b, v)
            nd = nd + jnp.dot((qk[:, hs] * wts).astype(BF16), jnp.concatenate([v_h, half_ones[hh]], axis=1),
                              preferred_element_type=F32)
        return nd

    def stage_out(d, p, c, pre, m_prev, h_ref, z, upd, sn, qs, nd):
        _, _, _, decay, gamma, _ = pre
        c0 = d * ML_HEADS + 2 * p
        c1 = c0 + 1
        sl = slice(p * LANES, (p + 1) * LANES)
        r_bc = jnp.where(half0, z[:, :LANES], z[:, LANES:2 * LANES])
        b_bc = z[:, 2 * LANES:]
        m_bc = jnp.where(half0, m_prev[c0:c0 + 1, :], m_prev[c1:c1 + 1, :])
        p_bc = jnp.maximum(m_bc, r_bc)
        alpha = jnp.exp2(r_bc - p_bc)
        beta = jnp.exp2(m_bc - p_bc)
        num = alpha * nd[:, :LANES] + beta * qs[:, :LANES]
        den = alpha * nd[:, LANES:] + beta * qs[:, LANES:]
        h_ref[c, :, sl] = num / jnp.maximum(jnp.abs(den), jnp.exp2(-(b_bc + p_bc)))

        dec_rows = jnp.where(rowhalf0, decay[c0:c0 + 1, :], decay[c1:c1 + 1, :])
        gam_rows = jnp.where(blockdiag, jnp.where(rowhalf0, gamma[c0:c0 + 1, :], gamma[c1:c1 + 1, :]), 0.0)
        sn_ref[2 * p + d] = jnp.concatenate([dec_rows * sn[:, :LANES] + gam_rows * upd[:, :LANES],
                                             dec_rows * sn[:, LANES:] + gam_rows * upd[:, LANES:]], axis=1)

    sn_ref[...] = jnp.zeros(sn_ref.shape, F32)

    def body(i, carry):
        cb = jnp.where(i < nctx, nctx - 1 - i, nchunk - 1 + nctx - i)
        streams = [(d, p, c, h_ref) for d, c, h_ref in ((0, i, hf_ref), (1, cb, hb_ref)) for p in range(npair)]
        pres = {0: prep(0, i, carry[0]), 1: prep(1, cb, carry[1])}
        first = [stage_mxu(d, p, c, pres[d]) for d, p, c, _ in streams]
        intra = [stage_intra(d, p, pres[d][1], z, qk, v)
                 for (d, p, _, _), (z, qk, _, _, _, v) in zip(streams, first)]
        for (d, p, c, h_ref), (z, _, upd, sn, qs, _), nd in zip(streams, first, intra):
            stage_out(d, p, c, pres[d], carry[d], h_ref, z, upd, sn, qs, nd)
        return pres[0][-1], pres[1][-1]

    m_init = jnp.zeros((N_GATE, t_), F32)
    lax.fori_loop(0, nchunk, body, (m_init, m_init), unroll=2)

    g = g_ref[...]
    head_shift = ML_HEAD_DIM.bit_length() - 1
    head_of_row = lax.broadcasted_iota(jnp.int32, (ML_WIDTH, ML_WIDTH), 0) >> head_shift
    head_of_col = lax.broadcasted_iota(jnp.int32, (ML_WIDTH, ML_WIDTH), 1) >> head_shift
    avg = jnp.where(head_of_row == head_of_col, 1.0 / ML_HEAD_DIM, 0.0).astype(BF16)
    avg2 = jnp.concatenate([avg, avg], axis=0)

    def head_mean(x):
        hi = x.astype(BF16)
        lo = (x - hi.astype(F32)).astype(BF16)
        return jnp.dot(jnp.concatenate([hi, lo], axis=1), avg2, preferred_element_type=F32)

    def fin(c, _):
        hh = (hf_ref[c] + hb_ref[c]) * _sigmoid(mo_ref[0, c])
        dl = hh - head_mean(hh)
        var = head_mean(dl * dl)
        o_ref[0, c] = (dl * lax.rsqrt(var + LN_EPS) * g).astype(o_ref.dtype)
        return 0

    lax.fori_loop(0, nchunk, fin, 0, unroll=6)


def _mlstm_call(mq, mkt, mv, mo, gir, gfr, g):
    b, ltot, w = mq.shape
    nc = ltot // CHUNK
    r4 = lambda a: a.reshape(b, nc, CHUNK, a.shape[-1])
    blk = lambda rows, width: pl.BlockSpec((1, nc, rows, width), lambda i: (i, 0, 0, 0))
    out = pl.pallas_call(
        functools.partial(_mlstm_kernel, nchunk=nc, nctx=CTX_LEN // CHUNK),
        out_shape=jax.ShapeDtypeStruct((b, nc, CHUNK, w), BF16),
        grid=(b,),
        in_specs=[blk(CHUNK, w), blk(w, CHUNK), blk(CHUNK, w), blk(CHUNK, w),
                  blk(N_GATE, CHUNK), blk(N_GATE, CHUNK),
                  pl.BlockSpec((1, w), lambda i: (0, 0))],
        out_specs=blk(CHUNK, w),
        scratch_shapes=[pltpu.VMEM((nc, CHUNK, w), F32), pltpu.VMEM((nc, CHUNK, w), F32),
                        pltpu.VMEM((2 * (ML_HEADS // 2), LANES, 2 * LANES), F32),
                        pltpu.VMEM((nc, CHUNK, LANES), BF16), pltpu.VMEM((nc * N_GATE, CHUNK), F32),
                        pltpu.VMEM((nc * N_GATE, CHUNK), F32), pltpu.VMEM((nc * N_GATE, CHUNK), F32),
                        pltpu.VMEM((nc * N_GATE, CHUNK), F32)],
        compiler_params=pltpu.CompilerParams(dimension_semantics=("parallel",),
                                             vmem_limit_bytes=VMEM_LIMIT_BYTES),
        name="mlstm",
    )(r4(mq), mkt, r4(mv), r4(mo), gir, gfr, g)
    return out.reshape(b, ltot, w)


def _post_kernel(*refs, t0, nt_all, split):
    if split:
        ctx_ref, x_ref, mod_ref, attc_ref, attl_ref = refs[:5]
        is_ctx = pl.program_id(1) == 0
        resid = jnp.where(is_ctx, ctx_ref[0], x_ref[0])
        att = jnp.where(is_ctx, attc_ref[0], attl_ref[0])
    else:
        x_ref, mod_ref, att_ref = refs[:3]
        resid = x_ref[0]
        att = att_ref[0]
    (z_ref, zp_ref, zn_ref, ml_ref, cw_ref, cb_ref, cg_ref, cbb_ref,
     wo_ref, g1_ref, b1_ref, o_ref, zbuf_ref, zsh_ref) = refs[5 if split else 3:]
    t = pl.program_id(1) + t0
    n_att = MLA_HEADS * V_HEAD
    y = jnp.dot(att, wo_ref[0:n_att, :], preferred_element_type=F32)
    y = y + jnp.dot(ml_ref[0], wo_ref[n_att + CONV_CH:, :], preferred_element_type=F32)
    left_ok = t >= 2
    right_ok = jnp.logical_and(t >= 1, t <= nt_all - 2)
    zbuf_ref[0:HALO, :] = jnp.where(left_ok, zp_ref[0], 0.0)
    zbuf_ref[HALO:HALO + TM, :] = z_ref[0]
    zbuf_ref[HALO + TM:, :] = jnp.where(right_ok, zn_ref[0], 0.0)
    off = HALO - CONV_WIDTH // 2
    span = TM + ((off + CONV_WIDTH - 1) // SUBLANES) * SUBLANES
    for s in range(SUBLANES):
        zsh_ref[s] = zbuf_ref[s:s + span, :]
    acc = jnp.zeros((TM, CONV_CH), F32)
    for j in range(CONV_WIDTH):
        s, a = (off + j) % SUBLANES, ((off + j) // SUBLANES) * SUBLANES
        acc = acc + cw_ref[j:j + 1, :] * zsh_ref[s, a:a + TM, :]
    cv = _layer_norm(acc + cb_ref[...], cg_ref[...], cbb_ref[...])
    cv = cv * _sigmoid(cv)

    y = y + jnp.dot(cv.astype(BF16), wo_ref[n_att:n_att + CONV_CH, :], preferred_element_type=F32)
    g1 = mod_ref[0, 2:3, :]
    o_ref[0] = _layer_norm(DEEPNORM_ALPHA * resid + g1 * y, g1_ref[...], b1_ref[...])


def _post_call(stream, mod, att, z, ml, cw, cb, cg, cbb, wo, g1, b1, t0):
    split = isinstance(stream, tuple)
    if split:
        ctx, x = stream
        b, seq, d = x.shape
        ltot = CTX_LEN + seq
    else:
        b, ltot, d = stream.shape
    nt_all = ltot // TM
    nt = nt_all - t0
    ctx_row = mod.shape[0] - SUBLANES
    hpt = TM // HALO
    nhalo = ltot // HALO

    def tok(width, off):
        return pl.BlockSpec((1, TM, width), lambda i, t: (i, t + off, 0))

    def const2(shape):
        return pl.BlockSpec(shape, lambda i, t: (0, 0))

    def pieces(width):
        return [pl.BlockSpec((1, TM, width), lambda i, t: (i, 0, 0)),
                pl.BlockSpec((1, TM, width), lambda i, t: (i, jnp.maximum(t - 1, 0), 0))]

    if split:
        assert t0 == 0 and isinstance(att, (tuple, list))
        x_args, x_specs = [ctx, x], pieces(d)
        att_args, att_specs = list(att), pieces(att[0].shape[-1])
    else:
        x_args, x_specs = [stream], [tok(d, t0)]
        att_args, att_specs = [att], [tok(att.shape[-1], 0)]
    return pl.pallas_call(
        functools.partial(_post_kernel, t0=t0, nt_all=nt_all, split=split),
        out_shape=jax.ShapeDtypeStruct((b, nt * TM, d), F32),
        grid=(b, nt),
        in_specs=x_specs + [
            pl.BlockSpec((1, 6, d), lambda i, t: (jnp.where(t + t0 == 0, ctx_row, i), 0, 0)),
        ] + att_specs + [
            tok(CONV_CH, t0),
            pl.BlockSpec((1, HALO, CONV_CH), lambda i, t: (i, jnp.maximum((t + t0) * hpt - 1, 0), 0)),
            pl.BlockSpec((1, HALO, CONV_CH), lambda i, t: (i, jnp.minimum((t + t0 + 1) * hpt, nhalo - 1), 0)),
            tok(ML_WIDTH, t0),
            const2(cw.shape), const2(cb.shape), const2(cg.shape), const2(cbb.shape),
            const2(wo.shape), const2(g1.shape), const2(b1.shape),
        ],
        out_specs=tok(d, 0),
        scratch_shapes=[pltpu.VMEM((TM + 2 * HALO, CONV_CH), F32),
                        pltpu.VMEM((SUBLANES, TM + 2 * HALO - SUBLANES, CONV_CH), F32)],
        compiler_params=pltpu.CompilerParams(dimension_semantics=("parallel", "parallel"),
                                             vmem_limit_bytes=VMEM_LIMIT_BYTES),
        name="out_proj",
    )(*x_args, mod, *att_args, z, z, z, ml, cw, cb, cg, cbb, wo, g1, b1)


FF_CHUNK = 1024


def _mlp_kernel(x_ref, mod_ref, modc_ref, w1_ref, b1_ref, w2_ref, b2_ref, g_ref, b_ref, o_ref, *, ctx_rows):
    x = x_ref[0]
    sh2 = _mod_rows(mod_ref, modc_ref, 3, x.shape[0], ctx_rows)
    sc2 = _mod_rows(mod_ref, modc_ref, 4, x.shape[0], ctx_rows)
    g2 = _mod_rows(mod_ref, modc_ref, 5, x.shape[0], ctx_rows)
    u = (x * (1.0 + sc2) + sh2).astype(BF16)
    def hidden(c):
        sl = slice(c * FF_CHUNK, (c + 1) * FF_CHUNK)
        return jnp.dot(u, w1_ref[:, sl], preferred_element_type=F32) + b1_ref[:, sl]

    acc = jnp.zeros(x.shape, F32)
    n_chunk = D_FF // FF_CHUNK
    pre = hidden(0)
    for c in range(n_chunk):
        cur = pre
        if c + 1 < n_chunk:
            pre = hidden(c + 1)
        h = jnp.maximum(cur, 0.0)
        acc = acc + jnp.dot((h * h).astype(BF16), w2_ref[c * FF_CHUNK:(c + 1) * FF_CHUNK, :],
                            preferred_element_type=F32)
    o_ref[0] = _layer_norm(DEEPNORM_ALPHA * x + g2 * (acc + b2_ref[...]), g_ref[...], b_ref[...])


def _mlp_call(x1, mod, w1, b1, w2, b2, g, bb, with_ctx):
    b, ln, d = x1.shape
    tm = _wide_tile(ln)
    nt = ln // tm
    ctx_row = mod.shape[0] - SUBLANES

    def const2(shape):
        return pl.BlockSpec(shape, lambda i, t: (0, 0), pipeline_mode=pl.Buffered(1))

    return pl.pallas_call(
        functools.partial(_mlp_kernel, ctx_rows=CTX_LEN if with_ctx else 0),
        out_shape=jax.ShapeDtypeStruct((b, ln, d), F32),
        grid=(b, nt),
        in_specs=[
            pl.BlockSpec((1, tm, d), lambda i, t: (i, t, 0)),
            pl.BlockSpec((1, 6, d), lambda i, t: (i, 0, 0)),
            pl.BlockSpec((1, 6, d), lambda i, t: (ctx_row, 0, 0)),
            const2(w1.shape), const2(b1.shape), const2(w2.shape), const2(b2.shape),
            const2(g.shape), const2(bb.shape),
        ],
        out_specs=pl.BlockSpec((1, tm, d), lambda i, t: (i, t, 0)),
        compiler_params=pltpu.CompilerParams(dimension_semantics=("parallel", "parallel"),
                                             vmem_limit_bytes=VMEM_LIMIT_BYTES),
        name="mlp",
    )(x1, mod, mod, w1, b1, w2, b2, g, bb)


def _rope_tables(seq):
    half = QK_ROPE // 2
    nf = half // 2
    pos = np.arange(seq)
    inv_freq = ROPE_THETA ** (-np.arange(nf, dtype=np.float32) / nf)
    cos = np.ones((CTX_LEN + seq, LANES), np.float32)
    slo = np.zeros((CTX_LEN + seq, LANES), np.float32)
    shi = np.zeros((CTX_LEN + seq, LANES), np.float32)
    for part, p in enumerate((pos // GRID_W, pos % GRID_W)):
        ang = p.astype(np.float32)[:, None] * inv_freq[None, :]
        c, s = np.cos(ang), np.sin(ang)
        base = QK_NOPE + part * half
        cos[CTX_LEN:, base:base + nf] = c
        cos[CTX_LEN:, base + nf:base + half] = c
        slo[CTX_LEN:, base:base + nf] = -s
        shi[CTX_LEN:, base + nf:base + half] = s
    return jnp.asarray(cos), jnp.asarray(slo), jnp.asarray(shi)


def _pad_in_weights(w_in):
    d = w_in.shape[0]
    offs = np.cumsum([0, Q_LORA, KV_LORA, QK_ROPE, 2 * CONV_CH, ML_WIDTH, ML_WIDTH, ML_WIDTH, ML_WIDTH])
    o_cq, o_ckv, o_kr, o_conv, o_mq, o_mk, o_mv, o_mo, o_mg = [int(o) for o in offs]
    z = lambda n: jnp.zeros((d, n), w_in.dtype)
    mg = w_in[:, o_mg:o_mg + 4 * ML_HEADS]
    h = ML_HEADS
    gi = jnp.concatenate([mg[:, 0:h], mg[:, 2 * h:3 * h]], 1)
    gf = jnp.concatenate([mg[:, h:2 * h], mg[:, 3 * h:4 * h]], 1)
    cols = [
        w_in[:, o_cq:o_ckv], w_in[:, o_ckv:o_kr],
        gi, gf, z(QK_NOPE - 2 * N_GATE), w_in[:, o_kr:o_conv], z(LANES - QK_NOPE - QK_ROPE),
        w_in[:, o_conv:o_mg],
    ]
    return jnp.concatenate(cols, 1).astype(BF16)


def _pad_gate_bias(b_gates):
    h = ML_HEADS
    pad = jnp.zeros((LANES - 2 * N_GATE,), b_gates.dtype)
    return jnp.concatenate([b_gates[0:h], b_gates[2 * h:3 * h], b_gates[h:2 * h], b_gates[3 * h:4 * h], pad])[None, :]


def _pad_mla_weights(w_uq, w_ukv):
    dq = QK_NOPE + QK_ROPE
    wq = w_uq.reshape(Q_LORA, MLA_HEADS, dq)
    wq = jnp.pad(wq, ((0, 0), (0, 0), (0, HEAD_PAD - dq))).reshape(Q_LORA, MLA_HEADS * HEAD_PAD)
    wkv = w_ukv.reshape(KV_LORA, MLA_HEADS, QK_NOPE + V_HEAD)
    wk = jnp.pad(wkv[:, :, :QK_NOPE], ((0, 0), (0, 0), (0, HEAD_PAD - QK_NOPE)))
    wk = wk.reshape(KV_LORA, MLA_HEADS * HEAD_PAD)
    wv = wkv[:, :, QK_NOPE:].reshape(KV_LORA, MLA_HEADS * V_HEAD)
    return wq.astype(BF16), wk.astype(BF16), wv.astype(BF16)


def kernel(x, c, ctx, c_ctx, w_ada, b_ada, w_in, g_qn, w_uq, g_kvn, w_ukv, conv_w, conv_b, conv_ln_g, conv_ln_b, b_gates, ml_norm_g, w_out, ln1_g, ln1_b, w_mlp1, b_mlp1, w_mlp2, b_mlp2, ln2_g, ln2_b):
    b, seq, d = x.shape
    depth = w_in.shape[0]
    assert ctx.shape[1] == CTX_LEN == TM and seq % TM == 0 and d == D_MODEL and depth == DEPTH
    row = lambda a: a[None, :]

    cc = jnp.concatenate([c, c_ctx[None, :], jnp.zeros((SUBLANES - 1, d), c.dtype)], 0)
    mod_all = _ada_call(cc, w_ada, b_ada).reshape(depth, cc.shape[0], 6, d)
    cos, slo, shi = _rope_tables(seq)

    xx = (ctx, x)
    for l in range(depth):
        last = l == depth - 1
        t0 = 1 if last else 0
        mod = mod_all[l]
        wq, wk, wv = _pad_mla_weights(w_uq[l], w_ukv[l])
        q, k, v, z, mq, mkt, mv, mo, gir, gfr = _in_call(
            xx, mod, _pad_in_weights(w_in[l]), row(g_qn[l]), row(g_kvn[l]), wq, wk, wv,
            _pad_gate_bias(b_gates[l]), cos, slo, shi)
        att = _attn_call(q, k, v, with_ctx=not last)
        ml = _mlstm_call(mq, mkt, mv, mo, gir, gfr, row(ml_norm_g[l]))
        x1 = _post_call(xx, mod, att, z, ml, conv_w[l], row(conv_b[l]), row(conv_ln_g[l]), row(conv_ln_b[l]),
                        w_out[l].astype(BF16), row(ln1_g[l]), row(ln1_b[l]), t0)
        xx = _mlp_call(x1, mod, w_mlp1[l].astype(BF16), row(b_mlp1[l]), w_mlp2[l].astype(BF16),
                       row(b_mlp2[l]), row(ln2_g[l]), row(ln2_b[l]), with_ctx=not last)
    return xx
```

```python
import functools

import numpy as np
import jax
import jax.numpy as jnp
from jax import lax
from jax.experimental import pallas as pl
from jax.experimental.pallas import tpu as pltpu

F32 = jnp.float32
BF16 = jnp.bfloat16

LANES = 128
SUBLANES = 8
VMEM_LIMIT_BYTES = 56 * 1024 * 1024

D_MODEL = 1024
GRID_W = 64
CTX_LEN = 256
MLA_HEADS = 8
QK_NOPE = 64
QK_ROPE = 32
V_HEAD = 64
Q_LORA = 256
KV_LORA = 128
CONV_CH = 256
CONV_WIDTH = 31
ML_HEADS = 4
ML_HEAD_DIM = 64
ML_WIDTH = ML_HEADS * ML_HEAD_DIM
CHUNK = 128
D_FF = 4 * D_MODEL
ROPE_THETA = 10000.0
LN_EPS = 1e-5
RMS_EPS = 1e-6
DEPTH = 2
DEEPNORM_ALPHA = (2 * DEPTH) ** 0.25

TM = 256
HALO = 16
HEAD_PAD = LANES

C_CQ = 0
C_CKV = C_CQ + Q_LORA
C_KRG = C_CKV + KV_LORA
C_CONV = C_KRG + LANES
C_MQ = C_CONV + 2 * CONV_CH
C_MK = C_MQ + ML_WIDTH
C_MV = C_MK + ML_WIDTH
C_MO = C_MV + ML_WIDTH
N_GATE = 2 * ML_HEADS
ROPE_PAIR_DIST = QK_ROPE // 4
WIDE_TILES = (768, 512)
LOG2_E = float(np.log2(np.e))
Q_SCALE = (QK_NOPE + QK_ROPE) ** -0.5 * LOG2_E


def _wide_tile(rows):
    for tile in WIDE_TILES + (TM,):
        if rows % tile == 0:
            return tile
    raise ValueError(f"no token tile divides {rows}")


def _layer_norm(v, g, b):
    mu = jnp.mean(v, -1, keepdims=True)
    d = v - mu
    var = jnp.mean(d * d, -1, keepdims=True)
    return d * lax.rsqrt(var + LN_EPS) * g + b


def _rms_norm(v, g):
    return v * lax.rsqrt(jnp.mean(v * v, -1, keepdims=True) + RMS_EPS) * g


def _sigmoid(v):
    return 1.0 / (1.0 + jnp.exp(-v))


def _log_sigmoid(v):
    return jnp.minimum(v, 0.0) - jnp.log1p(jnp.exp(-jnp.abs(v)))


def _rope(t, cos, sin_lo, sin_hi):
    return (t * cos + pltpu.roll(t, LANES - ROPE_PAIR_DIST, 1) * sin_lo
            + pltpu.roll(t, ROPE_PAIR_DIST, 1) * sin_hi)


def _ada_kernel(c_ref, w_ref, b_ref, o_ref):
    c = c_ref[...]
    s = (c * _sigmoid(c)).astype(BF16)
    o_ref[0] = jnp.dot(s, w_ref[0].astype(BF16), preferred_element_type=F32) + b_ref[0]


def _ada_call(cc, w_ada, b_ada):
    depth, d, n6 = w_ada.shape
    rows = cc.shape[0]
    nblk = n6 // d
    return pl.pallas_call(
        _ada_kernel,
        out_shape=jax.ShapeDtypeStruct((depth, rows, n6), F32),
        grid=(depth, nblk),
        in_specs=[
            pl.BlockSpec((rows, d), lambda l, n: (0, 0)),
            pl.BlockSpec((1, d, d), lambda l, n: (l, 0, n)),
            pl.BlockSpec((1, 1, d), lambda l, n: (l, 0, n)),
        ],
        out_specs=pl.BlockSpec((1, rows, d), lambda l, n: (l, 0, n)),
        compiler_params=pltpu.CompilerParams(dimension_semantics=("parallel", "parallel")),
        name="ada_mod",
    )(cc, w_ada, b_ada.reshape(depth, 1, n6))


def _mod_rows(mod_ref, modc_ref, idx, rows, ctx_rows):
    m = mod_ref[0, idx:idx + 1, :]
    if ctx_rows == 0:
        return m
    is_ctx = jnp.logical_and(lax.broadcasted_iota(jnp.int32, (rows, 1), 0) < ctx_rows, pl.program_id(1) == 0)
    return jnp.where(is_ctx, modc_ref[0, idx:idx + 1, :], m)


def _in_kernel(*refs, n_x):
    x_refs = refs[:n_x]
    (mod_ref, modc_ref, w_ref, gq_ref, gkv_ref, wuq_ref, wuk_ref, wuv_ref, bg_ref, cos_ref, slo_ref, shi_ref,
     q_ref, k_ref, v_ref, z_ref, mq_ref, mkt_ref, mv_ref, mo_ref, gir_ref, gfr_ref) = refs[n_x:]
    if n_x == 1:
        x = x_refs[0][0]
        tm = x.shape[0]
        sh1 = _mod_rows(mod_ref, modc_ref, 0, tm, CTX_LEN)
        sc1 = _mod_rows(mod_ref, modc_ref, 1, tm, CTX_LEN)
        xm = (x * (1.0 + sc1) + sh1).astype(BF16)
    else:
        ctx_ref, first_ref = x_refs[0], x_refs[1]
        is_ctx = pl.program_id(1) == 0
        pieces = []
        for j, piece_ref in enumerate(x_refs[1:]):
            sh1, sc1 = mod_ref[0, 0:1, :], mod_ref[0, 1:2, :]
            piece = piece_ref[0]
            if j == 0:
                sh1 = jnp.where(is_ctx, modc_ref[0, 0:1, :], sh1)
                sc1 = jnp.where(is_ctx, modc_ref[0, 1:2, :], sc1)
                piece = jnp.where(is_ctx, ctx_ref[0], first_ref[0])
            pieces.append((piece * (1.0 + sc1) + sh1).astype(BF16))
        xm = jnp.concatenate(pieces, axis=0)
        tm = xm.shape[0]

    def proj(lo, hi):
        return jnp.dot(xm, w_ref[:, lo:hi], preferred_element_type=F32)

    cos = cos_ref[...]
    slo = slo_ref[...]
    shi = shi_ref[...]

    cq = proj(C_CQ, C_CQ + Q_LORA)
    ckv_krg = proj(C_CKV, C_KRG + LANES)
    a = proj(C_CONV, C_CONV + CONV_CH)
    cqn = _rms_norm(cq, gq_ref[...]).astype(BF16)
    gt = proj(C_CONV + CONV_CH, C_CONV + 2 * CONV_CH)
    ckvn = _rms_norm(ckv_krg[:, :KV_LORA], gkv_ref[...]).astype(BF16)
    krg = ckv_krg[:, KV_LORA:]

    qf = jnp.dot(cqn, wuq_ref[...], preferred_element_type=F32) * Q_SCALE
    z_ref[0] = a * _sigmoid(gt)
    mq = proj(C_MQ, C_MQ + ML_WIDTH)
    kf = jnp.dot(ckvn, wuk_ref[...], preferred_element_type=F32)
    for h in range(MLA_HEADS):
        sl = slice(h * HEAD_PAD, (h + 1) * HEAD_PAD)
        q_ref[0, :, sl] = _rope(qf[:, sl], cos, slo, shi).astype(BF16)
    mq_ref[0] = mq.astype(BF16)

    mk = proj(C_MK, C_MK + ML_WIDTH)
    vf = jnp.dot(ckvn, wuv_ref[...], preferred_element_type=F32)
    lane = lax.broadcasted_iota(jnp.int32, (1, LANES), 1)
    is_rope_lane = jnp.logical_and(lane >= QK_NOPE, lane < QK_NOPE + QK_ROPE)
    kr = jnp.where(is_rope_lane, _rope(krg, cos, slo, shi), 0.0)
    for h in range(MLA_HEADS):
        sl = slice(h * HEAD_PAD, (h + 1) * HEAD_PAD)
        k_ref[0, :, sl] = (kf[:, sl] + kr).astype(BF16)
    mv = proj(C_MV, C_MV + ML_WIDTH)
    mo = proj(C_MO, C_MO + ML_WIDTH)
    v_ref[0] = vf.astype(BF16)

    mk_t = (mk * (ML_HEAD_DIM ** -0.5)).T
    g_t = (krg + bg_ref[...]).T
    gi_t = g_t[:N_GATE]
    gf_t = _log_sigmoid(g_t[N_GATE:2 * N_GATE])
    for cc in range(tm // CHUNK):
        cs = slice(cc * CHUNK, (cc + 1) * CHUNK)
        mkt_ref[0, cc] = mk_t[:, cs].astype(BF16)
        gir_ref[0, cc] = gi_t[:, cs]
        gfr_ref[0, cc] = gf_t[:, cs]
    mv_ref[0] = mv.astype(BF16)
    mo_ref[0] = mo


def _in_call(stream, mod, w_in_p, gq, gkv, wuq, wuk, wuv, bg, cos, slo, shi):
    split = isinstance(stream, tuple)
    if split:
        ctx, x = stream
        b, seq, d = x.shape
        ltot = CTX_LEN + seq
    else:
        b, ltot, d = stream.shape
    tm = _wide_tile(ltot)
    nt = ltot // tm
    nc = ltot // CHUNK
    ctx_row = mod.shape[0] - SUBLANES

    def tok(width):
        return pl.BlockSpec((1, tm, width), lambda i, t: (i, t, 0))

    def const2(shape):
        return pl.BlockSpec(shape, lambda i, t: (0, 0), pipeline_mode=pl.Buffered(1))

    tab = pl.BlockSpec((tm, LANES), lambda i, t: (t, 0))
    grow = pl.BlockSpec((1, tm // CHUNK, N_GATE, CHUNK), lambda i, t: (i, t, 0, 0))
    outs = [
        (jax.ShapeDtypeStruct((b, ltot, MLA_HEADS * HEAD_PAD), BF16), tok(MLA_HEADS * HEAD_PAD)),
        (jax.ShapeDtypeStruct((b, ltot, MLA_HEADS * HEAD_PAD), BF16), tok(MLA_HEADS * HEAD_PAD)),
        (jax.ShapeDtypeStruct((b, ltot, MLA_HEADS * V_HEAD), BF16), tok(MLA_HEADS * V_HEAD)),
        (jax.ShapeDtypeStruct((b, ltot, CONV_CH), F32), tok(CONV_CH)),
        (jax.ShapeDtypeStruct((b, ltot, ML_WIDTH), BF16), tok(ML_WIDTH)),
        (jax.ShapeDtypeStruct((b, nc, ML_WIDTH, CHUNK), BF16),
         pl.BlockSpec((1, tm // CHUNK, ML_WIDTH, CHUNK), lambda i, t: (i, t, 0, 0))),
        (jax.ShapeDtypeStruct((b, ltot, ML_WIDTH), BF16), tok(ML_WIDTH)),
        (jax.ShapeDtypeStruct((b, ltot, ML_WIDTH), F32), tok(ML_WIDTH)),
        (jax.ShapeDtypeStruct((b, nc, N_GATE, CHUNK), F32), grow),
        (jax.ShapeDtypeStruct((b, nc, N_GATE, CHUNK), F32), grow),
    ]
    if split:
        ppt = tm // TM
        x_args = [ctx] + [x] * ppt
        x_specs = [pl.BlockSpec((1, TM, d), lambda i, t: (i, 0, 0))] + [
            pl.BlockSpec((1, TM, d), lambda i, t, j=j: (i, jnp.maximum(t * ppt + j - 1, 0), 0)) for j in range(ppt)]
    else:
        x_args, x_specs = [stream], [tok(d)]
    return pl.pallas_call(
        functools.partial(_in_kernel, n_x=len(x_args)),
        out_shape=[o[0] for o in outs],
        grid=(b, nt),
        in_specs=x_specs + [
            pl.BlockSpec((1, 6, d), lambda i, t: (i, 0, 0)),
            pl.BlockSpec((1, 6, d), lambda i, t: (ctx_row, 0, 0)),
            const2(w_in_p.shape),
            const2(gq.shape), const2(gkv.shape),
            const2(wuq.shape), const2(wuk.shape), const2(wuv.shape),
            const2(bg.shape),
            tab, tab, tab,
        ],
        out_specs=[o[1] for o in outs],
        compiler_params=pltpu.CompilerParams(dimension_semantics=("parallel", "parallel"),
                                             vmem_limit_bytes=VMEM_LIMIT_BYTES),
        name="in_proj",
    )(*x_args, mod, mod, w_in_p, gq, gkv, wuq, wuk, wuv, bg, cos, slo, shi)


ATT_PAIRS = 4


def _attn_kernel(*refs, with_ctx, ltot):
    qa_ref, qb_ref, k_ref, v_ref = refs[:4]
    if with_ctx:
        octx_ref, o_ref, vaug_ref = refs[4:]
    else:
        o_ref, vaug_ref = refs[4:]
    pair_w = 2 * V_HEAD
    nh = 2 * ATT_PAIRS

    @pl.when(pl.program_id(2) == 0)
    def _():
        ones = jnp.ones((ltot, pair_w), BF16)
        for pp in range(ATT_PAIRS):
            vaug_ref[pp] = jnp.concatenate([v_ref[0, :, pp * pair_w:(pp + 1) * pair_w], ones], axis=1)

    def run(q_refs, nk, out_ref):
        chains = [(q_ref, h) for q_ref in q_refs for h in range(nh)]

        def scores(chain):
            q_ref, h = chain
            sl = slice(h * HEAD_PAD, (h + 1) * HEAD_PAD)
            return lax.dot_general(q_ref[0, :, sl], k_ref[0, :nk, sl], (((1,), (1,)), ((), ())),
                                   preferred_element_type=F32)

        def probs(s):
            return jnp.exp2((s - jnp.max(s, -1, keepdims=True)).astype(BF16))

        def weighted(chain, p):
            o = jnp.dot(p, vaug_ref[chain[1] // 2, :nk, :], preferred_element_type=F32)
            return o[:, :pair_w] / o[:, pair_w:]

        s_next = scores(chains[0])
        p_prev = None
        outs = []
        for c, chain in enumerate(chains):
            s_cur = s_next
            if c + 1 < len(chains):
                s_next = scores(chains[c + 1])
            p_cur = probs(s_cur)
            if p_prev is not None:
                outs.append(weighted(chains[c - 1], p_prev))
            p_prev = p_cur
        outs.append(weighted(chains[-1], p_prev))
        lane = lax.broadcasted_iota(jnp.int32, outs[0].shape, 1)
        for qi in range(len(q_refs)):
            for pp in range(ATT_PAIRS):
                pair = jnp.where(lane < V_HEAD, outs[qi * nh + 2 * pp], outs[qi * nh + 2 * pp + 1])
                out_ref[0, qi * TM:(qi + 1) * TM, pp * pair_w:(pp + 1) * pair_w] = pair.astype(out_ref.dtype)

    if with_ctx:
        t = pl.program_id(2)

        @pl.when(t == 0)
        def _():
            run([qa_ref], CTX_LEN, octx_ref)

        @pl.when(t > 0)
        def _():
            run([qa_ref, qb_ref], ltot, o_ref)
    else:
        run([qa_ref, qb_ref], ltot, o_ref)


def _attn_call(q, k, v, with_ctx):
    b, ltot, _ = q.shape
    ctx_tiles = CTX_LEN // TM
    seq = ltot - CTX_LEN
    assert ctx_tiles == 1 and seq % (2 * TM) == 0
    lead = 1 if with_ctx else 0
    nt = seq // (2 * TM) + lead
    ngroup = MLA_HEADS // (2 * ATT_PAIRS)
    qk_w = 2 * ATT_PAIRS * HEAD_PAD
    v_w = 2 * ATT_PAIRS * V_HEAD

    def q_tile(which):
        return lambda i, j, t: (i, jnp.where(t < lead, which, 2 * (t - lead) + ctx_tiles + which), j)

    lat_shape = jax.ShapeDtypeStruct((b, seq, MLA_HEADS * V_HEAD), BF16)
    lat_spec = pl.BlockSpec((1, 2 * TM, v_w), lambda i, j, t: (i, jnp.maximum(t - lead, 0), j))
    if with_ctx:
        out_shape = [jax.ShapeDtypeStruct((b, CTX_LEN, MLA_HEADS * V_HEAD), BF16), lat_shape]
        out_specs = [pl.BlockSpec((1, TM, v_w), lambda i, j, t: (i, 0, j)), lat_spec]
    else:
        out_shape, out_specs = lat_shape, lat_spec
    return pl.pallas_call(
        functools.partial(_attn_kernel, with_ctx=with_ctx, ltot=ltot),
        out_shape=out_shape,
        grid=(b, ngroup, nt),
        in_specs=[
            pl.BlockSpec((1, TM, qk_w), q_tile(0)),
            pl.BlockSpec((1, TM, qk_w), q_tile(1)),
            pl.BlockSpec((1, ltot, qk_w), lambda i, j, t: (i, 0, j)),
            pl.BlockSpec((1, ltot, v_w), lambda i, j, t: (i, 0, j)),
        ],
        out_specs=out_specs,
        scratch_shapes=[pltpu.VMEM((ATT_PAIRS, ltot, 4 * V_HEAD), BF16)],
        compiler_params=pltpu.CompilerParams(dimension_semantics=("parallel", "parallel", "arbitrary"),
                                             vmem_limit_bytes=VMEM_LIMIT_BYTES),
        name="mla_attn",
    )(q, q, k, v)


def _mlstm_kernel(q_ref, kt_ref, v_ref, mo_ref, gi_ref, gf_ref, g_ref,
                  o_ref, hf_ref, hb_ref, sn_ref, a_ref, r_ref, e_ref, tot_ref, gmax_ref, *, nchunk, nctx):
    t_ = CHUNK
    npair = ML_HEADS // 2
    row = lax.broadcasted_iota(jnp.int32, (t_, t_), 0)
    col = lax.broadcasted_iota(jnp.int32, (t_, t_), 1)
    causal_masks = (col <= row, col >= row)
    blockdiag = (row < ML_HEAD_DIM) == (col < ML_HEAD_DIM)
    half0 = lax.broadcasted_iota(jnp.int32, (1, LANES), 1) < ML_HEAD_DIM
    rowhalf0 = lax.broadcasted_iota(jnp.int32, (LANES, 1), 0) < ML_HEAD_DIM
    lane8 = lax.broadcasted_iota(jnp.int32, (1, t_), 1)
    n_parts = 3
    zpad = jnp.zeros((t_ - 2 * n_parts * N_GATE, t_), F32)
    ones_b = jnp.ones((t_, LANES), BF16)
    zeros_b = jnp.zeros((t_, LANES), BF16)
    half_ones = tuple(jnp.broadcast_to(jnp.where(half0, on, 1.0 - on), (t_, LANES)).astype(BF16) for on in (1.0, 0.0))

    r2 = lax.broadcasted_iota(jnp.int32, (t_, 3 * LANES), 0)
    l2 = lax.broadcasted_iota(jnp.int32, (t_, 3 * LANES), 1)
    blk, gate = r2 >> 3, r2 & 7
    lblk, lhalf = l2 >> 7, (l2 >> 6) & 1
    is_r_part = blk < n_parts
    is_b_part = jnp.logical_and(blk >= n_parts, blk < 2 * n_parts)
    consts = {}
    for d in range(2):
        for p in range(npair):
            c0 = d * ML_HEADS + 2 * p
            want_gate = jnp.where(lblk == 2, c0 + lhalf, c0 + lblk)
            want_part = jnp.where(lblk == 2, is_b_part.astype(jnp.int32), is_r_part.astype(jnp.int32)) == 1
            consts[d, p] = jnp.where(jnp.logical_and(want_part, gate == want_gate), 1.0, 0.0).astype(BF16)

    def split3(x):
        hi = x.astype(BF16).astype(F32)
        rem = x - hi
        mid = rem.astype(BF16).astype(F32)
        return [hi, mid, rem - mid]

    def lane_scan(x, op, fill, reverse):
        k = 1
        while k < t_:
            if reverse:
                shifted, valid = pltpu.roll(x, t_ - k, 1), lane8 < t_ - k
            else:
                shifted, valid = pltpu.roll(x, k, 1), lane8 >= k
            x = op(x, jnp.where(valid, shifted, fill))
            k *= 2
        return x

    nrow = nchunk * N_GATE
    is_fwd_row = (lax.broadcasted_iota(jnp.int32, (nrow, t_), 0) & (N_GATE - 1)) < ML_HEADS
    lf = gf_ref[0].reshape(nrow, t_) * LOG2_E
    lf_parts = jnp.concatenate([part.astype(BF16) for part in split3(lf)], axis=1)
    tri_up = jnp.where(row <= col, 1.0, 0.0).astype(BF16)
    tri_dn = jnp.where(row >= col, 1.0, 0.0).astype(BF16)
    b = jnp.where(is_fwd_row,
                  jnp.dot(lf_parts, jnp.concatenate([tri_up] * n_parts, axis=0), preferred_element_type=F32),
                  jnp.dot(lf_parts, jnp.concatenate([tri_dn] * n_parts, axis=0), preferred_element_type=F32))
    tot = jnp.sum(lf, axis=1, keepdims=True)
    r = gi_ref[0].reshape(nrow, t_) * LOG2_E - b
    rmax = jnp.where(is_fwd_row, lane_scan(r, jnp.maximum, -jnp.inf, False),
                     lane_scan(r, jnp.maximum, -jnp.inf, True))
    rlast = jnp.max(r, axis=1, keepdims=True)
    col_parts = split3(rmax) + split3(b)
    for c in range(nchunk):
        rows_c = slice(c * N_GATE, (c + 1) * N_GATE)
        packed = jnp.concatenate([part[rows_c] for part in col_parts] + [zpad], axis=0)
        a_ref[c] = packed.T.astype(BF16)
    r_ref[...] = r
    e_ref[...] = jnp.exp2(r - rlast)
    tot_ref[...] = jnp.broadcast_to(tot, (nrow, t_))
    gmax_ref[...] = jnp.broadcast_to(tot + rlast, (nrow, t_))

    def prep(d, c, m_prev):
        rows_c = pl.ds(pl.multiple_of(c * N_GATE, N_GATE), N_GATE)
        tot_c = tot_ref[rows_c, :]
        gmax_c = gmax_ref[rows_c, :]
        m_new = jnp.maximum(tot_c + m_prev, gmax_c)
        decay = jnp.exp2(tot_c + m_prev - m_new)
        gamma = jnp.exp2(gmax_c - m_new)
        return a_ref[c], r_ref[rows_c, :], e_ref[rows_c, :], decay, gamma, m_new

    def stage_mxu(d, p, c, pre):
        a, _, e_row, _, _, _ = pre
        c0 = d * ML_HEADS + 2 * p
        c1 = c0 + 1
        sl = slice(p * LANES, (p + 1) * LANES)
        q = q_ref[0, c, :, sl]
        kt = kt_ref[0, c, sl, :]
        v = v_ref[0, c, :, sl]
        z = jnp.dot(a, consts[d, p], preferred_element_type=F32)
        kt_heads = jnp.concatenate([jnp.where(rowhalf0, kt, zeros_b), jnp.where(rowhalf0, zeros_b, kt)], axis=1)
        qk = jnp.dot(q, kt_heads, preferred_element_type=F32)
        wkt = kt * jnp.where(rowhalf0, e_row[c0:c0 + 1, :], e_row[c1:c1 + 1, :]).astype(BF16)
        upd = jnp.dot(wkt, jnp.concatenate([v, ones_b], axis=1), preferred_element_type=F32)
        sn = sn_ref[2 * p + d]
        qs = jnp.dot(q, sn.astype(BF16), preferred_element_type=F32)
        return z, qk, upd, sn, qs, v

    def stage_intra(d, p, r_rows, z, qk, v):
        c0 = d * ML_HEADS + 2 * p
        nd = jnp.zeros((t_, 2 * LANES), F32)
        for hh in range(2):
            hs = slice(hh * LANES, (hh + 1) * LANES)
            wts = jnp.exp2(jnp.where(causal_masks[d], r_rows[c0 + hh:c0 + hh + 1, :] - z[:, hs], -jnp.inf))
            v_h = jnp.where(half0, v, zeros_b) if hh == 0 else jnp.where(half0, zeros_b, v)
            nd = nd + jnp.dot((qk[:, hs] * wts).astype(BF16), jnp.concatenate([v_h, half_ones[hh]], axis=1),
                              preferred_element_type=F32)
        return nd

    def stage_out(d, p, c, pre, m_prev, h_ref, z, upd, sn, qs, nd):
        _, _, _, decay, gamma, _ = pre
        c0 = d * ML_HEADS + 2 * p
        c1 = c0 + 1
        sl = slice(p * LANES, (p + 1) * LANES)
        r_bc = jnp.where(half0, z[:, :LANES], z[:, LANES:2 * LANES])
        b_bc = z[:, 2 * LANES:]
        m_bc = jnp.where(half0, m_prev[c0:c0 + 1, :], m_prev[c1:c1 + 1, :])
        p_bc = jnp.maximum(m_bc, r_bc)
        alpha = jnp.exp2(r_bc - p_bc)
        beta = jnp.exp2(m_bc - p_bc)
        num = alpha * nd[:, :LANES] + beta * qs[:, :LANES]
        den = alpha * nd[:, LANES:] + beta * qs[:, LANES:]
        h_ref[c, :, sl] = num / jnp.maximum(jnp.abs(den), jnp.exp2(-(b_bc + p_bc)))

        dec_rows = jnp.where(rowhalf0, decay[c0:c0 + 1, :], decay[c1:c1 + 1, :])
        gam_rows = jnp.where(blockdiag, jnp.where(rowhalf0, gamma[c0:c0 + 1, :], gamma[c1:c1 + 1, :]), 0.0)
        sn_ref[2 * p + d] = jnp.concatenate([dec_rows * sn[:, :LANES] + gam_rows * upd[:, :LANES],
                                             dec_rows * sn[:, LANES:] + gam_rows * upd[:, LANES:]], axis=1)

    sn_ref[...] = jnp.zeros(sn_ref.shape, F32)

    def body(i, carry):
        cb = jnp.where(i < nctx, nctx - 1 - i, nchunk - 1 + nctx - i)
        streams = [(d, p, c, h_ref) for d, c, h_ref in ((0, i, hf_ref), (1, cb, hb_ref)) for p in range(npair)]
        pres = {0: prep(0, i, carry[0]), 1: prep(1, cb, carry[1])}
        first = [stage_mxu(d, p, c, pres[d]) for d, p, c, _ in streams]
        intra = [stage_intra(d, p, pres[d][1], z, qk, v)
                 for (d, p, _, _), (z, qk, _, _, _, v) in zip(streams, first)]
        for (d, p, c, h_ref), (z, _, upd, sn, qs, _), nd in zip(streams, first, intra):
            stage_out(d, p, c, pres[d], carry[d], h_ref, z, upd, sn, qs, nd)
        return pres[0][-1], pres[1][-1]

    m_init = jnp.zeros((N_GATE, t_), F32)
    lax.fori_loop(0, nchunk, body, (m_init, m_init), unroll=9)

    g = g_ref[...]
    head_shift = ML_HEAD_DIM.bit_length() - 1
    head_of_row = lax.broadcasted_iota(jnp.int32, (ML_WIDTH, ML_WIDTH), 0) >> head_shift
    head_of_col = lax.broadcasted_iota(jnp.int32, (ML_WIDTH, ML_WIDTH), 1) >> head_shift
    avg = jnp.where(head_of_row == head_of_col, 1.0 / ML_HEAD_DIM, 0.0).astype(BF16)
    avg2 = jnp.concatenate([avg, avg], axis=0)

    def head_mean(x):
        hi = x.astype(BF16)
        lo = (x - hi.astype(F32)).astype(BF16)
        return jnp.dot(jnp.concatenate([hi, lo], axis=1), avg2, preferred_element_type=F32)

    def fin(c, _):
        hh = (hf_ref[c] + hb_ref[c]) * _sigmoid(mo_ref[0, c])
        dl = hh - head_mean(hh)
        var = head_mean(dl * dl)
        o_ref[0, c] = (dl * lax.rsqrt(var + LN_EPS) * g).astype(o_ref.dtype)
        return 0

    lax.fori_loop(0, nchunk, fin, 0, unroll=6)


def _mlstm_call(mq, mkt, mv, mo, gir, gfr, g):
    b, ltot, w = mq.shape
    nc = ltot // CHUNK
    r4 = lambda a: a.reshape(b, nc, CHUNK, a.shape[-1])
    blk = lambda rows, width: pl.BlockSpec((1, nc, rows, width), lambda i: (i, 0, 0, 0))
    out = pl.pallas_call(
        functools.partial(_mlstm_kernel, nchunk=nc, nctx=CTX_LEN // CHUNK),
        out_shape=jax.ShapeDtypeStruct((b, nc, CHUNK, w), BF16),
        grid=(b,),
        in_specs=[blk(CHUNK, w), blk(w, CHUNK), blk(CHUNK, w), blk(CHUNK, w),
                  blk(N_GATE, CHUNK), blk(N_GATE, CHUNK),
                  pl.BlockSpec((1, w), lambda i: (0, 0))],
        out_specs=blk(CHUNK, w),
        scratch_shapes=[pltpu.VMEM((nc, CHUNK, w), F32), pltpu.VMEM((nc, CHUNK, w), F32),
                        pltpu.VMEM((2 * (ML_HEADS // 2), LANES, 2 * LANES), F32),
                        pltpu.VMEM((nc, CHUNK, LANES), BF16), pltpu.VMEM((nc * N_GATE, CHUNK), F32),
                        pltpu.VMEM((nc * N_GATE, CHUNK), F32), pltpu.VMEM((nc * N_GATE, CHUNK), F32),
                        pltpu.VMEM((nc * N_GATE, CHUNK), F32)],
        compiler_params=pltpu.CompilerParams(dimension_semantics=("parallel",),
                                             vmem_limit_bytes=VMEM_LIMIT_BYTES),
        name="mlstm",
    )(r4(mq), mkt, r4(mv), r4(mo), gir, gfr, g)
    return out.reshape(b, ltot, w)


def _post_kernel(*refs, t0, nt_all, split):
    if split:
        ctx_ref, x_ref, mod_ref, attc_ref, attl_ref = refs[:5]
        is_ctx = pl.program_id(1) == 0
        resid = jnp.where(is_ctx, ctx_ref[0], x_ref[0])
        att = jnp.where(is_ctx, attc_ref[0], attl_ref[0])
    else:
        x_ref, mod_ref, att_ref = refs[:3]
        resid = x_ref[0]
        att = att_ref[0]
    (z_ref, zp_ref, zn_ref, ml_ref, cw_ref, cb_ref, cg_ref, cbb_ref,
     wo_ref, g1_ref, b1_ref, o_ref, zbuf_ref, zsh_ref) = refs[5 if split else 3:]
    t = pl.program_id(1) + t0
    n_att = MLA_HEADS * V_HEAD
    y = jnp.dot(att, wo_ref[0:n_att, :], preferred_element_type=F32)
    y = y + jnp.dot(ml_ref[0], wo_ref[n_att + CONV_CH:, :], preferred_element_type=F32)
    left_ok = t >= 2
    right_ok = jnp.logical_and(t >= 1, t <= nt_all - 2)
    zbuf_ref[0:HALO, :] = jnp.where(left_ok, zp_ref[0], 0.0)
    zbuf_ref[HALO:HALO + TM, :] = z_ref[0]
    zbuf_ref[HALO + TM:, :] = jnp.where(right_ok, zn_ref[0], 0.0)
    off = HALO - CONV_WIDTH // 2
    span = TM + ((off + CONV_WIDTH - 1) // SUBLANES) * SUBLANES
    for s in range(SUBLANES):
        zsh_ref[s] = zbuf_ref[s:s + span, :]
    acc = jnp.zeros((TM, CONV_CH), F32)
    for j in range(CONV_WIDTH):
        s, a = (off + j) % SUBLANES, ((off + j) // SUBLANES) * SUBLANES
        acc = acc + cw_ref[j:j + 1, :] * zsh_ref[s, a:a + TM, :]
    cv = _layer_norm(acc + cb_ref[...], cg_ref[...], cbb_ref[...])
    cv = cv * _sigmoid(cv)

    y = y + jnp.dot(cv.astype(BF16), wo_ref[n_att:n_att + CONV_CH, :], preferred_element_type=F32)
    g1 = mod_ref[0, 2:3, :]
    o_ref[0] = _layer_norm(DEEPNORM_ALPHA * resid + g1 * y, g1_ref[...], b1_ref[...])


def _post_call(stream, mod, att, z, ml, cw, cb, cg, cbb, wo, g1, b1, t0):
    split = isinstance(stream, tuple)
    if split:
        ctx, x = stream
        b, seq, d = x.shape
        ltot = CTX_LEN + seq
    else:
        b, ltot, d = stream.shape
    nt_all = ltot // TM
    nt = nt_all - t0
    ctx_row = mod.shape[0] - SUBLANES
    hpt = TM // HALO
    nhalo = ltot // HALO

    def tok(width, off):
        return pl.BlockSpec((1, TM, width), lambda i, t: (i, t + off, 0))

    def const2(shape):
        return pl.BlockSpec(shape, lambda i, t: (0, 0))

    def pieces(width):
        return [pl.BlockSpec((1, TM, width), lambda i, t: (i, 0, 0)),
                pl.BlockSpec((1, TM, width), lambda i, t: (i, jnp.maximum(t - 1, 0), 0))]

    if split:
        assert t0 == 0 and isinstance(att, (tuple, list))
        x_args, x_specs = [ctx, x], pieces(d)
        att_args, att_specs = list(att), pieces(att[0].shape[-1])
    else:
        x_args, x_specs = [stream], [tok(d, t0)]
        att_args, att_specs = [att], [tok(att.shape[-1], 0)]
    return pl.pallas_call(
        functools.partial(_post_kernel, t0=t0, nt_all=nt_all, split=split),
        out_shape=jax.ShapeDtypeStruct((b, nt * TM, d), F32),
        grid=(b, nt),
        in_specs=x_specs + [
            pl.BlockSpec((1, 6, d), lambda i, t: (jnp.where(t + t0 == 0, ctx_row, i), 0, 0)),
        ] + att_specs + [
            tok(CONV_CH, t0),
            pl.BlockSpec((1, HALO, CONV_CH), lambda i, t: (i, jnp.maximum((t + t0) * hpt - 1, 0), 0)),
            pl.BlockSpec((1, HALO, CONV_CH), lambda i, t: (i, jnp.minimum((t + t0 + 1) * hpt, nhalo - 1), 0)),
            tok(ML_WIDTH, t0),
            const2(cw.shape), const2(cb.shape), const2(cg.shape), const2(cbb.shape),
            const2(wo.shape), const2(g1.shape), const2(b1.shape),
        ],
        out_specs=tok(d, 0),
        scratch_shapes=[pltpu.VMEM((TM + 2 * HALO, CONV_CH), F32),
                        pltpu.VMEM((SUBLANES, TM + 2 * HALO - SUBLANES, CONV_CH), F32)],
        compiler_params=pltpu.CompilerParams(dimension_semantics=("parallel", "parallel"),
                                             vmem_limit_bytes=VMEM_LIMIT_BYTES),
        name="out_proj",
    )(*x_args, mod, *att_args, z, z, z, ml, cw, cb, cg, cbb, wo, g1, b1)


FF_CHUNK = 1024


def _mlp_kernel(x_ref, mod_ref, modc_ref, w1_ref, b1_ref, w2_ref, b2_ref, g_ref, b_ref, o_ref, *, ctx_rows):
    x = x_ref[0]
    sh2 = _mod_rows(mod_ref, modc_ref, 3, x.shape[0], ctx_rows)
    sc2 = _mod_rows(mod_ref, modc_ref, 4, x.shape[0], ctx_rows)
    g2 = _mod_rows(mod_ref, modc_ref, 5, x.shape[0], ctx_rows)
    u = (x * (1.0 + sc2) + sh2).astype(BF16)
    def hidden(c):
        sl = slice(c * FF_CHUNK, (c + 1) * FF_CHUNK)
        return jnp.dot(u, w1_ref[:, sl], preferred_element_type=F32) + b1_ref[:, sl]

    acc = jnp.zeros(x.shape, F32)
    n_chunk = D_FF // FF_CHUNK
    pre = hidden(0)
    for c in range(n_chunk):
        cur = pre
        if c + 1 < n_chunk:
            pre = hidden(c + 1)
        h = jnp.maximum(cur, 0.0)
        acc = acc + jnp.dot((h * h).astype(BF16), w2_ref[c * FF_CHUNK:(c + 1) * FF_CHUNK, :],
                            preferred_element_type=F32)
    o_ref[0] = _layer_norm(DEEPNORM_ALPHA * x + g2 * (acc + b2_ref[...]), g_ref[...], b_ref[...])


def _mlp_call(x1, mod, w1, b1, w2, b2, g, bb, with_ctx):
    b, ln, d = x1.shape
    tm = _wide_tile(ln)
    nt = ln // tm
    ctx_row = mod.shape[0] - SUBLANES

    def const2(shape):
        return pl.BlockSpec(shape, lambda i, t: (0, 0), pipeline_mode=pl.Buffered(1))

    return pl.pallas_call(
        functools.partial(_mlp_kernel, ctx_rows=CTX_LEN if with_ctx else 0),
        out_shape=jax.ShapeDtypeStruct((b, ln, d), F32),
        grid=(b, nt),
        in_specs=[
            pl.BlockSpec((1, tm, d), lambda i, t: (i, t, 0)),
            pl.BlockSpec((1, 6, d), lambda i, t: (i, 0, 0)),
            pl.BlockSpec((1, 6, d), lambda i, t: (ctx_row, 0, 0)),
            const2(w1.shape), const2(b1.shape), const2(w2.shape), const2(b2.shape),
            const2(g.shape), const2(bb.shape),
        ],
        out_specs=pl.BlockSpec((1, tm, d), lambda i, t: (i, t, 0)),
        compiler_params=pltpu.CompilerParams(dimension_semantics=("parallel", "parallel"),
                                             vmem_limit_bytes=VMEM_LIMIT_BYTES),
        name="mlp",
    )(x1, mod, mod, w1, b1, w2, b2, g, bb)


def _rope_tables(seq):
    half = QK_ROPE // 2
    nf = half // 2
    pos = np.arange(seq)
    inv_freq = ROPE_THETA ** (-np.arange(nf, dtype=np.float32) / nf)
    cos = np.ones((CTX_LEN + seq, LANES), np.float32)
    slo = np.zeros((CTX_LEN + seq, LANES), np.float32)
    shi = np.zeros((CTX_LEN + seq, LANES), np.float32)
    for part, p in enumerate((pos // GRID_W, pos % GRID_W)):
        ang = p.astype(np.float32)[:, None] * inv_freq[None, :]
        c, s = np.cos(ang), np.sin(ang)
        base = QK_NOPE + part * half
        cos[CTX_LEN:, base:base + nf] = c
        cos[CTX_LEN:, base + nf:base + half] = c
        slo[CTX_LEN:, base:base + nf] = -s
        shi[CTX_LEN:, base + nf:base + half] = s
    return jnp.asarray(cos), jnp.asarray(slo), jnp.asarray(shi)


def _pad_in_weights(w_in):
    d = w_in.shape[0]
    offs = np.cumsum([0, Q_LORA, KV_LORA, QK_ROPE, 2 * CONV_CH, ML_WIDTH, ML_WIDTH, ML_WIDTH, ML_WIDTH])
    o_cq, o_ckv, o_kr, o_conv, o_mq, o_mk, o_mv, o_mo, o_mg = [int(o) for o in offs]
    z = lambda n: jnp.zeros((d, n), w_in.dtype)
    mg = w_in[:, o_mg:o_mg + 4 * ML_HEADS]
    h = ML_HEADS
    gi = jnp.concatenate([mg[:, 0:h], mg[:, 2 * h:3 * h]], 1)
    gf = jnp.concatenate([mg[:, h:2 * h], mg[:, 3 * h:4 * h]], 1)
    cols = [
        w_in[:, o_cq:o_ckv], w_in[:, o_ckv:o_kr],
        gi, gf, z(QK_NOPE - 2 * N_GATE), w_in[:, o_kr:o_conv], z(LANES - QK_NOPE - QK_ROPE),
        w_in[:, o_conv:o_mg],
    ]
    return jnp.concatenate(cols, 1).astype(BF16)


def _pad_gate_bias(b_gates):
    h = ML_HEADS
    pad = jnp.zeros((LANES - 2 * N_GATE,), b_gates.dtype)
    return jnp.concatenate([b_gates[0:h], b_gates[2 * h:3 * h], b_gates[h:2 * h], b_gates[3 * h:4 * h], pad])[None, :]


def _pad_mla_weights(w_uq, w_ukv):
    dq = QK_NOPE + QK_ROPE
    wq = w_uq.reshape(Q_LORA, MLA_HEADS, dq)
    wq = jnp.pad(wq, ((0, 0), (0, 0), (0, HEAD_PAD - dq))).reshape(Q_LORA, MLA_HEADS * HEAD_PAD)
    wkv = w_ukv.reshape(KV_LORA, MLA_HEADS, QK_NOPE + V_HEAD)
    wk = jnp.pad(wkv[:, :, :QK_NOPE], ((0, 0), (0, 0), (0, HEAD_PAD - QK_NOPE)))
    wk = wk.reshape(KV_LORA, MLA_HEADS * HEAD_PAD)
    wv = wkv[:, :, QK_NOPE:].reshape(KV_LORA, MLA_HEADS * V_HEAD)
    return wq.astype(BF16), wk.astype(BF16), wv.astype(BF16)


def kernel(x, c, ctx, c_ctx, w_ada, b_ada, w_in, g_qn, w_uq, g_kvn, w_ukv, conv_w, conv_b, conv_ln_g, conv_ln_b, b_gates, ml_norm_g, w_out, ln1_g, ln1_b, w_mlp1, b_mlp1, w_mlp2, b_mlp2, ln2_g, ln2_b):
    b, seq, d = x.shape
    depth = w_in.shape[0]
    assert ctx.shape[1] == CTX_LEN == TM and seq % TM == 0 and d == D_MODEL and depth == DEPTH
    row = lambda a: a[None, :]

    cc = jnp.concatenate([c, c_ctx[None, :], jnp.zeros((SUBLANES - 1, d), c.dtype)], 0)
    mod_all = _ada_call(cc, w_ada, b_ada).reshape(depth, cc.shape[0], 6, d)
    cos, slo, shi = _rope_tables(seq)

    xx = (ctx, x)
    for l in range(depth):
        last = l == depth - 1
        t0 = 1 if last else 0
        mod = mod_all[l]
        wq, wk, wv = _pad_mla_weights(w_uq[l], w_ukv[l])
        q, k, v, z, mq, mkt, mv, mo, gir, gfr = _in_call(
            xx, mod, _pad_in_weights(w_in[l]), row(g_qn[l]), row(g_kvn[l]), wq, wk, wv,
            _pad_gate_bias(b_gates[l]), cos, slo, shi)
        att = _attn_call(q, k, v, with_ctx=not last)
        ml = _mlstm_call(mq, mkt, mv, mo, gir, gfr, row(ml_norm_g[l]))
        x1 = _post_call(xx, mod, att, z, ml, conv_w[l], row(conv_b[l]), row(conv_ln_g[l]), row(conv_ln_b[l]),
                        w_out[l].astype(BF16), row(ln1_g[l]), row(ln1_b[l]), t0)
        xx = _mlp_call(x1, mod, w_mlp1[l].astype(BF16), row(b_mlp1[l]), w_mlp2[l].astype(BF16),
                       row(b_mlp2[l]), row(ln2_g[l]), row(ln2_b[l]), with_ctx=not last)
    return xx
```

```python
import functools

import numpy as np
import jax
import jax.numpy as jnp
from jax import lax
from jax.experimental import pallas as pl
from jax.experimental.pallas import tpu as pltpu

F32 = jnp.float32
BF16 = jnp.bfloat16

LANES = 128
SUBLANES = 8
VMEM_LIMIT_BYTES = 56 * 1024 * 1024

D_MODEL = 1024
GRID_W = 64
CTX_LEN = 256
MLA_HEADS = 8
QK_NOPE = 64
QK_ROPE = 32
V_HEAD = 64
Q_LORA = 256
KV_LORA = 128
CONV_CH = 256
CONV_WIDTH = 31
ML_HEADS = 4
ML_HEAD_DIM = 64
ML_WIDTH = ML_HEADS * ML_HEAD_DIM
CHUNK = 128
D_FF = 4 * D_MODEL
ROPE_THETA = 10000.0
LN_EPS = 1e-5
RMS_EPS = 1e-6
DEPTH = 2
DEEPNORM_ALPHA = (2 * DEPTH) ** 0.25

TM = 256
HALO = 16
HEAD_PAD = LANES

C_CQ = 0
C_CKV = C_CQ + Q_LORA
C_KRG = C_CKV + KV_LORA
C_CONV = C_KRG + LANES
C_MQ = C_CONV + 2 * CONV_CH
C_MK = C_MQ + ML_WIDTH
C_MV = C_MK + ML_WIDTH
C_MO = C_MV + ML_WIDTH
N_GATE = 2 * ML_HEADS
ROPE_PAIR_DIST = QK_ROPE // 4
WIDE_TILES = (768, 512)
LOG2_E = float(np.log2(np.e))
Q_SCALE = (QK_NOPE + QK_ROPE) ** -0.5 * LOG2_E


def _wide_tile(rows):
    for tile in WIDE_TILES + (TM,):
        if rows % tile == 0:
            return tile
    raise ValueError(f"no token tile divides {rows}")


def _layer_norm(v, g, b):
    mu = jnp.mean(v, -1, keepdims=True)
    d = v - mu
    var = jnp.mean(d * d, -1, keepdims=True)
    return d * lax.rsqrt(var + LN_EPS) * g + b


def _rms_norm(v, g):
    return v * lax.rsqrt(jnp.mean(v * v, -1, keepdims=True) + RMS_EPS) * g


def _sigmoid(v):
    return 1.0 / (1.0 + jnp.exp(-v))


def _log_sigmoid(v):
    return jnp.minimum(v, 0.0) - jnp.log1p(jnp.exp(-jnp.abs(v)))


def _rope(t, cos, sin_lo, sin_hi):
    return (t * cos + pltpu.roll(t, LANES - ROPE_PAIR_DIST, 1) * sin_lo
            + pltpu.roll(t, ROPE_PAIR_DIST, 1) * sin_hi)


def _ada_kernel(c_ref, w_ref, b_ref, o_ref):
    c = c_ref[...]
    s = (c * _sigmoid(c)).astype(BF16)
    o_ref[0] = jnp.dot(s, w_ref[0].astype(BF16), preferred_element_type=F32) + b_ref[0]


def _ada_call(cc, w_ada, b_ada):
    depth, d, n6 = w_ada.shape
    rows = cc.shape[0]
    nblk = n6 // d
    return pl.pallas_call(
        _ada_kernel,
        out_shape=jax.ShapeDtypeStruct((depth, rows, n6), F32),
        grid=(depth, nblk),
        in_specs=[
            pl.BlockSpec((rows, d), lambda l, n: (0, 0)),
            pl.BlockSpec((1, d, d), lambda l, n: (l, 0, n)),
            pl.BlockSpec((1, 1, d), lambda l, n: (l, 0, n)),
        ],
        out_specs=pl.BlockSpec((1, rows, d), lambda l, n: (l, 0, n)),
        compiler_params=pltpu.CompilerParams(dimension_semantics=("parallel", "parallel")),
        name="ada_mod",
    )(cc, w_ada, b_ada.reshape(depth, 1, n6))


def _mod_rows(mod_ref, modc_ref, idx, rows, ctx_rows):
    m = mod_ref[0, idx:idx + 1, :]
    if ctx_rows == 0:
        return m
    is_ctx = jnp.logical_and(lax.broadcasted_iota(jnp.int32, (rows, 1), 0) < ctx_rows, pl.program_id(1) == 0)
    return jnp.where(is_ctx, modc_ref[0, idx:idx + 1, :], m)


def _in_kernel(*refs, n_x):
    x_refs = refs[:n_x]
    (mod_ref, modc_ref, w_ref, gq_ref, gkv_ref, wuq_ref, wuk_ref, wuv_ref, bg_ref, cos_ref, slo_ref, shi_ref,
     q_ref, k_ref, v_ref, z_ref, mq_ref, mkt_ref, mv_ref, mo_ref, gir_ref, gfr_ref) = refs[n_x:]
    if n_x == 1:
        x = x_refs[0][0]
        tm = x.shape[0]
        sh1 = _mod_rows(mod_ref, modc_ref, 0, tm, CTX_LEN)
        sc1 = _mod_rows(mod_ref, modc_ref, 1, tm, CTX_LEN)
        xm = (x * (1.0 + sc1) + sh1).astype(BF16)
    else:
        ctx_ref, first_ref = x_refs[0], x_refs[1]
        is_ctx = pl.program_id(1) == 0
        pieces = []
        for j, piece_ref in enumerate(x_refs[1:]):
            sh1, sc1 = mod_ref[0, 0:1, :], mod_ref[0, 1:2, :]
            piece = piece_ref[0]
            if j == 0:
                sh1 = jnp.where(is_ctx, modc_ref[0, 0:1, :], sh1)
                sc1 = jnp.where(is_ctx, modc_ref[0, 1:2, :], sc1)
                piece = jnp.where(is_ctx, ctx_ref[0], first_ref[0])
            pieces.append((piece * (1.0 + sc1) + sh1).astype(BF16))
        xm = jnp.concatenate(pieces, axis=0)
        tm = xm.shape[0]

    def proj(lo, hi):
        return jnp.dot(xm, w_ref[:, lo:hi], preferred_element_type=F32)

    cos = cos_ref[...]
    slo = slo_ref[...]
    shi = shi_ref[...]

    cq = proj(C_CQ, C_CQ + Q_LORA)
    ckv_krg = proj(C_CKV, C_KRG + LANES)
    a = proj(C_CONV, C_CONV + CONV_CH)
    cqn = _rms_norm(cq, gq_ref[...]).astype(BF16)
    gt = proj(C_CONV + CONV_CH, C_CONV + 2 * CONV_CH)
    ckvn = _rms_norm(ckv_krg[:, :KV_LORA], gkv_ref[...]).astype(BF16)
    krg = ckv_krg[:, KV_LORA:]

    qf = jnp.dot(cqn, wuq_ref[...], preferred_element_type=F32) * Q_SCALE
    z_ref[0] = a * _sigmoid(gt)
    mq = proj(C_MQ, C_MQ + ML_WIDTH)
    kf = jnp.dot(ckvn, wuk_ref[...], preferred_element_type=F32)
    for h in range(MLA_HEADS):
        sl = slice(h * HEAD_PAD, (h + 1) * HEAD_PAD)
        q_ref[0, :, sl] = _rope(qf[:, sl], cos, slo, shi).astype(BF16)
    mq_ref[0] = mq.astype(BF16)

    mk = proj(C_MK, C_MK + ML_WIDTH)
    vf = jnp.dot(ckvn, wuv_ref[...], preferred_element_type=F32)
    lane = lax.broadcasted_iota(jnp.int32, (1, LANES), 1)
    is_rope_lane = jnp.logical_and(lane >= QK_NOPE, lane < QK_NOPE + QK_ROPE)
    kr = jnp.where(is_rope_lane, _rope(krg, cos, slo, shi), 0.0)
    for h in range(MLA_HEADS):
        sl = slice(h * HEAD_PAD, (h + 1) * HEAD_PAD)
        k_ref[0, :, sl] = (kf[:, sl] + kr).astype(BF16)
    mv = proj(C_MV, C_MV + ML_WIDTH)
    mo = proj(C_MO, C_MO + ML_WIDTH)
    v_ref[0] = vf.astype(BF16)

    mk_t = (mk * (ML_HEAD_DIM ** -0.5)).T
    g_t = (krg + bg_ref[...]).T
    gi_t = g_t[:N_GATE]
    gf_t = _log_sigmoid(g_t[N_GATE:2 * N_GATE])
    for cc in range(tm // CHUNK):
        cs = slice(cc * CHUNK, (cc + 1) * CHUNK)
        mkt_ref[0, cc] = mk_t[:, cs].astype(BF16)
        gir_ref[0, cc] = gi_t[:, cs]
        gfr_ref[0, cc] = gf_t[:, cs]
    mv_ref[0] = mv.astype(BF16)
    mo_ref[0] = mo


def _in_call(stream, mod, w_in_p, gq, gkv, wuq, wuk, wuv, bg, cos, slo, shi):
    split = isinstance(stream, tuple)
    if split:
        ctx, x = stream
        b, seq, d = x.shape
        ltot = CTX_LEN + seq
    else:
        b, ltot, d = stream.shape
    tm = _wide_tile(ltot)
    nt = ltot // tm
    nc = ltot // CHUNK
    ctx_row = mod.shape[0] - SUBLANES

    def tok(width):
        return pl.BlockSpec((1, tm, width), lambda i, t: (i, t, 0))

    def const2(shape):
        return pl.BlockSpec(shape, lambda i, t: (0, 0), pipeline_mode=pl.Buffered(1))

    tab = pl.BlockSpec((tm, LANES), lambda i, t: (t, 0))
    grow = pl.BlockSpec((1, tm // CHUNK, N_GATE, CHUNK), lambda i, t: (i, t, 0, 0))
    outs = [
        (jax.ShapeDtypeStruct((b, ltot, MLA_HEADS * HEAD_PAD), BF16), tok(MLA_HEADS * HEAD_PAD)),
        (jax.ShapeDtypeStruct((b, ltot, MLA_HEADS * HEAD_PAD), BF16), tok(MLA_HEADS * HEAD_PAD)),
        (jax.ShapeDtypeStruct((b, ltot, MLA_HEADS * V_HEAD), BF16), tok(MLA_HEADS * V_HEAD)),
        (jax.ShapeDtypeStruct((b, ltot, CONV_CH), F32), tok(CONV_CH)),
        (jax.ShapeDtypeStruct((b, ltot, ML_WIDTH), BF16), tok(ML_WIDTH)),
        (jax.ShapeDtypeStruct((b, nc, ML_WIDTH, CHUNK), BF16),
         pl.BlockSpec((1, tm // CHUNK, ML_WIDTH, CHUNK), lambda i, t: (i, t, 0, 0))),
        (jax.ShapeDtypeStruct((b, ltot, ML_WIDTH), BF16), tok(ML_WIDTH)),
        (jax.ShapeDtypeStruct((b, ltot, ML_WIDTH), F32), tok(ML_WIDTH)),
        (jax.ShapeDtypeStruct((b, nc, N_GATE, CHUNK), F32), grow),
        (jax.ShapeDtypeStruct((b, nc, N_GATE, CHUNK), F32), grow),
    ]
    if split:
        ppt = tm // TM
        x_args = [ctx] + [x] * ppt
        x_specs = [pl.BlockSpec((1, TM, d), lambda i, t: (i, 0, 0))] + [
            pl.BlockSpec((1, TM, d), lambda i, t, j=j: (i, jnp.maximum(t * ppt + j - 1, 0), 0)) for j in range(ppt)]
    else:
        x_args, x_specs = [stream], [tok(d)]
    return pl.pallas_call(
        functools.partial(_in_kernel, n_x=len(x_args)),
        out_shape=[o[0] for o in outs],
        grid=(b, nt),
        in_specs=x_specs + [
            pl.BlockSpec((1, 6, d), lambda i, t: (i, 0, 0)),
            pl.BlockSpec((1, 6, d), lambda i, t: (ctx_row, 0, 0)),
            const2(w_in_p.shape),
            const2(gq.shape), const2(gkv.shape),
            const2(wuq.shape), const2(wuk.shape), const2(wuv.shape),
            const2(bg.shape),
            tab, tab, tab,
        ],
        out_specs=[o[1] for o in outs],
        compiler_params=pltpu.CompilerParams(dimension_semantics=("parallel", "parallel"),
                                             vmem_limit_bytes=VMEM_LIMIT_BYTES),
        name="in_proj",
    )(*x_args, mod, mod, w_in_p, gq, gkv, wuq, wuk, wuv, bg, cos, slo, shi)


ATT_PAIRS = 4
SCORES_AHEAD = 2


def _attn_kernel(*refs, with_ctx, ltot):
    qa_ref, qb_ref, k_ref, v_ref = refs[:4]
    if with_ctx:
        octx_ref, o_ref, vaug_ref = refs[4:]
    else:
        o_ref, vaug_ref = refs[4:]
    pair_w = 2 * V_HEAD
    nh = 2 * ATT_PAIRS

    @pl.when(pl.program_id(2) == 0)
    def _():
        ones = jnp.ones((ltot, pair_w), BF16)
        for pp in range(ATT_PAIRS):
            vaug_ref[pp] = jnp.concatenate([v_ref[0, :, pp * pair_w:(pp + 1) * pair_w], ones], axis=1)

    def run(q_refs, nk, out_ref):
        chains = [(q_ref, h) for q_ref in q_refs for h in range(nh)]

        def scores(chain):
            q_ref, h = chain
            sl = slice(h * HEAD_PAD, (h + 1) * HEAD_PAD)
            return lax.dot_general(q_ref[0, :, sl], k_ref[0, :nk, sl], (((1,), (1,)), ((), ())),
                                   preferred_element_type=F32)

        def probs(s):
            return jnp.exp2((s - jnp.max(s, -1, keepdims=True)).astype(BF16))

        def weighted(chain, p):
            o = jnp.dot(p, vaug_ref[chain[1] // 2, :nk, :], preferred_element_type=F32)
            return o[:, :pair_w] / o[:, pair_w:]

        s_queue = [scores(chain) for chain in chains[:SCORES_AHEAD]]
        p_prev = None
        outs = []
        for c, chain in enumerate(chains):
            s_cur = s_queue.pop(0)
            if c + SCORES_AHEAD < len(chains):
                s_queue.append(scores(chains[c + SCORES_AHEAD]))
            p_cur = probs(s_cur)
            if p_prev is not None:
                outs.append(weighted(chains[c - 1], p_prev))
            p_prev = p_cur
        outs.append(weighted(chains[-1], p_prev))
        lane = lax.broadcasted_iota(jnp.int32, outs[0].shape, 1)
        for qi in range(len(q_refs)):
            for pp in range(ATT_PAIRS):
                pair = jnp.where(lane < V_HEAD, outs[qi * nh + 2 * pp], outs[qi * nh + 2 * pp + 1])
                out_ref[0, qi * TM:(qi + 1) * TM, pp * pair_w:(pp + 1) * pair_w] = pair.astype(out_ref.dtype)

    if with_ctx:
        t = pl.program_id(2)

        @pl.when(t == 0)
        def _():
            run([qa_ref], CTX_LEN, octx_ref)

        @pl.when(t > 0)
        def _():
            run([qa_ref, qb_ref], ltot, o_ref)
    else:
        run([qa_ref, qb_ref], ltot, o_ref)


def _attn_call(q, k, v, with_ctx):
    b, ltot, _ = q.shape
    ctx_tiles = CTX_LEN // TM
    seq = ltot - CTX_LEN
    assert ctx_tiles == 1 and seq % (2 * TM) == 0
    lead = 1 if with_ctx else 0
    nt = seq // (2 * TM) + lead
    ngroup = MLA_HEADS // (2 * ATT_PAIRS)
    qk_w = 2 * ATT_PAIRS * HEAD_PAD
    v_w = 2 * ATT_PAIRS * V_HEAD

    def q_tile(which):
        return lambda i, j, t: (i, jnp.where(t < lead, which, 2 * (t - lead) + ctx_tiles + which), j)

    lat_shape = jax.ShapeDtypeStruct((b, seq, MLA_HEADS * V_HEAD), BF16)
    lat_spec = pl.BlockSpec((1, 2 * TM, v_w), lambda i, j, t: (i, jnp.maximum(t - lead, 0), j))
    if with_ctx:
        out_shape = [jax.ShapeDtypeStruct((b, CTX_LEN, MLA_HEADS * V_HEAD), BF16), lat_shape]
        out_specs = [pl.BlockSpec((1, TM, v_w), lambda i, j, t: (i, 0, j)), lat_spec]
    else:
        out_shape, out_specs = lat_shape, lat_spec
    return pl.pallas_call(
        functools.partial(_attn_kernel, with_ctx=with_ctx, ltot=ltot),
        out_shape=out_shape,
        grid=(b, ngroup, nt),
        in_specs=[
            pl.BlockSpec((1, TM, qk_w), q_tile(0)),
            pl.BlockSpec((1, TM, qk_w), q_tile(1)),
            pl.BlockSpec((1, ltot, qk_w), lambda i, j, t: (i, 0, j)),
            pl.BlockSpec((1, ltot, v_w), lambda i, j, t: (i, 0, j)),
        ],
        out_specs=out_specs,
        scratch_shapes=[pltpu.VMEM((ATT_PAIRS, ltot, 4 * V_HEAD), BF16)],
        compiler_params=pltpu.CompilerParams(dimension_semantics=("parallel", "parallel", "arbitrary"),
                                             vmem_limit_bytes=VMEM_LIMIT_BYTES),
        name="mla_attn",
    )(q, q, k, v)


def _mlstm_kernel(q_ref, kt_ref, v_ref, mo_ref, gi_ref, gf_ref, g_ref,
                  o_ref, hf_ref, hb_ref, sn_ref, a_ref, r_ref, e_ref, tot_ref, gmax_ref, *, nchunk, nctx):
    t_ = CHUNK
    npair = ML_HEADS // 2
    row = lax.broadcasted_iota(jnp.int32, (t_, t_), 0)
    col = lax.broadcasted_iota(jnp.int32, (t_, t_), 1)
    causal_masks = (col <= row, col >= row)
    blockdiag = (row < ML_HEAD_DIM) == (col < ML_HEAD_DIM)
    half0 = lax.broadcasted_iota(jnp.int32, (1, LANES), 1) < ML_HEAD_DIM
    rowhalf0 = lax.broadcasted_iota(jnp.int32, (LANES, 1), 0) < ML_HEAD_DIM
    lane8 = lax.broadcasted_iota(jnp.int32, (1, t_), 1)
    n_parts = 3
    zpad = jnp.zeros((t_ - 2 * n_parts * N_GATE, t_), F32)
    ones_b = jnp.ones((t_, LANES), BF16)
    zeros_b = jnp.zeros((t_, LANES), BF16)
    half_ones = tuple(jnp.broadcast_to(jnp.where(half0, on, 1.0 - on), (t_, LANES)).astype(BF16) for on in (1.0, 0.0))

    r2 = lax.broadcasted_iota(jnp.int32, (t_, 3 * LANES), 0)
    l2 = lax.broadcasted_iota(jnp.int32, (t_, 3 * LANES), 1)
    blk, gate = r2 >> 3, r2 & 7
    lblk, lhalf = l2 >> 7, (l2 >> 6) & 1
    is_r_part = blk < n_parts
    is_b_part = jnp.logical_and(blk >= n_parts, blk < 2 * n_parts)
    consts = {}
    for d in range(2):
        for p in range(npair):
            c0 = d * ML_HEADS + 2 * p
            want_gate = jnp.where(lblk == 2, c0 + lhalf, c0 + lblk)
            want_part = jnp.where(lblk == 2, is_b_part.astype(jnp.int32), is_r_part.astype(jnp.int32)) == 1
            consts[d, p] = jnp.where(jnp.logical_and(want_part, gate == want_gate), 1.0, 0.0).astype(BF16)

    def split3(x):
        hi = x.astype(BF16).astype(F32)
        rem = x - hi
        mid = rem.astype(BF16).astype(F32)
        return [hi, mid, rem - mid]

    def lane_scan(x, op, fill, reverse):
        k = 1
        while k < t_:
            if reverse:
                shifted, valid = pltpu.roll(x, t_ - k, 1), lane8 < t_ - k
            else:
                shifted, valid = pltpu.roll(x, k, 1), lane8 >= k
            x = op(x, jnp.where(valid, shifted, fill))
            k *= 2
        return x

    nrow = nchunk * N_GATE
    is_fwd_row = (lax.broadcasted_iota(jnp.int32, (nrow, t_), 0) & (N_GATE - 1)) < ML_HEADS
    lf = gf_ref[0].reshape(nrow, t_) * LOG2_E
    lf_parts = jnp.concatenate([part.astype(BF16) for part in split3(lf)], axis=1)
    tri_up = jnp.where(row <= col, 1.0, 0.0).astype(BF16)
    tri_dn = jnp.where(row >= col, 1.0, 0.0).astype(BF16)
    b = jnp.where(is_fwd_row,
                  jnp.dot(lf_parts, jnp.concatenate([tri_up] * n_parts, axis=0), preferred_element_type=F32),
                  jnp.dot(lf_parts, jnp.concatenate([tri_dn] * n_parts, axis=0), preferred_element_type=F32))
    tot = jnp.sum(lf, axis=1, keepdims=True)
    r = gi_ref[0].reshape(nrow, t_) * LOG2_E - b
    rmax = jnp.where(is_fwd_row, lane_scan(r, jnp.maximum, -jnp.inf, False),
                     lane_scan(r, jnp.maximum, -jnp.inf, True))
    rlast = jnp.max(r, axis=1, keepdims=True)
    col_parts = split3(rmax) + split3(b)
    for c in range(nchunk):
        rows_c = slice(c * N_GATE, (c + 1) * N_GATE)
        packed = jnp.concatenate([part[rows_c] for part in col_parts] + [zpad], axis=0)
        a_ref[c] = packed.T.astype(BF16)
    r_ref[...] = r
    e_ref[...] = jnp.exp2(r - rlast)
    tot_ref[...] = jnp.broadcast_to(tot, (nrow, t_))
    gmax_ref[...] = jnp.broadcast_to(tot + rlast, (nrow, t_))

    def prep(d, c, m_prev):
        rows_c = slice(c * N_GATE, (c + 1) * N_GATE)
        tot_c = tot_ref[rows_c, :]
        gmax_c = gmax_ref[rows_c, :]
        m_new = jnp.maximum(tot_c + m_prev, gmax_c)
        decay = jnp.exp2(tot_c + m_prev - m_new)
        gamma = jnp.exp2(gmax_c - m_new)
        return a_ref[c], r_ref[rows_c, :], e_ref[rows_c, :], decay, gamma, m_new

    def stage_mxu(d, p, c, pre):
        a, _, e_row, _, _, _ = pre
        c0 = d * ML_HEADS + 2 * p
        c1 = c0 + 1
        sl = slice(p * LANES, (p + 1) * LANES)
        q = q_ref[0, c, :, sl]
        kt = kt_ref[0, c, sl, :]
        v = v_ref[0, c, :, sl]
        z = jnp.dot(a, consts[d, p], preferred_element_type=F32)
        kt_heads = jnp.concatenate([jnp.where(rowhalf0, kt, zeros_b), jnp.where(rowhalf0, zeros_b, kt)], axis=1)
        qk = jnp.dot(q, kt_heads, preferred_element_type=F32)
        wkt = kt * jnp.where(rowhalf0, e_row[c0:c0 + 1, :], e_row[c1:c1 + 1, :]).astype(BF16)
        upd = jnp.dot(wkt, jnp.concatenate([v, ones_b], axis=1), preferred_element_type=F32)
        sn = sn_ref[2 * p + d]
        qs = jnp.dot(q, sn.astype(BF16), preferred_element_type=F32)
        return z, qk, upd, sn, qs, v

    def stage_intra(d, p, r_rows, z, qk, v):
        c0 = d * ML_HEADS + 2 * p
        nd = jnp.zeros((t_, 2 * LANES), F32)
        for hh in range(2):
            hs = slice(hh * LANES, (hh + 1) * LANES)
            wts = jnp.exp2(jnp.where(causal_masks[d], r_rows[c0 + hh:c0 + hh + 1, :] - z[:, hs], -jnp.inf))
            v_h = jnp.where(half0, v, zeros_b) if hh == 0 else jnp.where(half0, zeros_b, v)
            nd = nd + jnp.dot((qk[:, hs] * wts).astype(BF16), jnp.concatenate([v_h, half_ones[hh]], axis=1),
                              preferred_element_type=F32)
        return nd

    def stage_out(d, p, c, pre, m_prev, h_ref, z, upd, sn, qs, nd):
        _, _, _, decay, gamma, _ = pre
        c0 = d * ML_HEADS + 2 * p
        c1 = c0 + 1
        sl = slice(p * LANES, (p + 1) * LANES)
        r_bc = jnp.where(half0, z[:, :LANES], z[:, LANES:2 * LANES])
        b_bc = z[:, 2 * LANES:]
        m_bc = jnp.where(half0, m_prev[c0:c0 + 1, :], m_prev[c1:c1 + 1, :])
        p_bc = jnp.maximum(m_bc, r_bc)
        alpha = jnp.exp2(r_bc - p_bc)
        beta = jnp.exp2(m_bc - p_bc)
        num = alpha * nd[:, :LANES] + beta * qs[:, :LANES]
        den = alpha * nd[:, LANES:] + beta * qs[:, LANES:]
        h_ref[c, :, sl] = num / jnp.maximum(jnp.abs(den), jnp.exp2(-(b_bc + p_bc)))

        dec_rows = jnp.where(rowhalf0, decay[c0:c0 + 1, :], decay[c1:c1 + 1, :])
        gam_rows = jnp.where(blockdiag, jnp.where(rowhalf0, gamma[c0:c0 + 1, :], gamma[c1:c1 + 1, :]), 0.0)
        sn_ref[2 * p + d] = jnp.concatenate([dec_rows * sn[:, :LANES] + gam_rows * upd[:, :LANES],
                                             dec_rows * sn[:, LANES:] + gam_rows * upd[:, LANES:]], axis=1)

    sn_ref[...] = jnp.zeros(sn_ref.shape, F32)

    def body(i, cb, carry):
        streams = [(d, p, c, h_ref) for d, c, h_ref in ((0, i, hf_ref), (1, cb, hb_ref)) for p in range(npair)]
        pres = {0: prep(0, i, carry[0]), 1: prep(1, cb, carry[1])}
        first = [stage_mxu(d, p, c, pres[d]) for d, p, c, _ in streams]
        intra = [stage_intra(d, p, pres[d][1], z, qk, v)
                 for (d, p, _, _), (z, qk, _, _, _, v) in zip(streams, first)]
        for (d, p, c, h_ref), (z, _, upd, sn, qs, _), nd in zip(streams, first, intra):
            stage_out(d, p, c, pres[d], carry[d], h_ref, z, upd, sn, qs, nd)
        return pres[0][-1], pres[1][-1]

    g = g_ref[...]
    head_shift = ML_HEAD_DIM.bit_length() - 1
    head_of_row = lax.broadcasted_iota(jnp.int32, (ML_WIDTH, ML_WIDTH), 0) >> head_shift
    head_of_col = lax.broadcasted_iota(jnp.int32, (ML_WIDTH, ML_WIDTH), 1) >> head_shift
    avg = jnp.where(head_of_row == head_of_col, 1.0 / ML_HEAD_DIM, 0.0).astype(BF16)
    avg2 = jnp.concatenate([avg, avg], axis=0)

    def head_mean(x):
        hi = x.astype(BF16)
        lo = (x - hi.astype(F32)).astype(BF16)
        return jnp.dot(jnp.concatenate([hi, lo], axis=1), avg2, preferred_element_type=F32)

    def fin(c):
        hh = (hf_ref[c] + hb_ref[c]) * _sigmoid(mo_ref[0, c])
        dl = hh - head_mean(hh)
        var = head_mean(dl * dl)
        o_ref[0, c] = (dl * lax.rsqrt(var + LN_EPS) * g).astype(o_ref.dtype)

    carry = (jnp.zeros((N_GATE, t_), F32),) * 2
    for i in range(nchunk):
        carry = body(i, nctx - 1 - i if i < nctx else nchunk - 1 + nctx - i, carry)
    for c in range(nchunk):
        fin(c)


def _mlstm_call(mq, mkt, mv, mo, gir, gfr, g):
    b, ltot, w = mq.shape
    nc = ltot // CHUNK
    r4 = lambda a: a.reshape(b, nc, CHUNK, a.shape[-1])
    blk = lambda rows, width: pl.BlockSpec((1, nc, rows, width), lambda i: (i, 0, 0, 0))
    out = pl.pallas_call(
        functools.partial(_mlstm_kernel, nchunk=nc, nctx=CTX_LEN // CHUNK),
        out_shape=jax.ShapeDtypeStruct((b, nc, CHUNK, w), BF16),
        grid=(b,),
        in_specs=[blk(CHUNK, w), blk(w, CHUNK), blk(CHUNK, w), blk(CHUNK, w),
                  blk(N_GATE, CHUNK), blk(N_GATE, CHUNK),
                  pl.BlockSpec((1, w), lambda i: (0, 0))],
        out_specs=blk(CHUNK, w),
        scratch_shapes=[pltpu.VMEM((nc, CHUNK, w), F32), pltpu.VMEM((nc, CHUNK, w), F32),
                        pltpu.VMEM((2 * (ML_HEADS // 2), LANES, 2 * LANES), F32),
                        pltpu.VMEM((nc, CHUNK, LANES), BF16), pltpu.VMEM((nc * N_GATE, CHUNK), F32),
                        pltpu.VMEM((nc * N_GATE, CHUNK), F32), pltpu.VMEM((nc * N_GATE, CHUNK), F32),
                        pltpu.VMEM((nc * N_GATE, CHUNK), F32)],
        compiler_params=pltpu.CompilerParams(dimension_semantics=("parallel",),
                                             vmem_limit_bytes=VMEM_LIMIT_BYTES),
        name="mlstm",
    )(r4(mq), mkt, r4(mv), r4(mo), gir, gfr, g)
    return out.reshape(b, ltot, w)


def _post_kernel(*refs, t0, nt_all, split):
    if split:
        ctx_ref, x_ref, mod_ref, attc_ref, attl_ref = refs[:5]
        is_ctx = pl.program_id(1) == 0
        resid = jnp.where(is_ctx, ctx_ref[0], x_ref[0])
        att = jnp.where(is_ctx, attc_ref[0], attl_ref[0])
    else:
        x_ref, mod_ref, att_ref = refs[:3]
        resid = x_ref[0]
        att = att_ref[0]
    (z_ref, zp_ref, zn_ref, ml_ref, cw_ref, cb_ref, cg_ref, cbb_ref,
     wo_ref, g1_ref, b1_ref, o_ref, zbuf_ref, zsh_ref) = refs[5 if split else 3:]
    t = pl.program_id(1) + t0
    n_att = MLA_HEADS * V_HEAD
    y = jnp.dot(att, wo_ref[0:n_att, :], preferred_element_type=F32)
    y = y + jnp.dot(ml_ref[0], wo_ref[n_att + CONV_CH:, :], preferred_element_type=F32)
    left_ok = t >= 2
    right_ok = jnp.logical_and(t >= 1, t <= nt_all - 2)
    zbuf_ref[0:HALO, :] = jnp.where(left_ok, zp_ref[0], 0.0)
    zbuf_ref[HALO:HALO + TM, :] = z_ref[0]
    zbuf_ref[HALO + TM:, :] = jnp.where(right_ok, zn_ref[0], 0.0)
    off = HALO - CONV_WIDTH // 2
    span = TM + ((off + CONV_WIDTH - 1) // SUBLANES) * SUBLANES
    for s in range(SUBLANES):
        zsh_ref[s] = zbuf_ref[s:s + span, :]
    acc = jnp.zeros((TM, CONV_CH), F32)
    for j in range(CONV_WIDTH):
        s, a = (off + j) % SUBLANES, ((off + j) // SUBLANES) * SUBLANES
        acc = acc + cw_ref[j:j + 1, :] * zsh_ref[s, a:a + TM, :]
    cv = _layer_norm(acc + cb_ref[...], cg_ref[...], cbb_ref[...])
    cv = cv * _sigmoid(cv)

    y = y + jnp.dot(cv.astype(BF16), wo_ref[n_att:n_att + CONV_CH, :], preferred_element_type=F32)
    g1 = mod_ref[0, 2:3, :]
    o_ref[0] = _layer_norm(DEEPNORM_ALPHA * resid + g1 * y, g1_ref[...], b1_ref[...])


def _post_call(stream, mod, att, z, ml, cw, cb, cg, cbb, wo, g1, b1, t0):
    split = isinstance(stream, tuple)
    if split:
        ctx, x = stream
        b, seq, d = x.shape
        ltot = CTX_LEN + seq
    else:
        b, ltot, d = stream.shape
    nt_all = ltot // TM
    nt = nt_all - t0
    ctx_row = mod.shape[0] - SUBLANES
    hpt = TM // HALO
    nhalo = ltot // HALO

    def tok(width, off):
        return pl.BlockSpec((1, TM, width), lambda i, t: (i, t + off, 0))

    def const2(shape):
        return pl.BlockSpec(shape, lambda i, t: (0, 0))

    def pieces(width):
        return [pl.BlockSpec((1, TM, width), lambda i, t: (i, 0, 0)),
                pl.BlockSpec((1, TM, width), lambda i, t: (i, jnp.maximum(t - 1, 0), 0))]

    if split:
        assert t0 == 0 and isinstance(att, (tuple, list))
        x_args, x_specs = [ctx, x], pieces(d)
        att_args, att_specs = list(att), pieces(att[0].shape[-1])
    else:
        x_args, x_specs = [stream], [tok(d, t0)]
        att_args, att_specs = [att], [tok(att.shape[-1], 0)]
    return pl.pallas_call(
        functools.partial(_post_kernel, t0=t0, nt_all=nt_all, split=split),
        out_shape=jax.ShapeDtypeStruct((b, nt * TM, d), F32),
        grid=(b, nt),
        in_specs=x_specs + [
            pl.BlockSpec((1, 6, d), lambda i, t: (jnp.where(t + t0 == 0, ctx_row, i), 0, 0)),
        ] + att_specs + [
            tok(CONV_CH, t0),
            pl.BlockSpec((1, HALO, CONV_CH), lambda i, t: (i, jnp.maximum((t + t0) * hpt - 1, 0), 0)),
            pl.BlockSpec((1, HALO, CONV_CH), lambda i, t: (i, jnp.minimum((t + t0 + 1) * hpt, nhalo - 1), 0)),
            tok(ML_WIDTH, t0),
            const2(cw.shape), const2(cb.shape), const2(cg.shape), const2(cbb.shape),
            const2(wo.shape), const2(g1.shape), const2(b1.shape),
        ],
        out_specs=tok(d, 0),
        scratch_shapes=[pltpu.VMEM((TM + 2 * HALO, CONV_CH), F32),
                        pltpu.VMEM((SUBLANES, TM + 2 * HALO - SUBLANES, CONV_CH), F32)],
        compiler_params=pltpu.CompilerParams(dimension_semantics=("parallel", "parallel"),
                                             vmem_limit_bytes=VMEM_LIMIT_BYTES),
        name="out_proj",
    )(*x_args, mod, *att_args, z, z, z, ml, cw, cb, cg, cbb, wo, g1, b1)


FF_CHUNK = 1024


def _mlp_kernel(x_ref, mod_ref, modc_ref, w1_ref, b1_ref, w2_ref, b2_ref, g_ref, b_ref, o_ref, *, ctx_rows):
    x = x_ref[0]
    sh2 = _mod_rows(mod_ref, modc_ref, 3, x.shape[0], ctx_rows)
    sc2 = _mod_rows(mod_ref, modc_ref, 4, x.shape[0], ctx_rows)
    g2 = _mod_rows(mod_ref, modc_ref, 5, x.shape[0], ctx_rows)
    u = (x * (1.0 + sc2) + sh2).astype(BF16)
    def hidden(c):
        sl = slice(c * FF_CHUNK, (c + 1) * FF_CHUNK)
        return jnp.dot(u, w1_ref[:, sl], preferred_element_type=F32) + b1_ref[:, sl]

    acc = jnp.zeros(x.shape, F32)
    n_chunk = D_FF // FF_CHUNK
    pre = hidden(0)
    for c in range(n_chunk):
        cur = pre
        if c + 1 < n_chunk:
            pre = hidden(c + 1)
        h = jnp.maximum(cur, 0.0)
        acc = acc + jnp.dot((h * h).astype(BF16), w2_ref[c * FF_CHUNK:(c + 1) * FF_CHUNK, :],
                            preferred_element_type=F32)
    o_ref[0] = _layer_norm(DEEPNORM_ALPHA * x + g2 * (acc + b2_ref[...]), g_ref[...], b_ref[...])


def _mlp_call(x1, mod, w1, b1, w2, b2, g, bb, with_ctx):
    b, ln, d = x1.shape
    tm = _wide_tile(ln)
    nt = ln // tm
    ctx_row = mod.shape[0] - SUBLANES

    def const2(shape):
        return pl.BlockSpec(shape, lambda i, t: (0, 0), pipeline_mode=pl.Buffered(1))

    return pl.pallas_call(
        functools.partial(_mlp_kernel, ctx_rows=CTX_LEN if with_ctx else 0),
        out_shape=jax.ShapeDtypeStruct((b, ln, d), F32),
        grid=(b, nt),
        in_specs=[
            pl.BlockSpec((1, tm, d), lambda i, t: (i, t, 0)),
            pl.BlockSpec((1, 6, d), lambda i, t: (i, 0, 0)),
            pl.BlockSpec((1, 6, d), lambda i, t: (ctx_row, 0, 0)),
            const2(w1.shape), const2(b1.shape), const2(w2.shape), const2(b2.shape),
            const2(g.shape), const2(bb.shape),
        ],
        out_specs=pl.BlockSpec((1, tm, d), lambda i, t: (i, t, 0)),
        compiler_params=pltpu.CompilerParams(dimension_semantics=("parallel", "parallel"),
                                             vmem_limit_bytes=VMEM_LIMIT_BYTES),
        name="mlp",
    )(x1, mod, mod, w1, b1, w2, b2, g, bb)


def _rope_tables(seq):
    half = QK_ROPE // 2
    nf = half // 2
    pos = np.arange(seq)
    inv_freq = ROPE_THETA ** (-np.arange(nf, dtype=np.float32) / nf)
    cos = np.ones((CTX_LEN + seq, LANES), np.float32)
    slo = np.zeros((CTX_LEN + seq, LANES), np.float32)
    shi = np.zeros((CTX_LEN + seq, LANES), np.float32)
    for part, p in enumerate((pos // GRID_W, pos % GRID_W)):
        ang = p.astype(np.float32)[:, None] * inv_freq[None, :]
        c, s = np.cos(ang), np.sin(ang)
        base = QK_NOPE + part * half
        cos[CTX_LEN:, base:base + nf] = c
        cos[CTX_LEN:, base + nf:base + half] = c
        slo[CTX_LEN:, base:base + nf] = -s
        shi[CTX_LEN:, base + nf:base + half] = s
    return jnp.asarray(cos), jnp.asarray(slo), jnp.asarray(shi)


def _pad_in_weights(w_in):
    d = w_in.shape[0]
    offs = np.cumsum([0, Q_LORA, KV_LORA, QK_ROPE, 2 * CONV_CH, ML_WIDTH, ML_WIDTH, ML_WIDTH, ML_WIDTH])
    o_cq, o_ckv, o_kr, o_conv, o_mq, o_mk, o_mv, o_mo, o_mg = [int(o) for o in offs]
    z = lambda n: jnp.zeros((d, n), w_in.dtype)
    mg = w_in[:, o_mg:o_mg + 4 * ML_HEADS]
    h = ML_HEADS
    gi = jnp.concatenate([mg[:, 0:h], mg[:, 2 * h:3 * h]], 1)
    gf = jnp.concatenate([mg[:, h:2 * h], mg[:, 3 * h:4 * h]], 1)
    cols = [
        w_in[:, o_cq:o_ckv], w_in[:, o_ckv:o_kr],
        gi, gf, z(QK_NOPE - 2 * N_GATE), w_in[:, o_kr:o_conv], z(LANES - QK_NOPE - QK_ROPE),
        w_in[:, o_conv:o_mg],
    ]
    return jnp.concatenate(cols, 1).astype(BF16)


def _pad_gate_bias(b_gates):
    h = ML_HEADS
    pad = jnp.zeros((LANES - 2 * N_GATE,), b_gates.dtype)
    return jnp.concatenate([b_gates[0:h], b_gates[2 * h:3 * h], b_gates[h:2 * h], b_gates[3 * h:4 * h], pad])[None, :]


def _pad_mla_weights(w_uq, w_ukv):
    dq = QK_NOPE + QK_ROPE
    wq = w_uq.reshape(Q_LORA, MLA_HEADS, dq)
    wq = jnp.pad(wq, ((0, 0), (0, 0), (0, HEAD_PAD - dq))).reshape(Q_LORA, MLA_HEADS * HEAD_PAD)
    wkv = w_ukv.reshape(KV_LORA, MLA_HEADS, QK_NOPE + V_HEAD)
    wk = jnp.pad(wkv[:, :, :QK_NOPE], ((0, 0), (0, 0), (0, HEAD_PAD - QK_NOPE)))
    wk = wk.reshape(KV_LORA, MLA_HEADS * HEAD_PAD)
    wv = wkv[:, :, QK_NOPE:].reshape(KV_LORA, MLA_HEADS * V_HEAD)
    return wq.astype(BF16), wk.astype(BF16), wv.astype(BF16)


def kernel(x, c, ctx, c_ctx, w_ada, b_ada, w_in, g_qn, w_uq, g_kvn, w_ukv, conv_w, conv_b, conv_ln_g, conv_ln_b, b_gates, ml_norm_g, w_out, ln1_g, ln1_b, w_mlp1, b_mlp1, w_mlp2, b_mlp2, ln2_g, ln2_b):
    b, seq, d = x.shape
    depth = w_in.shape[0]
    assert ctx.shape[1] == CTX_LEN == TM and seq % TM == 0 and d == D_MODEL and depth == DEPTH
    row = lambda a: a[None, :]

    cc = jnp.concatenate([c, c_ctx[None, :], jnp.zeros((SUBLANES - 1, d), c.dtype)], 0)
    mod_all = _ada_call(cc, w_ada, b_ada).reshape(depth, cc.shape[0], 6, d)
    cos, slo, shi = _rope_tables(seq)

    xx = (ctx, x)
    for l in range(depth):
        last = l == depth - 1
        t0 = 1 if last else 0
        mod = mod_all[l]
        wq, wk, wv = _pad_mla_weights(w_uq[l], w_ukv[l])
        q, k, v, z, mq, mkt, mv, mo, gir, gfr = _in_call(
            xx, mod, _pad_in_weights(w_in[l]), row(g_qn[l]), row(g_kvn[l]), wq, wk, wv,
            _pad_gate_bias(b_gates[l]), cos, slo, shi)
        att = _attn_call(q, k, v, with_ctx=not last)
        ml = _mlstm_call(mq, mkt, mv, mo, gir, gfr, row(ml_norm_g[l]))
        x1 = _post_call(xx, mod, att, z, ml, conv_w[l], row(conv_b[l]), row(conv_ln_g[l]), row(conv_ln_b[l]),
                        w_out[l].astype(BF16), row(ln1_g[l]), row(ln1_b[l]), t0)
        xx = _mlp_call(x1, mod, w_mlp1[l].astype(BF16), row(b_mlp1[l]), w_mlp2[l].astype(BF16),
                       row(b_mlp2[l]), row(ln2_g[l]), row(ln2_b[l]), with_ctx=not last)
    return xx
```

```python
import functools

import numpy as np
import jax
import jax.numpy as jnp
from jax import lax
from jax.experimental import pallas as pl
from jax.experimental.pallas import tpu as pltpu

F32 = jnp.float32
BF16 = jnp.bfloat16

LANES = 128
SUBLANES = 8
VMEM_LIMIT_BYTES = 56 * 1024 * 1024

D_MODEL = 1024
GRID_W = 64
CTX_LEN = 256
MLA_HEADS = 8
QK_NOPE = 64
QK_ROPE = 32
V_HEAD = 64
Q_LORA = 256
KV_LORA = 128
CONV_CH = 256
CONV_WIDTH = 31
ML_HEADS = 4
ML_HEAD_DIM = 64
ML_WIDTH = ML_HEADS * ML_HEAD_DIM
CHUNK = 128
D_FF = 4 * D_MODEL
ROPE_THETA = 10000.0
LN_EPS = 1e-5
RMS_EPS = 1e-6
DEPTH = 2
DEEPNORM_ALPHA = (2 * DEPTH) ** 0.25

TM = 256
HALO = 16
HEAD_PAD = LANES

C_CQ = 0
C_CKV = C_CQ + Q_LORA
C_KRG = C_CKV + KV_LORA
C_CONV = C_KRG + LANES
C_MQ = C_CONV + 2 * CONV_CH
C_MK = C_MQ + ML_WIDTH
C_MV = C_MK + ML_WIDTH
C_MO = C_MV + ML_WIDTH
N_GATE = 2 * ML_HEADS
ROPE_PAIR_DIST = QK_ROPE // 4
WIDE_TILES = (768, 512)
LOG2_E = float(np.log2(np.e))
Q_SCALE = (QK_NOPE + QK_ROPE) ** -0.5 * LOG2_E


def _wide_tile(rows):
    for tile in WIDE_TILES + (TM,):
        if rows % tile == 0:
            return tile
    raise ValueError(f"no token tile divides {rows}")


def _layer_norm(v, g, b):
    mu = jnp.mean(v, -1, keepdims=True)
    d = v - mu
    var = jnp.mean(d * d, -1, keepdims=True)
    return d * lax.rsqrt(var + LN_EPS) * g + b


def _rms_norm(v, g):
    return v * lax.rsqrt(jnp.mean(v * v, -1, keepdims=True) + RMS_EPS) * g


def _sigmoid(v):
    return 1.0 / (1.0 + jnp.exp(-v))


def _log_sigmoid(v):
    return jnp.minimum(v, 0.0) - jnp.log1p(jnp.exp(-jnp.abs(v)))


def _rope(t, cos, sin_lo, sin_hi):
    return (t * cos + pltpu.roll(t, LANES - ROPE_PAIR_DIST, 1) * sin_lo
            + pltpu.roll(t, ROPE_PAIR_DIST, 1) * sin_hi)


def _ada_kernel(c_ref, w_ref, b_ref, o_ref):
    c = c_ref[...]
    s = (c * _sigmoid(c)).astype(BF16)
    o_ref[0] = jnp.dot(s, w_ref[0].astype(BF16), preferred_element_type=F32) + b_ref[0]


def _ada_call(cc, w_ada, b_ada):
    depth, d, n6 = w_ada.shape
    rows = cc.shape[0]
    nblk = n6 // d
    return pl.pallas_call(
        _ada_kernel,
        out_shape=jax.ShapeDtypeStruct((depth, rows, n6), F32),
        grid=(depth, nblk),
        in_specs=[
            pl.BlockSpec((rows, d), lambda l, n: (0, 0)),
            pl.BlockSpec((1, d, d), lambda l, n: (l, 0, n)),
            pl.BlockSpec((1, 1, d), lambda l, n: (l, 0, n)),
        ],
        out_specs=pl.BlockSpec((1, rows, d), lambda l, n: (l, 0, n)),
        compiler_params=pltpu.CompilerParams(dimension_semantics=("parallel", "parallel")),
        name="ada_mod",
    )(cc, w_ada, b_ada.reshape(depth, 1, n6))


def _mod_rows(mod_ref, modc_ref, idx, rows, ctx_rows):
    m = mod_ref[0, idx:idx + 1, :]
    if ctx_rows == 0:
        return m
    is_ctx = jnp.logical_and(lax.broadcasted_iota(jnp.int32, (rows, 1), 0) < ctx_rows, pl.program_id(1) == 0)
    return jnp.where(is_ctx, modc_ref[0, idx:idx + 1, :], m)


def _in_kernel(*refs, n_x):
    x_refs = refs[:n_x]
    (mod_ref, modc_ref, w_ref, gq_ref, gkv_ref, wuq_ref, wuk_ref, wuv_ref, bg_ref, cos_ref, slo_ref, shi_ref,
     q_ref, k_ref, v_ref, z_ref, mq_ref, mkt_ref, mv_ref, mo_ref, gir_ref, gfr_ref) = refs[n_x:]
    if n_x == 1:
        x = x_refs[0][0]
        tm = x.shape[0]
        sh1 = _mod_rows(mod_ref, modc_ref, 0, tm, CTX_LEN)
        sc1 = _mod_rows(mod_ref, modc_ref, 1, tm, CTX_LEN)
        xm = (x * (1.0 + sc1) + sh1).astype(BF16)
    else:
        ctx_ref, first_ref = x_refs[0], x_refs[1]
        is_ctx = pl.program_id(1) == 0
        pieces = []
        for j, piece_ref in enumerate(x_refs[1:]):
            sh1, sc1 = mod_ref[0, 0:1, :], mod_ref[0, 1:2, :]
            piece = piece_ref[0]
            if j == 0:
                sh1 = jnp.where(is_ctx, modc_ref[0, 0:1, :], sh1)
                sc1 = jnp.where(is_ctx, modc_ref[0, 1:2, :], sc1)
                piece = jnp.where(is_ctx, ctx_ref[0], first_ref[0])
            pieces.append((piece * (1.0 + sc1) + sh1).astype(BF16))
        xm = jnp.concatenate(pieces, axis=0)
        tm = xm.shape[0]

    def proj(lo, hi):
        return jnp.dot(xm, w_ref[:, lo:hi], preferred_element_type=F32)

    cos = cos_ref[...]
    slo = slo_ref[...]
    shi = shi_ref[...]

    cq = proj(C_CQ, C_CQ + Q_LORA)
    ckv_krg = proj(C_CKV, C_KRG + LANES)
    a = proj(C_CONV, C_CONV + CONV_CH)
    cqn = _rms_norm(cq, gq_ref[...]).astype(BF16)
    gt = proj(C_CONV + CONV_CH, C_CONV + 2 * CONV_CH)
    ckvn = _rms_norm(ckv_krg[:, :KV_LORA], gkv_ref[...]).astype(BF16)
    krg = ckv_krg[:, KV_LORA:]

    qf = jnp.dot(cqn, wuq_ref[...], preferred_element_type=F32) * Q_SCALE
    z_ref[0] = a * _sigmoid(gt)
    mq = proj(C_MQ, C_MQ + ML_WIDTH)
    kf = jnp.dot(ckvn, wuk_ref[...], preferred_element_type=F32)
    for h in range(MLA_HEADS):
        sl = slice(h * HEAD_PAD, (h + 1) * HEAD_PAD)
        q_ref[0, :, sl] = _rope(qf[:, sl], cos, slo, shi).astype(BF16)
    mq_ref[0] = mq.astype(BF16)

    mk = proj(C_MK, C_MK + ML_WIDTH)
    vf = jnp.dot(ckvn, wuv_ref[...], preferred_element_type=F32)
    lane = lax.broadcasted_iota(jnp.int32, (1, LANES), 1)
    is_rope_lane = jnp.logical_and(lane >= QK_NOPE, lane < QK_NOPE + QK_ROPE)
    kr = jnp.where(is_rope_lane, _rope(krg, cos, slo, shi), 0.0)
    for h in range(MLA_HEADS):
        sl = slice(h * HEAD_PAD, (h + 1) * HEAD_PAD)
        k_ref[0, :, sl] = (kf[:, sl] + kr).astype(BF16)
    mv = proj(C_MV, C_MV + ML_WIDTH)
    mo = proj(C_MO, C_MO + ML_WIDTH)
    v_ref[0] = vf.astype(BF16)

    mk_t = (mk * (ML_HEAD_DIM ** -0.5)).T
    g_t = (krg + bg_ref[...]).T
    gi_t = g_t[:N_GATE]
    gf_t = _log_sigmoid(g_t[N_GATE:2 * N_GATE])
    for cc in range(tm // CHUNK):
        cs = slice(cc * CHUNK, (cc + 1) * CHUNK)
        mkt_ref[0, cc] = mk_t[:, cs].astype(BF16)
        gir_ref[0, cc] = gi_t[:, cs]
        gfr_ref[0, cc] = gf_t[:, cs]
    mv_ref[0] = mv.astype(BF16)
    mo_ref[0] = mo


def _in_call(stream, mod, w_in_p, gq, gkv, wuq, wuk, wuv, bg, cos, slo, shi):
    split = isinstance(stream, tuple)
    if split:
        ctx, x = stream
        b, seq, d = x.shape
        ltot = CTX_LEN + seq
    else:
        b, ltot, d = stream.shape
    tm = _wide_tile(ltot)
    nt = ltot // tm
    nc = ltot // CHUNK
    ctx_row = mod.shape[0] - SUBLANES

    def tok(width):
        return pl.BlockSpec((1, tm, width), lambda i, t: (i, t, 0))

    def const2(shape):
        return pl.BlockSpec(shape, lambda i, t: (0, 0), pipeline_mode=pl.Buffered(1))

    tab = pl.BlockSpec((tm, LANES), lambda i, t: (t, 0))
    grow = pl.BlockSpec((1, tm // CHUNK, N_GATE, CHUNK), lambda i, t: (i, t, 0, 0))
    outs = [
        (jax.ShapeDtypeStruct((b, ltot, MLA_HEADS * HEAD_PAD), BF16), tok(MLA_HEADS * HEAD_PAD)),
        (jax.ShapeDtypeStruct((b, ltot, MLA_HEADS * HEAD_PAD), BF16), tok(MLA_HEADS * HEAD_PAD)),
        (jax.ShapeDtypeStruct((b, ltot, MLA_HEADS * V_HEAD), BF16), tok(MLA_HEADS * V_HEAD)),
        (jax.ShapeDtypeStruct((b, ltot, CONV_CH), F32), tok(CONV_CH)),
        (jax.ShapeDtypeStruct((b, ltot, ML_WIDTH), BF16), tok(ML_WIDTH)),
        (jax.ShapeDtypeStruct((b, nc, ML_WIDTH, CHUNK), BF16),
         pl.BlockSpec((1, tm // CHUNK, ML_WIDTH, CHUNK), lambda i, t: (i, t, 0, 0))),
        (jax.ShapeDtypeStruct((b, ltot, ML_WIDTH), BF16), tok(ML_WIDTH)),
        (jax.ShapeDtypeStruct((b, ltot, ML_WIDTH), F32), tok(ML_WIDTH)),
        (jax.ShapeDtypeStruct((b, nc, N_GATE, CHUNK), F32), grow),
        (jax.ShapeDtypeStruct((b, nc, N_GATE, CHUNK), F32), grow),
    ]
    if split:
        ppt = tm // TM
        x_args = [ctx] + [x] * ppt
        x_specs = [pl.BlockSpec((1, TM, d), lambda i, t: (i, 0, 0))] + [
            pl.BlockSpec((1, TM, d), lambda i, t, j=j: (i, jnp.maximum(t * ppt + j - 1, 0), 0)) for j in range(ppt)]
    else:
        x_args, x_specs = [stream], [tok(d)]
    return pl.pallas_call(
        functools.partial(_in_kernel, n_x=len(x_args)),
        out_shape=[o[0] for o in outs],
        grid=(b, nt),
        in_specs=x_specs + [
            pl.BlockSpec((1, 6, d), lambda i, t: (i, 0, 0)),
            pl.BlockSpec((1, 6, d), lambda i, t: (ctx_row, 0, 0)),
            const2(w_in_p.shape),
            const2(gq.shape), const2(gkv.shape),
            const2(wuq.shape), const2(wuk.shape), const2(wuv.shape),
            const2(bg.shape),
            tab, tab, tab,
        ],
        out_specs=[o[1] for o in outs],
        compiler_params=pltpu.CompilerParams(dimension_semantics=("parallel", "parallel"),
                                             vmem_limit_bytes=VMEM_LIMIT_BYTES),
        name="in_proj",
    )(*x_args, mod, mod, w_in_p, gq, gkv, wuq, wuk, wuv, bg, cos, slo, shi)


ATT_PAIRS = 4
SCORES_AHEAD = 2
Q_TILES = 4


def _attn_kernel(*refs, with_ctx, ltot):
    q_refs = list(refs[:Q_TILES])
    k_ref, v_ref = refs[Q_TILES:Q_TILES + 2]
    if with_ctx:
        octx_ref, o_ref, vaug_ref = refs[Q_TILES + 2:]
    else:
        o_ref, vaug_ref = refs[Q_TILES + 2:]
    pair_w = 2 * V_HEAD
    nh = 2 * ATT_PAIRS

    @pl.when(pl.program_id(2) == 0)
    def _():
        ones = jnp.ones((ltot, pair_w), BF16)
        for pp in range(ATT_PAIRS):
            vaug_ref[pp] = jnp.concatenate([v_ref[0, :, pp * pair_w:(pp + 1) * pair_w], ones], axis=1)

    def run(q_refs, nk, out_ref):
        chains = [(q_ref, h) for q_ref in q_refs for h in range(nh)]

        def scores(chain):
            q_ref, h = chain
            sl = slice(h * HEAD_PAD, (h + 1) * HEAD_PAD)
            return lax.dot_general(q_ref[0, :, sl], k_ref[0, :nk, sl], (((1,), (1,)), ((), ())),
                                   preferred_element_type=F32)

        def probs(s):
            return jnp.exp2((s - jnp.max(s, -1, keepdims=True)).astype(BF16))

        def weighted(chain, p):
            o = jnp.dot(p, vaug_ref[chain[1] // 2, :nk, :], preferred_element_type=F32)
            return o[:, :pair_w] / o[:, pair_w:]

        s_queue = [scores(chain) for chain in chains[:SCORES_AHEAD]]
        p_prev = None
        outs = []
        for c, chain in enumerate(chains):
            s_cur = s_queue.pop(0)
            if c + SCORES_AHEAD < len(chains):
                s_queue.append(scores(chains[c + SCORES_AHEAD]))
            p_cur = probs(s_cur)
            if p_prev is not None:
                outs.append(weighted(chains[c - 1], p_prev))
            p_prev = p_cur
        outs.append(weighted(chains[-1], p_prev))
        lane = lax.broadcasted_iota(jnp.int32, outs[0].shape, 1)
        for qi in range(len(q_refs)):
            for pp in range(ATT_PAIRS):
                pair = jnp.where(lane < V_HEAD, outs[qi * nh + 2 * pp], outs[qi * nh + 2 * pp + 1])
                out_ref[0, qi * TM:(qi + 1) * TM, pp * pair_w:(pp + 1) * pair_w] = pair.astype(out_ref.dtype)

    if with_ctx:
        t = pl.program_id(2)

        @pl.when(t == 0)
        def _():
            run(q_refs[:1], CTX_LEN, octx_ref)

        @pl.when(t > 0)
        def _():
            run(q_refs, ltot, o_ref)
    else:
        run(q_refs, ltot, o_ref)


def _attn_call(q, k, v, with_ctx):
    b, ltot, _ = q.shape
    ctx_tiles = CTX_LEN // TM
    seq = ltot - CTX_LEN
    assert ctx_tiles == 1 and seq % (Q_TILES * TM) == 0
    lead = 1 if with_ctx else 0
    nt = seq // (Q_TILES * TM) + lead
    ngroup = MLA_HEADS // (2 * ATT_PAIRS)
    qk_w = 2 * ATT_PAIRS * HEAD_PAD
    v_w = 2 * ATT_PAIRS * V_HEAD

    def q_tile(which):
        return lambda i, j, t: (i, jnp.where(t < lead, which, Q_TILES * (t - lead) + ctx_tiles + which), j)

    lat_shape = jax.ShapeDtypeStruct((b, seq, MLA_HEADS * V_HEAD), BF16)
    lat_spec = pl.BlockSpec((1, Q_TILES * TM, v_w), lambda i, j, t: (i, jnp.maximum(t - lead, 0), j))
    if with_ctx:
        out_shape = [jax.ShapeDtypeStruct((b, CTX_LEN, MLA_HEADS * V_HEAD), BF16), lat_shape]
        out_specs = [pl.BlockSpec((1, TM, v_w), lambda i, j, t: (i, 0, j)), lat_spec]
    else:
        out_shape, out_specs = lat_shape, lat_spec
    return pl.pallas_call(
        functools.partial(_attn_kernel, with_ctx=with_ctx, ltot=ltot),
        out_shape=out_shape,
        grid=(b, ngroup, nt),
        in_specs=[pl.BlockSpec((1, TM, qk_w), q_tile(which)) for which in range(Q_TILES)] + [
            pl.BlockSpec((1, ltot, qk_w), lambda i, j, t: (i, 0, j)),
            pl.BlockSpec((1, ltot, v_w), lambda i, j, t: (i, 0, j)),
        ],
        out_specs=out_specs,
        scratch_shapes=[pltpu.VMEM((ATT_PAIRS, ltot, 4 * V_HEAD), BF16)],
        compiler_params=pltpu.CompilerParams(dimension_semantics=("parallel", "parallel", "arbitrary"),
                                             vmem_limit_bytes=VMEM_LIMIT_BYTES),
        name="mla_attn",
    )(*[q] * Q_TILES, k, v)


def _mlstm_kernel(q_ref, kt_ref, v_ref, mo_ref, gi_ref, gf_ref, g_ref,
                  o_ref, hf_ref, hb_ref, sn_ref, a_ref, r_ref, e_ref, tot_ref, gmax_ref, *, nchunk, nctx):
    t_ = CHUNK
    npair = ML_HEADS // 2
    row = lax.broadcasted_iota(jnp.int32, (t_, t_), 0)
    col = lax.broadcasted_iota(jnp.int32, (t_, t_), 1)
    causal_masks = (col <= row, col >= row)
    blockdiag = (row < ML_HEAD_DIM) == (col < ML_HEAD_DIM)
    half0 = lax.broadcasted_iota(jnp.int32, (1, LANES), 1) < ML_HEAD_DIM
    rowhalf0 = lax.broadcasted_iota(jnp.int32, (LANES, 1), 0) < ML_HEAD_DIM
    lane8 = lax.broadcasted_iota(jnp.int32, (1, t_), 1)
    n_parts = 3
    zpad = jnp.zeros((t_ - 2 * n_parts * N_GATE, t_), F32)
    ones_b = jnp.ones((t_, LANES), BF16)
    zeros_b = jnp.zeros((t_, LANES), BF16)
    half_ones = tuple(jnp.broadcast_to(jnp.where(half0, on, 1.0 - on), (t_, LANES)).astype(BF16) for on in (1.0, 0.0))

    r2 = lax.broadcasted_iota(jnp.int32, (t_, 3 * LANES), 0)
    l2 = lax.broadcasted_iota(jnp.int32, (t_, 3 * LANES), 1)
    blk, gate = r2 >> 3, r2 & 7
    lblk, lhalf = l2 >> 7, (l2 >> 6) & 1
    is_r_part = blk < n_parts
    is_b_part = jnp.logical_and(blk >= n_parts, blk < 2 * n_parts)
    consts = {}
    for d in range(2):
        for p in range(npair):
            c0 = d * ML_HEADS + 2 * p
            want_gate = jnp.where(lblk == 2, c0 + lhalf, c0 + lblk)
            want_part = jnp.where(lblk == 2, is_b_part.astype(jnp.int32), is_r_part.astype(jnp.int32)) == 1
            consts[d, p] = jnp.where(jnp.logical_and(want_part, gate == want_gate), 1.0, 0.0).astype(BF16)

    def split3(x):
        hi = x.astype(BF16).astype(F32)
        rem = x - hi
        mid = rem.astype(BF16).astype(F32)
        return [hi, mid, rem - mid]

    def lane_scan(x, op, fill, reverse):
        k = 1
        while k < t_:
            if reverse:
                shifted, valid = pltpu.roll(x, t_ - k, 1), lane8 < t_ - k
            else:
                shifted, valid = pltpu.roll(x, k, 1), lane8 >= k
            x = op(x, jnp.where(valid, shifted, fill))
            k *= 2
        return x

    nrow = nchunk * N_GATE
    is_fwd_row = (lax.broadcasted_iota(jnp.int32, (nrow, t_), 0) & (N_GATE - 1)) < ML_HEADS
    lf = gf_ref[0].reshape(nrow, t_) * LOG2_E
    lf_parts = jnp.concatenate([part.astype(BF16) for part in split3(lf)], axis=1)
    tri_up = jnp.where(row <= col, 1.0, 0.0).astype(BF16)
    tri_dn = jnp.where(row >= col, 1.0, 0.0).astype(BF16)
    b = jnp.where(is_fwd_row,
                  jnp.dot(lf_parts, jnp.concatenate([tri_up] * n_parts, axis=0), preferred_element_type=F32),
                  jnp.dot(lf_parts, jnp.concatenate([tri_dn] * n_parts, axis=0), preferred_element_type=F32))
    tot = jnp.sum(lf, axis=1, keepdims=True)
    r = gi_ref[0].reshape(nrow, t_) * LOG2_E - b
    rmax = jnp.where(is_fwd_row, lane_scan(r, jnp.maximum, -jnp.inf, False),
                     lane_scan(r, jnp.maximum, -jnp.inf, True))
    rlast = jnp.max(r, axis=1, keepdims=True)
    col_parts = split3(rmax) + split3(b)
    for c in range(nchunk):
        rows_c = slice(c * N_GATE, (c + 1) * N_GATE)
        packed = jnp.concatenate([part[rows_c] for part in col_parts] + [zpad], axis=0)
        a_ref[c] = packed.T.astype(BF16)
    r_ref[...] = r
    e_ref[...] = jnp.exp2(r - rlast)
    tot_ref[...] = jnp.broadcast_to(tot, (nrow, t_))
    gmax_ref[...] = jnp.broadcast_to(tot + rlast, (nrow, t_))

    def prep(d, c, m_prev):
        rows_c = slice(c * N_GATE, (c + 1) * N_GATE)
        tot_c = tot_ref[rows_c, :]
        gmax_c = gmax_ref[rows_c, :]
        m_new = jnp.maximum(tot_c + m_prev, gmax_c)
        decay = jnp.exp2(tot_c + m_prev - m_new)
        gamma = jnp.exp2(gmax_c - m_new)
        return a_ref[c], r_ref[rows_c, :], e_ref[rows_c, :], decay, gamma, m_new

    def stage_mxu(d, p, c, pre):
        a, _, e_row, _, _, _ = pre
        c0 = d * ML_HEADS + 2 * p
        c1 = c0 + 1
        sl = slice(p * LANES, (p + 1) * LANES)
        q = q_ref[0, c, :, sl]
        kt = kt_ref[0, c, sl, :]
        v = v_ref[0, c, :, sl]
        z = jnp.dot(a, consts[d, p], preferred_element_type=F32)
        kt_heads = jnp.concatenate([jnp.where(rowhalf0, kt, zeros_b), jnp.where(rowhalf0, zeros_b, kt)], axis=1)
        qk = jnp.dot(q, kt_heads, preferred_element_type=F32)
        wkt = kt * jnp.where(rowhalf0, e_row[c0:c0 + 1, :], e_row[c1:c1 + 1, :]).astype(BF16)
        upd = jnp.dot(wkt, jnp.concatenate([v, ones_b], axis=1), preferred_element_type=F32)
        sn = sn_ref[2 * p + d]
        qs = jnp.dot(q, sn.astype(BF16), preferred_element_type=F32)
        return z, qk, upd, sn, qs, v

    def stage_intra(d, p, r_rows, z, qk, v):
        c0 = d * ML_HEADS + 2 * p
        nd = jnp.zeros((t_, 2 * LANES), F32)
        for hh in range(2):
            hs = slice(hh * LANES, (hh + 1) * LANES)
            wts = jnp.exp2(jnp.where(causal_masks[d], r_rows[c0 + hh:c0 + hh + 1, :] - z[:, hs], -jnp.inf))
            v_h = jnp.where(half0, v, zeros_b) if hh == 0 else jnp.where(half0, zeros_b, v)
            nd = nd + jnp.dot((qk[:, hs] * wts).astype(BF16), jnp.concatenate([v_h, half_ones[hh]], axis=1),
                              preferred_element_type=F32)
        return nd

    def stage_out(d, p, c, pre, m_prev, h_ref, z, upd, sn, qs, nd):
        _, _, _, decay, gamma, _ = pre
        c0 = d * ML_HEADS + 2 * p
        c1 = c0 + 1
        sl = slice(p * LANES, (p + 1) * LANES)
        r_bc = jnp.where(half0, z[:, :LANES], z[:, LANES:2 * LANES])
        b_bc = z[:, 2 * LANES:]
        m_bc = jnp.where(half0, m_prev[c0:c0 + 1, :], m_prev[c1:c1 + 1, :])
        p_bc = jnp.maximum(m_bc, r_bc)
        alpha = jnp.exp2(r_bc - p_bc)
        beta = jnp.exp2(m_bc - p_bc)
        num = alpha * nd[:, :LANES] + beta * qs[:, :LANES]
        den = alpha * nd[:, LANES:] + beta * qs[:, LANES:]
        h_ref[c, :, sl] = num / jnp.maximum(jnp.abs(den), jnp.exp2(-(b_bc + p_bc)))

        dec_rows = jnp.where(rowhalf0, decay[c0:c0 + 1, :], decay[c1:c1 + 1, :])
        gam_rows = jnp.where(blockdiag, jnp.where(rowhalf0, gamma[c0:c0 + 1, :], gamma[c1:c1 + 1, :]), 0.0)
        sn_ref[2 * p + d] = jnp.concatenate([dec_rows * sn[:, :LANES] + gam_rows * upd[:, :LANES],
                                             dec_rows * sn[:, LANES:] + gam_rows * upd[:, LANES:]], axis=1)

    sn_ref[...] = jnp.zeros(sn_ref.shape, F32)

    def body(i, cb, carry):
        streams = [(d, p, c, h_ref) for d, c, h_ref in ((0, i, hf_ref), (1, cb, hb_ref)) for p in range(npair)]
        pres = {0: prep(0, i, carry[0]), 1: prep(1, cb, carry[1])}
        first = [stage_mxu(d, p, c, pres[d]) for d, p, c, _ in streams]
        intra = [stage_intra(d, p, pres[d][1], z, qk, v)
                 for (d, p, _, _), (z, qk, _, _, _, v) in zip(streams, first)]
        for (d, p, c, h_ref), (z, _, upd, sn, qs, _), nd in zip(streams, first, intra):
            stage_out(d, p, c, pres[d], carry[d], h_ref, z, upd, sn, qs, nd)
        return pres[0][-1], pres[1][-1]

    g = g_ref[...]
    head_shift = ML_HEAD_DIM.bit_length() - 1
    head_of_row = lax.broadcasted_iota(jnp.int32, (ML_WIDTH, ML_WIDTH), 0) >> head_shift
    head_of_col = lax.broadcasted_iota(jnp.int32, (ML_WIDTH, ML_WIDTH), 1) >> head_shift
    avg = jnp.where(head_of_row == head_of_col, 1.0 / ML_HEAD_DIM, 0.0).astype(BF16)
    avg2 = jnp.concatenate([avg, avg], axis=0)

    def head_mean(x):
        hi = x.astype(BF16)
        lo = (x - hi.astype(F32)).astype(BF16)
        return jnp.dot(jnp.concatenate([hi, lo], axis=1), avg2, preferred_element_type=F32)

    def fin(c):
        hh = (hf_ref[c] + hb_ref[c]) * _sigmoid(mo_ref[0, c])
        dl = hh - head_mean(hh)
        var = head_mean(dl * dl)
        o_ref[0, c] = (dl * lax.rsqrt(var + LN_EPS) * g).astype(o_ref.dtype)

    carry = (jnp.zeros((N_GATE, t_), F32),) * 2
    for i in range(nchunk):
        carry = body(i, nctx - 1 - i if i < nctx else nchunk - 1 + nctx - i, carry)
    for c in range(nchunk):
        fin(c)


def _mlstm_call(mq, mkt, mv, mo, gir, gfr, g):
    b, ltot, w = mq.shape
    nc = ltot // CHUNK
    r4 = lambda a: a.reshape(b, nc, CHUNK, a.shape[-1])
    blk = lambda rows, width: pl.BlockSpec((1, nc, rows, width), lambda i: (i, 0, 0, 0))
    out = pl.pallas_call(
        functools.partial(_mlstm_kernel, nchunk=nc, nctx=CTX_LEN // CHUNK),
        out_shape=jax.ShapeDtypeStruct((b, nc, CHUNK, w), BF16),
        grid=(b,),
        in_specs=[blk(CHUNK, w), blk(w, CHUNK), blk(CHUNK, w), blk(CHUNK, w),
                  blk(N_GATE, CHUNK), blk(N_GATE, CHUNK),
                  pl.BlockSpec((1, w), lambda i: (0, 0))],
        out_specs=blk(CHUNK, w),
        scratch_shapes=[pltpu.VMEM((nc, CHUNK, w), F32), pltpu.VMEM((nc, CHUNK, w), F32),
                        pltpu.VMEM((2 * (ML_HEADS // 2), LANES, 2 * LANES), F32),
                        pltpu.VMEM((nc, CHUNK, LANES), BF16), pltpu.VMEM((nc * N_GATE, CHUNK), F32),
                        pltpu.VMEM((nc * N_GATE, CHUNK), F32), pltpu.VMEM((nc * N_GATE, CHUNK), F32),
                        pltpu.VMEM((nc * N_GATE, CHUNK), F32)],
        compiler_params=pltpu.CompilerParams(dimension_semantics=("parallel",),
                                             vmem_limit_bytes=VMEM_LIMIT_BYTES),
        name="mlstm",
    )(r4(mq), mkt, r4(mv), r4(mo), gir, gfr, g)
    return out.reshape(b, ltot, w)


def _post_kernel(*refs, t0, nt_all, split):
    if split:
        ctx_ref, x_ref, mod_ref, attc_ref, attl_ref = refs[:5]
        is_ctx = pl.program_id(1) == 0
        resid = jnp.where(is_ctx, ctx_ref[0], x_ref[0])
        att = jnp.where(is_ctx, attc_ref[0], attl_ref[0])
    else:
        x_ref, mod_ref, att_ref = refs[:3]
        resid = x_ref[0]
        att = att_ref[0]
    (z_ref, zp_ref, zn_ref, ml_ref, cw_ref, cb_ref, cg_ref, cbb_ref,
     wo_ref, g1_ref, b1_ref, o_ref, zbuf_ref, zsh_ref) = refs[5 if split else 3:]
    t = pl.program_id(1) + t0
    n_att = MLA_HEADS * V_HEAD
    y = jnp.dot(att, wo_ref[0:n_att, :], preferred_element_type=F32)
    y = y + jnp.dot(ml_ref[0], wo_ref[n_att + CONV_CH:, :], preferred_element_type=F32)
    left_ok = t >= 2
    right_ok = jnp.logical_and(t >= 1, t <= nt_all - 2)
    zbuf_ref[0:HALO, :] = jnp.where(left_ok, zp_ref[0], 0.0)
    zbuf_ref[HALO:HALO + TM, :] = z_ref[0]
    zbuf_ref[HALO + TM:, :] = jnp.where(right_ok, zn_ref[0], 0.0)
    off = HALO - CONV_WIDTH // 2
    span = TM + ((off + CONV_WIDTH - 1) // SUBLANES) * SUBLANES
    for s in range(SUBLANES):
        zsh_ref[s] = zbuf_ref[s:s + span, :]
    acc = jnp.zeros((TM, CONV_CH), F32)
    for j in range(CONV_WIDTH):
        s, a = (off + j) % SUBLANES, ((off + j) // SUBLANES) * SUBLANES
        acc = acc + cw_ref[j:j + 1, :] * zsh_ref[s, a:a + TM, :]
    cv = _layer_norm(acc + cb_ref[...], cg_ref[...], cbb_ref[...])
    cv = cv * _sigmoid(cv)

    y = y + jnp.dot(cv.astype(BF16), wo_ref[n_att:n_att + CONV_CH, :], preferred_element_type=F32)
    g1 = mod_ref[0, 2:3, :]
    o_ref[0] = _layer_norm(DEEPNORM_ALPHA * resid + g1 * y, g1_ref[...], b1_ref[...])


def _post_call(stream, mod, att, z, ml, cw, cb, cg, cbb, wo, g1, b1, t0):
    split = isinstance(stream, tuple)
    if split:
        ctx, x = stream
        b, seq, d = x.shape
        ltot = CTX_LEN + seq
    else:
        b, ltot, d = stream.shape
    nt_all = ltot // TM
    nt = nt_all - t0
    ctx_row = mod.shape[0] - SUBLANES
    hpt = TM // HALO
    nhalo = ltot // HALO

    def tok(width, off):
        return pl.BlockSpec((1, TM, width), lambda i, t: (i, t + off, 0))

    def const2(shape):
        return pl.BlockSpec(shape, lambda i, t: (0, 0))

    def pieces(width):
        return [pl.BlockSpec((1, TM, width), lambda i, t: (i, 0, 0)),
                pl.BlockSpec((1, TM, width), lambda i, t: (i, jnp.maximum(t - 1, 0), 0))]

    if split:
        assert t0 == 0 and isinstance(att, (tuple, list))
        x_args, x_specs = [ctx, x], pieces(d)
        att_args, att_specs = list(att), pieces(att[0].shape[-1])
    else:
        x_args, x_specs = [stream], [tok(d, t0)]
        att_args, att_specs = [att], [tok(att.shape[-1], 0)]
    return pl.pallas_call(
        functools.partial(_post_kernel, t0=t0, nt_all=nt_all, split=split),
        out_shape=jax.ShapeDtypeStruct((b, nt * TM, d), F32),
        grid=(b, nt),
        in_specs=x_specs + [
            pl.BlockSpec((1, 6, d), lambda i, t: (jnp.where(t + t0 == 0, ctx_row, i), 0, 0)),
        ] + att_specs + [
            tok(CONV_CH, t0),
            pl.BlockSpec((1, HALO, CONV_CH), lambda i, t: (i, jnp.maximum((t + t0) * hpt - 1, 0), 0)),
            pl.BlockSpec((1, HALO, CONV_CH), lambda i, t: (i, jnp.minimum((t + t0 + 1) * hpt, nhalo - 1), 0)),
            tok(ML_WIDTH, t0),
            const2(cw.shape), const2(cb.shape), const2(cg.shape), const2(cbb.shape),
            const2(wo.shape), const2(g1.shape), const2(b1.shape),
        ],
        out_specs=tok(d, 0),
        scratch_shapes=[pltpu.VMEM((TM + 2 * HALO, CONV_CH), F32),
                        pltpu.VMEM((SUBLANES, TM + 2 * HALO - SUBLANES, CONV_CH), F32)],
        compiler_params=pltpu.CompilerParams(dimension_semantics=("parallel", "parallel"),
                                             vmem_limit_bytes=VMEM_LIMIT_BYTES),
        name="out_proj",
    )(*x_args, mod, *att_args, z, z, z, ml, cw, cb, cg, cbb, wo, g1, b1)


FF_CHUNK = 1024


def _mlp_kernel(x_ref, mod_ref, modc_ref, w1_ref, b1_ref, w2_ref, b2_ref, g_ref, b_ref, o_ref, *, ctx_rows):
    x = x_ref[0]
    sh2 = _mod_rows(mod_ref, modc_ref, 3, x.shape[0], ctx_rows)
    sc2 = _mod_rows(mod_ref, modc_ref, 4, x.shape[0], ctx_rows)
    g2 = _mod_rows(mod_ref, modc_ref, 5, x.shape[0], ctx_rows)
    u = (x * (1.0 + sc2) + sh2).astype(BF16)
    def hidden(c):
        sl = slice(c * FF_CHUNK, (c + 1) * FF_CHUNK)
        return jnp.dot(u, w1_ref[:, sl], preferred_element_type=F32) + b1_ref[:, sl]

    acc = jnp.zeros(x.shape, F32)
    n_chunk = D_FF // FF_CHUNK
    pre = hidden(0)
    for c in range(n_chunk):
        cur = pre
        if c + 1 < n_chunk:
            pre = hidden(c + 1)
        h = jnp.maximum(cur, 0.0)
        acc = acc + jnp.dot((h * h).astype(BF16), w2_ref[c * FF_CHUNK:(c + 1) * FF_CHUNK, :],
                            preferred_element_type=F32)
    o_ref[0] = _layer_norm(DEEPNORM_ALPHA * x + g2 * (acc + b2_ref[...]), g_ref[...], b_ref[...])


def _mlp_call(x1, mod, w1, b1, w2, b2, g, bb, with_ctx):
    b, ln, d = x1.shape
    tm = _wide_tile(ln)
    nt = ln // tm
    ctx_row = mod.shape[0] - SUBLANES

    def const2(shape):
        return pl.BlockSpec(shape, lambda i, t: (0, 0), pipeline_mode=pl.Buffered(1))

    return pl.pallas_call(
        functools.partial(_mlp_kernel, ctx_rows=CTX_LEN if with_ctx else 0),
        out_shape=jax.ShapeDtypeStruct((b, ln, d), F32),
        grid=(b, nt),
        in_specs=[
            pl.BlockSpec((1, tm, d), lambda i, t: (i, t, 0)),
            pl.BlockSpec((1, 6, d), lambda i, t: (i, 0, 0)),
            pl.BlockSpec((1, 6, d), lambda i, t: (ctx_row, 0, 0)),
            const2(w1.shape), const2(b1.shape), const2(w2.shape), const2(b2.shape),
            const2(g.shape), const2(bb.shape),
        ],
        out_specs=pl.BlockSpec((1, tm, d), lambda i, t: (i, t, 0)),
        compiler_params=pltpu.CompilerParams(dimension_semantics=("parallel", "parallel"),
                                             vmem_limit_bytes=VMEM_LIMIT_BYTES),
        name="mlp",
    )(x1, mod, mod, w1, b1, w2, b2, g, bb)


def _rope_tables(seq):
    half = QK_ROPE // 2
    nf = half // 2
    pos = np.arange(seq)
    inv_freq = ROPE_THETA ** (-np.arange(nf, dtype=np.float32) / nf)
    cos = np.ones((CTX_LEN + seq, LANES), np.float32)
    slo = np.zeros((CTX_LEN + seq, LANES), np.float32)
    shi = np.zeros((CTX_LEN + seq, LANES), np.float32)
    for part, p in enumerate((pos // GRID_W, pos % GRID_W)):
        ang = p.astype(np.float32)[:, None] * inv_freq[None, :]
        c, s = np.cos(ang), np.sin(ang)
        base = QK_NOPE + part * half
        cos[CTX_LEN:, base:base + nf] = c
        cos[CTX_LEN:, base + nf:base + half] = c
        slo[CTX_LEN:, base:base + nf] = -s
        shi[CTX_LEN:, base + nf:base + half] = s
    return jnp.asarray(cos), jnp.asarray(slo), jnp.asarray(shi)


def _pad_in_weights(w_in):
    d = w_in.shape[0]
    offs = np.cumsum([0, Q_LORA, KV_LORA, QK_ROPE, 2 * CONV_CH, ML_WIDTH, ML_WIDTH, ML_WIDTH, ML_WIDTH])
    o_cq, o_ckv, o_kr, o_conv, o_mq, o_mk, o_mv, o_mo, o_mg = [int(o) for o in offs]
    z = lambda n: jnp.zeros((d, n), w_in.dtype)
    mg = w_in[:, o_mg:o_mg + 4 * ML_HEADS]
    h = ML_HEADS
    gi = jnp.concatenate([mg[:, 0:h], mg[:, 2 * h:3 * h]], 1)
    gf = jnp.concatenate([mg[:, h:2 * h], mg[:, 3 * h:4 * h]], 1)
    cols = [
        w_in[:, o_cq:o_ckv], w_in[:, o_ckv:o_kr],
        gi, gf, z(QK_NOPE - 2 * N_GATE), w_in[:, o_kr:o_conv], z(LANES - QK_NOPE - QK_ROPE),
        w_in[:, o_conv:o_mg],
    ]
    return jnp.concatenate(cols, 1).astype(BF16)


def _pad_gate_bias(b_gates):
    h = ML_HEADS
    pad = jnp.zeros((LANES - 2 * N_GATE,), b_gates.dtype)
    return jnp.concatenate([b_gates[0:h], b_gates[2 * h:3 * h], b_gates[h:2 * h], b_gates[3 * h:4 * h], pad])[None, :]


def _pad_mla_weights(w_uq, w_ukv):
    dq = QK_NOPE + QK_ROPE
    wq = w_uq.reshape(Q_LORA, MLA_HEADS, dq)
    wq = jnp.pad(wq, ((0, 0), (0, 0), (0, HEAD_PAD - dq))).reshape(Q_LORA, MLA_HEADS * HEAD_PAD)
    wkv = w_ukv.reshape(KV_LORA, MLA_HEADS, QK_NOPE + V_HEAD)
    wk = jnp.pad(wkv[:, :, :QK_NOPE], ((0, 0), (0, 0), (0, HEAD_PAD - QK_NOPE)))
    wk = wk.reshape(KV_LORA, MLA_HEADS * HEAD_PAD)
    wv = wkv[:, :, QK_NOPE:].reshape(KV_LORA, MLA_HEADS * V_HEAD)
    return wq.astype(BF16), wk.astype(BF16), wv.astype(BF16)


def kernel(x, c, ctx, c_ctx, w_ada, b_ada, w_in, g_qn, w_uq, g_kvn, w_ukv, conv_w, conv_b, conv_ln_g, conv_ln_b, b_gates, ml_norm_g, w_out, ln1_g, ln1_b, w_mlp1, b_mlp1, w_mlp2, b_mlp2, ln2_g, ln2_b):
    b, seq, d = x.shape
    depth = w_in.shape[0]
    assert ctx.shape[1] == CTX_LEN == TM and seq % TM == 0 and d == D_MODEL and depth == DEPTH
    row = lambda a: a[None, :]

    cc = jnp.concatenate([c, c_ctx[None, :], jnp.zeros((SUBLANES - 1, d), c.dtype)], 0)
    mod_all = _ada_call(cc, w_ada, b_ada).reshape(depth, cc.shape[0], 6, d)
    cos, slo, shi = _rope_tables(seq)

    xx = (ctx, x)
    for l in range(depth):
        last = l == depth - 1
        t0 = 1 if last else 0
        mod = mod_all[l]
        wq, wk, wv = _pad_mla_weights(w_uq[l], w_ukv[l])
        q, k, v, z, mq, mkt, mv, mo, gir, gfr = _in_call(
            xx, mod, _pad_in_weights(w_in[l]), row(g_qn[l]), row(g_kvn[l]), wq, wk, wv,
            _pad_gate_bias(b_gates[l]), cos, slo, shi)
        att = _attn_call(q, k, v, with_ctx=not last)
        ml = _mlstm_call(mq, mkt, mv, mo, gir, gfr, row(ml_norm_g[l]))
        x1 = _post_call(xx, mod, att, z, ml, conv_w[l], row(conv_b[l]), row(conv_ln_g[l]), row(conv_ln_b[l]),
                        w_out[l].astype(BF16), row(ln1_g[l]), row(ln1_b[l]), t0)
        xx = _mlp_call(x1, mod, w_mlp1[l].astype(BF16), row(b_mlp1[l]), w_mlp2[l].astype(BF16),
                       row(b_mlp2[l]), row(ln2_g[l]), row(ln2_b[l]), with_ctx=not last)
    return xx
```
